```python
import jax, jax.numpy as jnp
from jax import lax
import numpy as np

D_MODEL = 2048
BATCH = 8
SEQ = 2048
DEPTH = 2
DEC_BATCH = 128
DEC_SEQ = 8
PAST_LEN = 2048
PAGE_SIZE = 128

HEAD_DIM = 64
A_HEADS = 16
A_WIDTH = A_HEADS * HEAD_DIM
A_LORA_W = 64
A_LORA_A = 64
A_LORA_G = 128
A_PROJ = 3 * A_WIDTH + A_LORA_W + A_LORA_A + A_LORA_G
RWKV_GN_EPS = 64e-5
B_HEADS = 16
B_WIDTH = B_HEADS * HEAD_DIM
B_PROJ = 3 * B_WIDTH + B_HEADS
L0_PROJ = A_PROJ + B_PROJ
FOX_Q_BLOCK = 128
C_HEADS = 32
C_KV_HEADS = 8
C_WIDTH = C_HEADS * HEAD_DIM
C_KV_WIDTH = C_KV_HEADS * HEAD_DIM
L1_PROJ = C_WIDTH + 2 * C_KV_WIDTH
MOBA_BLOCK = 256
MOBA_TOPK = 3
MOBA_ROW_CHUNK = 32
D_FF = 5632
N_EXPERTS = 8
TOP_K_EXPERTS = 2
D_FF_EXPERT = 2816
NORM_EPS = 1e-6
POOL_NUM = 5
POOL_DEN = 4

kernel_name = 'hybrid_rwkv7_fox_moba_decode_step'


def rmsnorm(x, g):
    xf = x.astype(jnp.float32)
    y = xf * lax.rsqrt(jnp.mean(xf * xf, axis=-1, keepdims=True) + NORM_EPS)
    return (y * g.astype(jnp.float32)).astype(x.dtype)


def swiglu(x, w_gate, w_up, w_down):
    return (jax.nn.silu(x @ w_gate) * (x @ w_up)) @ w_down


def gather_pages(pool, page_table):
    rows = pool[page_table]
    return rows.reshape((page_table.shape[0], page_table.shape[1] * pool.shape[1]) + pool.shape[2:])


def rwkv7_time_mix(p, S0, shift0, mu, w0, w2, a0, a2, g2, k_k, k_a, r_k, ln_w, ln_b):
    B, T, _ = p.shape
    p_prev = jnp.concatenate([shift0[:, None, :].astype(p.dtype), p[:, :-1]], axis=1)
    ps = p + (p_prev - p) * mu.astype(p.dtype)
    splits = [A_WIDTH, 2 * A_WIDTH, 3 * A_WIDTH, 3 * A_WIDTH + A_LORA_W, 3 * A_WIDTH + A_LORA_W + A_LORA_A]
    r, k, v, wd, ad, gd = jnp.split(ps, splits, axis=-1)
    w_log = -jax.nn.softplus(-(w0 + jnp.tanh(wd) @ w2).astype(jnp.float32)) - 0.5
    decay = jnp.exp(-jnp.exp(w_log))
    a = jax.nn.sigmoid((a0 + ad @ a2).astype(jnp.float32))
    g = (jax.nn.sigmoid(gd) @ g2).astype(jnp.float32)

    def heads(t):
        return t.astype(jnp.float32).reshape(B, T, A_HEADS, HEAD_DIM)

    def hvec(t):
        return t.astype(jnp.float32).reshape(A_HEADS, HEAD_DIM)

    r, k, v, decay, a = heads(r), heads(k), heads(v), heads(decay), heads(a)
    kk = k * hvec(k_k)
    kk = kk / jnp.maximum(jnp.sqrt(jnp.sum(kk * kk, axis=-1, keepdims=True)), 1e-12)
    k = k * (1.0 + (a - 1.0) * hvec(k_a))

    def step(S, inp):
        r_t, w_t, k_t, v_t, kk_t, a_t = inp
        s_kk = jnp.einsum('bhij,bhj->bhi', S, -kk_t)
        S = (S * w_t[:, :, None, :]
             + s_kk[..., :, None] * (kk_t * a_t)[:, :, None, :]
             + v_t[..., :, None] * k_t[:, :, None, :])
        return S, jnp.einsum('bhij,bhj->bhi', S, r_t)

    xs = tuple(jnp.moveaxis(t, 1, 0) for t in (r, decay, k, v, kk, a))
    S_fin, ys = lax.scan(step, S0.astype(jnp.float32), xs)
    y = jnp.moveaxis(ys, 0, 1)
    mean = jnp.mean(y, axis=-1, keepdims=True)
    var = jnp.mean(jnp.square(y - mean), axis=-1, keepdims=True)
    y = (y - mean) * lax.rsqrt(var + RWKV_GN_EPS) * hvec(ln_w) + hvec(ln_b)
    y = y + jnp.sum(r * k * hvec(r_k), axis=-1, keepdims=True) * v
    out = y.reshape(B, T, A_WIDTH) * g
    return out.astype(p.dtype), S_fin, p[:, -1]


def fox_attention(q, k, v, logf):
    Tq = q.shape[1]
    L = k.shape[1]
    off = L - Tq
    scale = HEAD_DIM ** -0.5
    c = jnp.swapaxes(jnp.cumsum(logf, axis=1), 1, 2)
    outs = []
    for qs in range(0, Tq, FOX_Q_BLOCK):
        qe = min(qs + FOX_Q_BLOCK, Tq)
        ke = off + qe
        logits = jnp.einsum('bqhd,bkhd->bhqk', q[:, qs:qe], k[:, :ke]).astype(jnp.float32) * scale
        logits = logits + c[:, :, off + qs:off + qe, None] - c[:, :, None, :ke]
        q_pos = jnp.arange(off + qs, off + qe)
        k_pos = jnp.arange(ke)
        logits = jnp.where(q_pos[:, None] >= k_pos[None, :], logits, -jnp.inf)
        probs = jax.nn.softmax(logits, axis=-1)
        outs.append(jnp.einsum('bhqk,bkhd->bqhd', probs.astype(v.dtype), v[:, :ke]))
    return jnp.concatenate(outs, axis=1)


def moba_attention(q, k, v, q_pos):
    B, Tq, H, Dh = q.shape
    L, Hkv = k.shape[1], k.shape[2]
    nb = -(-L // MOBA_BLOCK)
    pad = nb * MOBA_BLOCK - L

    def blocks(t):
        t = jnp.pad(t, ((0, 0), (0, pad), (0, 0), (0, 0)))
        return jnp.transpose(t.reshape(B, nb, MOBA_BLOCK, Hkv, Dh), (0, 3, 1, 2, 4))

    k_blk, v_blk = blocks(k), blocks(v)
    k_mean = jnp.mean(k_blk.astype(jnp.float32), axis=3)
    kv_head = jnp.arange(H) // (H // Hkv)
    slopes = jnp.asarray(2.0 ** (-8.0 * np.arange(1, H + 1) / H), dtype=jnp.float32)
    k_sel = min(MOBA_TOPK, nb)
    scale = Dh ** -0.5
    n_rows = B * Tq
    n_chunks = -(-n_rows // MOBA_ROW_CHUNK)
    rpad = n_chunks * MOBA_ROW_CHUNK - n_rows
    rq = jnp.pad(q.reshape(n_rows, H, Dh), ((0, rpad), (0, 0), (0, 0))).reshape(n_chunks, MOBA_ROW_CHUNK, H, Dh)
    rb = jnp.pad(jnp.repeat(jnp.arange(B, dtype=jnp.int32), Tq), (0, rpad)).reshape(n_chunks, MOBA_ROW_CHUNK)
    rt = jnp.pad(jnp.tile(q_pos.astype(jnp.int32), B), (0, rpad)).reshape(n_chunks, MOBA_ROW_CHUNK)

    def chunk(args):
        qc, bc, tc = args
        own = tc // MOBA_BLOCK
        gate = jnp.einsum('chd,chnd->chn', qc.astype(jnp.float32), k_mean[bc[:, None], kv_head[None, :]])
        gate = jnp.where(jnp.arange(nb)[None, None, :] < own[:, None, None], gate, -jnp.inf)
        _, sel = lax.top_k(gate, k_sel)
        valid = sel < own[:, None, None]
        own_b = jnp.broadcast_to(own[:, None, None], (qc.shape[0], H, 1))
        idx = jnp.concatenate([sel, own_b], axis=-1)
        valid = jnp.concatenate([valid, jnp.ones_like(own_b, dtype=bool)], axis=-1)
        kb = k_blk[bc[:, None, None], kv_head[None, :, None], idx]
        vb = v_blk[bc[:, None, None], kv_head[None, :, None], idx]
        s_pos = idx[..., None] * MOBA_BLOCK + jnp.arange(MOBA_BLOCK)
        dist = tc[:, None, None, None] - s_pos
        logits = jnp.einsum('chd,chjsd->chjs', qc, kb).astype(jnp.float32) * scale - slopes[None, :, None, None] * dist
        logits = jnp.where(valid[..., None] & (dist >= 0), logits, -jnp.inf)
        C, _, J, _ = logits.shape
        probs = jax.nn.softmax(logits.reshape(C, H, J * MOBA_BLOCK), axis=-1).reshape(C, H, J, MOBA_BLOCK)
        return jnp.einsum('chjs,chjsd->chd', probs.astype(vb.dtype), vb)

    out = lax.map(chunk, (rq, rb, rt))
    return out.reshape(n_chunks * MOBA_ROW_CHUNK, H, Dh)[:n_rows].reshape(B, Tq, H, Dh)


def moe_swiglu(x, router_w, router_b, w_gate, w_up, w_down):
    logits = (x @ router_w).astype(jnp.float32) + router_b.astype(jnp.float32)
    top_val, top_idx = lax.top_k(logits, TOP_K_EXPERTS)
    probs = jax.nn.softmax(top_val, axis=-1)
    gate = jnp.sum(jax.nn.one_hot(top_idx, N_EXPERTS, dtype=jnp.float32) * probs[..., None], axis=-2)
    out = jnp.zeros_like(x)
    for e in range(N_EXPERTS):
        out = out + gate[..., e:e + 1].astype(x.dtype) * swiglu(x, w_gate[e], w_up[e], w_down[e])
    return out


def setup_inputs(seed: int = 0) -> dict:
    key = jax.random.key(seed)
    keys = iter(jax.random.split(key, 64))

    def nrm(shape, scale=1.0):
        return scale * jax.random.normal(next(keys), shape, jnp.float32)

    def gain(n):
        return 1.0 + 0.05 * nrm((n,))

    n_pages = PAST_LEN // PAGE_SIZE
    n_used = DEC_BATCH * n_pages
    n_pool = (n_used * POOL_NUM + POOL_DEN - 1) // POOL_DEN
    d_inv = D_MODEL ** -0.5
    inp = {}
    inp['x_prompt'] = nrm((BATCH, SEQ, D_MODEL))
    inp['x_sample'] = nrm((DEC_BATCH, DEC_SEQ, D_MODEL))
    inp['state_rwkv_S'] = nrm((DEC_BATCH, A_HEADS, HEAD_DIM, HEAD_DIM), 0.5)
    inp['state_rwkv_shift'] = nrm((DEC_BATCH, A_PROJ))
    inp['cache_fox_k'] = nrm((n_pool, PAGE_SIZE, B_HEADS, HEAD_DIM))
    inp['cache_fox_v'] = nrm((n_pool, PAGE_SIZE, B_HEADS, HEAD_DIM))
    inp['cache_fox_logf'] = jax.nn.log_sigmoid(2.0 + nrm((n_pool, PAGE_SIZE, B_HEADS)))
    inp['cache_moba_k'] = nrm((n_pool, PAGE_SIZE, C_KV_HEADS, HEAD_DIM))
    inp['cache_moba_v'] = nrm((n_pool, PAGE_SIZE, C_KV_HEADS, HEAD_DIM))
    inp['page_table'] = jax.random.permutation(next(keys), n_pool)[:n_used].reshape(DEC_BATCH, n_pages).astype(jnp.int32)
    inp['norm0_mix_g'] = gain(D_MODEL)
    inp['w_in0'] = nrm((D_MODEL, L0_PROJ), d_inv)
    inp['fox_b_f'] = 2.0 + 0.5 * nrm((B_HEADS,))
    inp['rwkv_mu'] = jax.random.uniform(next(keys), (A_PROJ,), jnp.float32)
    inp['rwkv_w0'] = jax.random.uniform(next(keys), (A_WIDTH,), jnp.float32, -6.0, -1.0)
    inp['rwkv_w2'] = nrm((A_LORA_W, A_WIDTH), 0.1)
    inp['rwkv_a0'] = nrm((A_WIDTH,), 0.1)
    inp['rwkv_a2'] = nrm((A_LORA_A, A_WIDTH), 0.1)
    inp['rwkv_g2'] = nrm((A_LORA_G, A_WIDTH), A_LORA_G ** -0.5)
    inp['rwkv_k_k'] = 0.85 + 0.05 * nrm((A_WIDTH,))
    inp['rwkv_k_a'] = 1.0 + 0.05 * nrm((A_WIDTH,))
    inp['rwkv_r_k'] = nrm((A_HEADS, HEAD_DIM), 0.1)
    inp['rwkv_ln_w'] = gain(A_WIDTH)
    inp['rwkv_ln_b'] = nrm((A_WIDTH,), 0.02)
    inp['w_out0'] = nrm((A_WIDTH + B_WIDTH, D_MODEL), (A_WIDTH + B_WIDTH) ** -0.5)
    inp['norm0_ffn_g'] = gain(D_MODEL)
    inp['ffn_w_gate'] = nrm((D_MODEL, D_FF), d_inv)
    inp['ffn_w_up'] = nrm((D_MODEL, D_FF), d_inv)
    inp['ffn_w_down'] = nrm((D_FF, D_MODEL), D_FF ** -0.5)
    inp['norm1_mix_g'] = gain(D_MODEL)
    inp['w_in1'] = nrm((D_MODEL, L1_PROJ), d_inv)
    inp['w_out1'] = nrm((C_WIDTH, D_MODEL), C_WIDTH ** -0.5)
    inp['norm1_ffn_g'] = gain(D_MODEL)
    inp['router_w'] = nrm((D_MODEL, N_EXPERTS), d_inv)
    inp['router_b'] = nrm((N_EXPERTS,), 0.01)
    inp['moe_w_gate'] = nrm((N_EXPERTS, D_MODEL, D_FF_EXPERT), d_inv)
    inp['moe_w_up'] = nrm((N_EXPERTS, D_MODEL, D_FF_EXPERT), d_inv)
    inp['moe_w_down'] = nrm((N_EXPERTS, D_FF_EXPERT, D_MODEL), D_FF_EXPERT ** -0.5)
    inp['norm_final_g'] = gain(D_MODEL)
    return inp


def reference(x_prompt, x_sample, state_rwkv_S, state_rwkv_shift, cache_fox_k, cache_fox_v, cache_fox_logf,
              cache_moba_k, cache_moba_v, page_table, norm0_mix_g, w_in0, fox_b_f, rwkv_mu, rwkv_w0, rwkv_w2,
              rwkv_a0, rwkv_a2, rwkv_g2, rwkv_k_k, rwkv_k_a, rwkv_r_k, rwkv_ln_w, rwkv_ln_b, w_out0, norm0_ffn_g,
              ffn_w_gate, ffn_w_up, ffn_w_down, norm1_mix_g, w_in1, w_out1, norm1_ffn_g, router_w, router_b,
              moe_w_gate, moe_w_up, moe_w_down, norm_final_g):

    def layer0_mixer(h, S0, shift0, fk_past, fv_past, flf_past):
        B, T, _ = h.shape
        proj = h @ w_in0
        ya, S_new, shift_new = rwkv7_time_mix(proj[..., :A_PROJ], S0, shift0, rwkv_mu, rwkv_w0, rwkv_w2,
                                              rwkv_a0, rwkv_a2, rwkv_g2, rwkv_k_k, rwkv_k_a, rwkv_r_k,
                                              rwkv_ln_w, rwkv_ln_b)
        q, k, v, f = jnp.split(proj[..., A_PROJ:], [B_WIDTH, 2 * B_WIDTH, 3 * B_WIDTH], axis=-1)
        q = q.reshape(B, T, B_HEADS, HEAD_DIM)
        k = k.reshape(B, T, B_HEADS, HEAD_DIM)
        v = v.reshape(B, T, B_HEADS, HEAD_DIM)
        logf = jax.nn.log_sigmoid(f.astype(jnp.float32) + fox_b_f.astype(jnp.float32))
        if fk_past is None:
            k_all, v_all, lf_all = k, v, logf
        else:
            k_all = jnp.concatenate([fk_past.astype(k.dtype), k], axis=1)
            v_all = jnp.concatenate([fv_past.astype(v.dtype), v], axis=1)
            lf_all = jnp.concatenate([flf_past.astype(jnp.float32), logf], axis=1)
        yb = fox_attention(q, k_all, v_all, lf_all).reshape(B, T, B_WIDTH).astype(ya.dtype)
        y = jnp.concatenate([ya, yb], axis=-1) @ w_out0
        return y, S_new, shift_new, k, v, logf

    def layer1_mixer(h, mk_past, mv_past):
        B, T, _ = h.shape
        q, k, v = jnp.split(h @ w_in1, [C_WIDTH, C_WIDTH + C_KV_WIDTH], axis=-1)
        q = q.reshape(B, T, C_HEADS, HEAD_DIM)
        k = k.reshape(B, T, C_KV_HEADS, HEAD_DIM)
        v = v.reshape(B, T, C_KV_HEADS, HEAD_DIM)
        if mk_past is None:
            k_all, v_all = k, v
        else:
            k_all = jnp.concatenate([mk_past.astype(k.dtype), k], axis=1)
            v_all = jnp.concatenate([mv_past.astype(v.dtype), v], axis=1)
        L = k_all.shape[1]
        y = moba_attention(q, k_all, v_all, jnp.arange(L - T, L))
        return y.reshape(B, T, C_WIDTH) @ w_out1, k, v

    def run(x, S0, shift0, fk_past, fv_past, flf_past, mk_past, mv_past):
        for layer in range(DEPTH):
            if layer % 2 == 0:
                y, S_new, shift_new, fk, fv, flf = layer0_mixer(rmsnorm(x, norm0_mix_g), S0, shift0,
                                                               fk_past, fv_past, flf_past)
                x = x + y
                x = x + swiglu(rmsnorm(x, norm0_ffn_g), ffn_w_gate, ffn_w_up, ffn_w_down)
            else:
                y, mk, mv = layer1_mixer(rmsnorm(x, norm1_mix_g), mk_past, mv_past)
                x = x + y
                x = x + moe_swiglu(rmsnorm(x, norm1_ffn_g), router_w, router_b, moe_w_gate, moe_w_up, moe_w_down)
        return rmsnorm(x, norm_final_g), S_new, shift_new, fk, fv, flf, mk, mv

    n_prompt = x_prompt.shape[0]
    y_prompt, p_S, p_shift, p_fk, p_fv, p_flf, p_mk, p_mv = run(
        x_prompt,
        jnp.zeros((n_prompt, A_HEADS, HEAD_DIM, HEAD_DIM), jnp.float32),
        jnp.zeros((n_prompt, A_PROJ), x_prompt.dtype),
        None, None, None, None, None)
    y_sample, s_S, s_shift, s_fk, s_fv, s_flf, s_mk, s_mv = run(
        x_sample, state_rwkv_S, state_rwkv_shift,
        gather_pages(cache_fox_k, page_table), gather_pages(cache_fox_v, page_table),
        gather_pages(cache_fox_logf, page_table),
        gather_pages(cache_moba_k, page_table), gather_pages(cache_moba_v, page_table))
    return (y_prompt, y_sample, p_S, p_shift, p_fk, p_fv, p_flf, p_mk, p_mv,
            s_S, s_shift, s_fk, s_fv, s_flf, s_mk, s_mv)
```

```python
import functools

import jax
import jax.numpy as jnp
from jax import lax
from jax.experimental import pallas as pl
from jax.experimental.pallas import tpu as pltpu

F32 = jnp.float32
BF16 = jnp.bfloat16

HEAD_DIM = 64
A_HEADS = 16
A_WIDTH = A_HEADS * HEAD_DIM
A_LORA_W = 64
A_LORA_A = 64
A_LORA_G = 128
A_PROJ = 3 * A_WIDTH + A_LORA_W + A_LORA_A + A_LORA_G
RWKV_GN_EPS = 64e-5
B_HEADS = 16
B_WIDTH = B_HEADS * HEAD_DIM
C_HEADS = 32
C_KV_HEADS = 8
C_REP = C_HEADS // C_KV_HEADS
C_WIDTH = C_HEADS * HEAD_DIM
C_KV_WIDTH = C_KV_HEADS * HEAD_DIM
MOBA_BLOCK = 256
MOBA_TOPK = 3
N_EXPERTS = 8
NORM_EPS = 1e-6

LANES = 128
GROUP_LANES = 256
VMEM_LIMIT = 56 * 1024 * 1024
NEG_INF = float("-inf")


def _cparams(*sem):
    return pltpu.CompilerParams(dimension_semantics=sem, vmem_limit_bytes=VMEM_LIMIT)


def _bf(x):
    return x.astype(BF16)


def _dot(a, b):
    return jnp.dot(_bf(a), _bf(b), preferred_element_type=F32)


def _dot_nt(a, b):
    return lax.dot_general(_bf(a), _bf(b), (((1,), (1,)), ((), ())), preferred_element_type=F32)


def _dot_tn(a, b):
    return lax.dot_general(_bf(a), _bf(b), (((0,), (0,)), ((), ())), preferred_element_type=F32)


def _split2(x):
    hi = _bf(x)
    lo = _bf(x - hi.astype(F32))
    return hi, lo


def _split3(x):
    hi = _bf(x)
    r1 = x - hi.astype(F32)
    mid = _bf(r1)
    lo = _bf(r1 - mid.astype(F32))
    return hi, mid, lo


def _dot3(a, b_hi, b_lo):
    a_hi, a_lo = _split2(a)
    return (jnp.dot(a_hi, b_hi, preferred_element_type=F32)
            + (jnp.dot(a_hi, b_lo, preferred_element_type=F32)
               + jnp.dot(a_lo, b_hi, preferred_element_type=F32)))


def _dot3_nt(a, b_hi, b_lo):
    a_hi, a_lo = _split2(a)
    dn = (((1,), (1,)), ((), ()))
    return (lax.dot_general(a_hi, b_hi, dn, preferred_element_type=F32)
            + (lax.dot_general(a_hi, b_lo, dn, preferred_element_type=F32)
               + lax.dot_general(a_lo, b_hi, dn, preferred_element_type=F32)))


def _dot_exact_rhs(a, b_exact):
    a_hi, a_lo = _split2(a)
    return (jnp.dot(a_hi, b_exact, preferred_element_type=F32)
            + jnp.dot(a_lo, b_exact, preferred_element_type=F32))


def _sigmoid(x):
    return 1.0 / (1.0 + jnp.exp(-x))


def _pick(n, prefs):
    for p in prefs:
        if n % p == 0:
            return p
    return n


def _norm_mm_kernel(x_ref, g_ref, w_ref, o_ref, xn_ref):
    @pl.when(pl.program_id(1) == 0)
    def _():
        x = x_ref[...]
        ms = jnp.mean(x * x, axis=-1, keepdims=True)
        xn_ref[...] = _bf(x * lax.rsqrt(ms + NORM_EPS) * g_ref[...])

    o_ref[0] = jnp.dot(xn_ref[...], w_ref[...], preferred_element_type=F32)


def norm_matmul(x, g, w, cw, tn):
    M, K = x.shape
    N = w.shape[1]
    nc = N // cw
    tm = _pick(M, (1024, 512, 256, 128, 8))
    per = cw // tn
    return pl.pallas_call(
        _norm_mm_kernel,
        grid=(M // tm, N // tn),
        in_specs=[pl.BlockSpec((tm, K), lambda i, j: (i, 0)),
                  pl.BlockSpec((1, K), lambda i, j: (0, 0)),
                  pl.BlockSpec((K, tn), lambda i, j: (0, j))],
        out_specs=pl.BlockSpec((1, tm, tn), lambda i, j: (j // per, i, j % per)),
        out_shape=jax.ShapeDtypeStruct((nc, M, cw), F32),
        scratch_shapes=[pltpu.VMEM((tm, K), BF16)],
        compiler_params=_cparams("parallel", "arbitrary"),
    )(x, g.reshape(1, K), w)


def _mm_res_kernel(a_ref, w_ref, r_ref, *rest, nk, final_norm):
    if final_norm:
        g_ref, o_ref, acc_ref = rest
    else:
        o_ref, acc_ref = rest
    k = pl.program_id(2)

    @pl.when(k == 0)
    def _():
        acc_ref[...] = r_ref[...]

    acc_ref[...] += jnp.dot(a_ref[0], w_ref[...], preferred_element_type=F32)

    @pl.when(k == nk - 1)
    def _():
        y = acc_ref[...]
        if final_norm:
            ms = jnp.mean(y * y, axis=-1, keepdims=True)
            y = y * lax.rsqrt(ms + NORM_EPS) * g_ref[...]
        o_ref[...] = y


def matmul_res(a, w, res, tk, final_g=None):
    ka, M, kw = a.shape
    N = w.shape[1]
    tm = _pick(M, (512, 256, 128, 8))
    tn = N if final_g is not None else _pick(N, (1024, 512, 256, 128))
    per = kw // tk
    nk = ka * per
    in_specs = [pl.BlockSpec((1, tm, tk), lambda i, j, k: (k // per, i, k % per)),
                pl.BlockSpec((tk, tn), lambda i, j, k: (k, j)),
                pl.BlockSpec((tm, tn), lambda i, j, k: (i, j))]
    args = [a, w, res]
    if final_g is not None:
        in_specs.append(pl.BlockSpec((1, tn), lambda i, j, k: (0, 0)))
        args.append(final_g.reshape(1, N))
    return pl.pallas_call(
        functools.partial(_mm_res_kernel, nk=nk, final_norm=final_g is not None),
        grid=(M // tm, N // tn, nk),
        in_specs=in_specs,
        out_specs=pl.BlockSpec((tm, tn), lambda i, j, k: (i, j)),
        out_shape=jax.ShapeDtypeStruct((M, N), F32),
        scratch_shapes=[pltpu.VMEM((tm, tn), F32)],
        compiler_params=_cparams("parallel", "parallel", "arbitrary"),
    )(*args)


def _swiglu_up_kernel(x_ref, g_ref, wg_ref, wu_ref, *rest, routed):
    if routed:
        rwh_ref, rwl_ref, rb_ref, h_ref, xn_ref, gate_ref = rest
    else:
        h_ref, xn_ref = rest
    e = pl.program_id(1)

    @pl.when((e == 0) & (pl.program_id(2) == 0))
    def _():
        x = x_ref[...]
        ms = jnp.mean(x * x, axis=-1, keepdims=True)
        xn = x * lax.rsqrt(ms + NORM_EPS) * g_ref[...]
        xn_ref[...] = _bf(xn)
        if routed:
            logits = _dot3(xn, rwh_ref[...], rwl_ref[...]) + rb_ref[...]
            lane = lax.broadcasted_iota(jnp.int32, logits.shape, 1).astype(F32)
            live = lane < N_EXPERTS
            z = jnp.where(live, logits, NEG_INF)
            m1 = jnp.max(z, axis=1, keepdims=True)
            i1 = jnp.min(jnp.where(z == m1, lane, float(LANES)), axis=1, keepdims=True)
            z2 = jnp.where(lane == i1, NEG_INF, z)
            m2 = jnp.max(z2, axis=1, keepdims=True)
            i2 = jnp.min(jnp.where(z2 == m2, lane, float(LANES)), axis=1, keepdims=True)
            e2 = jnp.exp(m2 - m1)
            den = 1.0 + e2
            gate_ref[...] = jnp.where(lane == i1, 1.0 / den, jnp.where(lane == i2, e2 / den, 0.0))

    xn = xn_ref[...]
    a = jnp.dot(xn, wg_ref[0], preferred_element_type=F32)
    b = jnp.dot(xn, wu_ref[0], preferred_element_type=F32)
    h = a * _sigmoid(a) * b
    if routed:
        gates = gate_ref[...]
        lane = lax.broadcasted_iota(jnp.int32, gates.shape, 1)
        h = h * jnp.sum(jnp.where(lane == e, gates, 0.0), axis=1, keepdims=True)
    h_ref[...] = _bf(h)


def swiglu_up(x, g, wg, wu, router=None):
    M, K = x.shape
    E, _, F = wg.shape
    tm = _pick(M, (1024, 512, 256, 128, 8))
    tn = _pick(F, (512, 256, 128))
    per = F // tn
    routed = router is not None
    in_specs = [pl.BlockSpec((tm, K), lambda i, e, j: (i, 0)),
                pl.BlockSpec((1, K), lambda i, e, j: (0, 0)),
                pl.BlockSpec((1, K, tn), lambda i, e, j: (e, 0, j)),
                pl.BlockSpec((1, K, tn), lambda i, e, j: (e, 0, j))]
    args = [x, g.reshape(1, K), wg, wu]
    scratch = [pltpu.VMEM((tm, K), BF16)]
    if routed:
        in_specs += [pl.BlockSpec((K, LANES), lambda i, e, j: (0, 0)),
                     pl.BlockSpec((K, LANES), lambda i, e, j: (0, 0)),
                     pl.BlockSpec((1, LANES), lambda i, e, j: (0, 0))]
        args += list(router)
        scratch.append(pltpu.VMEM((tm, LANES), F32))
    return pl.pallas_call(
        functools.partial(_swiglu_up_kernel, routed=routed),
        grid=(M // tm, E, per),
        in_specs=in_specs,
        out_specs=pl.BlockSpec((tm, tn), lambda i, e, j: (i, e * per + j)),
        out_shape=jax.ShapeDtypeStruct((M, E * F), BF16),
        scratch_shapes=scratch,
        compiler_params=_cparams("parallel", "arbitrary", "arbitrary"),
    )(*args)


def _rwkv_prep_kernel(p_ref, pp_ref, mu_ref, w0_ref, w2h_ref, w2l_ref, a0_ref, a2h_ref, a2l_ref,
                      g2h_ref, g2l_ref, kk_ref, ka_ref, bd_ref,
                      r_ref, lw_ref, k_ref, v_ref, kn_ref, b_ref, g_ref):
    p = p_ref[0]
    ps = p + (pp_ref[...] - p) * mu_ref[...]
    W = A_WIDTH
    r = ps[:, 0:W]
    k = ps[:, W:2 * W]
    v = ps[:, 2 * W:3 * W]
    o = 3 * W
    wd = ps[:, o:o + A_LORA_W]
    ad = ps[:, o + A_LORA_W:o + A_LORA_W + A_LORA_A]
    gd = ps[:, o + A_LORA_W + A_LORA_A:]
    z = -(w0_ref[...] + _dot3(jnp.tanh(wd), w2h_ref[...], w2l_ref[...]))
    softplus = jnp.maximum(z, 0.0) + jnp.log(1.0 + jnp.exp(-jnp.abs(z)))
    lw_ref[...] = -jnp.exp(-softplus - 0.5)
    a = _sigmoid(a0_ref[...] + _dot3(ad, a2h_ref[...], a2l_ref[...]))
    g_ref[...] = _dot3(_sigmoid(gd), g2h_ref[...], g2l_ref[...])
    kk = k * kk_ref[...]
    ss = _dot_exact_rhs(kk * kk, bd_ref[...])
    kn = kk / jnp.maximum(jnp.sqrt(ss), 1e-12)
    r_ref[...] = r
    v_ref[...] = v
    kn_ref[...] = kn
    b_ref[...] = kn * a
    k_ref[...] = k * (1.0 + (a - 1.0) * ka_ref[...])


def rwkv_prep(pab, p_prev, prm):
    M = p_prev.shape[0]
    tm = _pick(M, (256, 128, 8))
    W = A_WIDTH
    row = lambda n: pl.BlockSpec((1, n), lambda i: (0, 0))
    mat = lambda a, b: pl.BlockSpec((a, b), lambda i: (0, 0))
    out = pl.BlockSpec((tm, W), lambda i: (i, 0))
    return pl.pallas_call(
        _rwkv_prep_kernel,
        grid=(M // tm,),
        in_specs=[pl.BlockSpec((1, tm, A_PROJ), lambda i: (0, i, 0)),
                  pl.BlockSpec((tm, A_PROJ), lambda i: (i, 0)),
                  row(A_PROJ), row(W), mat(A_LORA_W, W), mat(A_LORA_W, W),
                  row(W), mat(A_LORA_A, W), mat(A_LORA_A, W),
                  mat(A_LORA_G, W), mat(A_LORA_G, W), row(W), row(W), mat(W, W)],
        out_specs=[out] * 7,
        out_shape=[jax.ShapeDtypeStruct((M, W), F32)] * 7,
        compiler_params=_cparams("parallel"),
    )(pab, p_prev, prm["mu"], prm["w0"], *prm["w2"], prm["a0"], *prm["a2"], *prm["g2"],
      prm["k_k"], prm["k_a"], prm["bd"])


def _rwkv_chunk_kernel(r_ref, lw_ref, k_ref, v_ref, kn_ref, b_ref, s0_ref, y_ref, s_ref, *, C, nc, Bb):
    C2 = 2 * C
    lane = lax.broadcasted_iota(jnp.int32, (C2, LANES), 1)
    rowi = lax.broadcasted_iota(jnp.int32, (C2, LANES), 0)
    mask2 = ((rowi >= C) == (lane >= HEAD_DIM)).astype(F32)
    ri = lax.broadcasted_iota(jnp.int32, (C2, C2), 0)
    ci = lax.broadcasted_iota(jnp.int32, (C2, C2), 1)
    same = (ri >= C) == (ci >= C)
    strict = same & (ri > ci)
    incl = same & (ri >= ci)
    eye = (ri == ci).astype(F32)
    ti = lax.broadcasted_iota(jnp.int32, (C, C), 0)
    tj = lax.broadcasted_iota(jnp.int32, (C, C), 1)
    tri = (ti >= tj).astype(BF16)
    n_levels = max(C.bit_length() - 2, 0)

    def stack(x):
        return jnp.concatenate([x, x], axis=0) * mask2

    for bi in range(Bb):
        def chunk(c, S, bi=bi):
            sl = pl.ds(pl.multiple_of(c * C, C), C)
            r = r_ref[bi, sl, :]
            lw = lw_ref[bi, sl, :]
            k = k_ref[bi, sl, :]
            v = v_ref[bi, sl, :]
            kn = kn_ref[bi, sl, :]
            b = b_ref[bi, sl, :]
            hi, mid, lo = _split3(lw)
            cs = (jnp.dot(tri, hi, preferred_element_type=F32)
                  + (jnp.dot(tri, mid, preferred_element_type=F32)
                     + jnp.dot(tri, lo, preferred_element_type=F32)))
            gt = cs[C - 1:C, :]
            e_neg = jnp.exp(-cs)
            e_rem = jnp.exp(gt - cs)
            KT = stack(kn * jnp.exp(cs - lw))
            BI = stack(b * e_neg)
            KI = stack(k * e_neg)
            RT = stack(r * jnp.exp(cs))
            V2 = stack(v)
            KG = stack(k * e_rem)
            BG = stack(b * e_rem)
            a_kb = _dot_nt(KT, BI)
            a_kv = _dot_nt(KT, KI)
            a_rb = jnp.where(incl, _dot_nt(RT, BI), 0.0)
            a_rk = jnp.where(incl, _dot_nt(RT, KI), 0.0)
            nmat = -jnp.where(strict, a_kb, 0.0)
            tinv = eye + nmat
            pw = nmat
            for _ in range(n_levels):
                pw = _dot(pw, pw)
                tinv = tinv + _dot(tinv, pw)
            kp = _dot(tinv, KT)
            w1 = _dot(tinv, _dot(jnp.where(strict, a_kv, 0.0), V2))
            rp = RT - _dot(a_rb, kp)
            y1 = _dot(a_rk, V2) - _dot(a_rb, w1)
            mlow = _dot_tn(kp, BG)
            nt = _dot_tn(V2, KG) - _dot_tn(w1, BG)
            s_hi, s_lo = _split2(S)
            y2 = _dot3_nt(rp, s_hi, s_lo) + y1
            y_ref[bi, sl, :] = y2[:C] + y2[C:]
            m_hi, m_lo = _split2(mlow)
            return S * jnp.exp(gt) - _dot3(S, m_hi, m_lo) + nt

        s_ref[bi, 0] = lax.fori_loop(0, nc, chunk, s0_ref[bi, 0])


def rwkv_chunk(r, lw, k, v, kn, b, s_bd, B, T):
    C = min(T, 64)
    nc = T // C
    Bb = 1 if nc > 1 else _pick(B, (8, 4, 2, 1))
    HP = A_WIDTH // LANES
    seq = pl.BlockSpec((Bb, T, LANES), lambda i, h: (i, 0, h))
    st = pl.BlockSpec((Bb, 1, LANES, LANES), lambda i, h: (i, h, 0, 0))
    r3 = lambda x: x.reshape(B, T, A_WIDTH)
    y, s = pl.pallas_call(
        functools.partial(_rwkv_chunk_kernel, C=C, nc=nc, Bb=Bb),
        grid=(B // Bb, HP),
        in_specs=[seq] * 6 + [st],
        out_specs=[seq, st],
        out_shape=[jax.ShapeDtypeStruct((B, T, A_WIDTH), F32),
                   jax.ShapeDtypeStruct((B, HP, LANES, LANES), F32)],
        compiler_params=_cparams("parallel", "parallel"),
    )(r3(r), r3(lw), r3(k), r3(v), r3(kn), r3(b), s_bd)
    return y.reshape(B * T, A_WIDTH), s


def _rwkv_post_kernel(y_ref, r_ref, k_ref, v_ref, g_ref, rk_ref, lnw_ref, lnb_ref, bd_ref, o_ref):
    y = y_ref[...]
    bd = bd_ref[...]
    inv_n = 1.0 / HEAD_DIM
    mean = _dot_exact_rhs(y, bd) * inv_n
    d = y - mean
    var = _dot_exact_rhs(d * d, bd) * inv_n
    yn = d * lax.rsqrt(var + RWKV_GN_EPS) * lnw_ref[...] + lnb_ref[...]
    bonus = _dot_exact_rhs(r_ref[...] * k_ref[...] * rk_ref[...], bd)
    o_ref[...] = _bf((yn + bonus * v_ref[...]) * g_ref[...])


def rwkv_post(y, r, k, v, g, prm):
    M, W = y.shape
    tm = _pick(M, (256, 128, 8))
    blk = pl.BlockSpec((tm, W), lambda i: (i, 0))
    row = pl.BlockSpec((1, W), lambda i: (0, 0))
    return pl.pallas_call(
        _rwkv_post_kernel,
        grid=(M // tm,),
        in_specs=[blk] * 5 + [row] * 3 + [pl.BlockSpec((W, W), lambda i: (0, 0))],
        out_specs=blk,
        out_shape=jax.ShapeDtypeStruct((M, W), BF16),
        compiler_params=_cparams("parallel"),
    )(y, r, k, v, g, prm["r_k"], prm["ln_w"], prm["ln_b"], prm["bd"])


def _logf_kernel(f_ref, b_ref, o_ref):
    z = f_ref[0] + b_ref[...]
    o_ref[...] = jnp.minimum(z, 0.0) - jnp.log(1.0 + jnp.exp(-jnp.abs(z)))


def fox_logf(pab, bias_row, col_block):
    M = pab.shape[1]
    tm = _pick(M, (1024, 512, 256, 128, 8))
    return pl.pallas_call(
        _logf_kernel,
        grid=(M // tm,),
        in_specs=[pl.BlockSpec((1, tm, LANES), lambda i: (1, i, col_block)),
                  pl.BlockSpec((1, LANES), lambda i: (0, 0))],
        out_specs=pl.BlockSpec((tm, LANES), lambda i: (i, 0)),
        out_shape=jax.ShapeDtypeStruct((M, LANES), F32),
        compiler_params=_cparams("parallel"),
    )(pab, bias_row)


def _cumsum_body(x, o_ref, carry_ref, first):
    @pl.when(first)
    def _():
        carry_ref[...] = jnp.zeros_like(carry_ref)

    ti = lax.broadcasted_iota(jnp.int32, (LANES, LANES), 0)
    tj = lax.broadcasted_iota(jnp.int32, (LANES, LANES), 1)
    triu = (ti <= tj).astype(BF16)
    hi, mid, lo = _split3(x)
    cs = (jnp.dot(hi, triu, preferred_element_type=F32)
          + (jnp.dot(mid, triu, preferred_element_type=F32)
             + jnp.dot(lo, triu, preferred_element_type=F32))) + carry_ref[...]
    o_ref[0] = cs
    carry_ref[...] = jnp.broadcast_to(cs[:, LANES - 1:LANES], cs.shape)


def _cumsum_kernel(x_ref, o_ref, carry_ref):
    _cumsum_body(x_ref[0], o_ref, carry_ref, pl.program_id(1) == 0)


def _cumsum_paged_kernel(pt_ref, pool_ref, new_ref, o_ref, carry_ref, *, n_pages):
    p = pl.program_id(1)
    x = jnp.where(p < n_pages, pool_ref[0], new_ref[0])
    _cumsum_body(x, o_ref, carry_ref, p == 0)


def cumsum_lanes(xT):
    B, H, L = xT.shape
    return pl.pallas_call(
        _cumsum_kernel,
        grid=(B, L // LANES),
        in_specs=[pl.BlockSpec((1, H, LANES), lambda b, p: (b, 0, p))],
        out_specs=pl.BlockSpec((1, H, LANES), lambda b, p: (b, 0, p)),
        out_shape=jax.ShapeDtypeStruct((B, H, L), F32),
        scratch_shapes=[pltpu.VMEM((H, LANES), F32)],
        compiler_params=_cparams("parallel", "arbitrary"),
    )(xT)


def cumsum_lanes_paged(page_table, poolT, newT):
    B, n_pages = page_table.shape
    H = poolT.shape[1]
    gs = pltpu.PrefetchScalarGridSpec(
        num_scalar_prefetch=1,
        grid=(B, n_pages + 1),
        in_specs=[pl.BlockSpec((1, H, LANES), lambda b, p, pt: (pt[b, jnp.minimum(p, n_pages - 1)], 0, 0)),
                  pl.BlockSpec((1, H, LANES), lambda b, p, pt: (b, 0, 0))],
        out_specs=pl.BlockSpec((1, H, LANES), lambda b, p, pt: (b, 0, p)),
        scratch_shapes=[pltpu.VMEM((H, LANES), F32)])
    return pl.pallas_call(
        functools.partial(_cumsum_paged_kernel, n_pages=n_pages),
        grid_spec=gs,
        out_shape=jax.ShapeDtypeStruct((B, H, (n_pages + 1) * LANES), F32),
        compiler_params=_cparams("parallel", "arbitrary"),
    )(page_table, poolT, newT)


def _softmax_step(logits, v_bf, m_ref, l_ref, acc_ref, rows):
    m_old = m_ref[rows, :]
    m_new = jnp.maximum(m_old, jnp.max(logits, axis=1, keepdims=True))
    alpha = jnp.exp(m_old - m_new)
    p = jnp.exp(logits - m_new)
    l_ref[rows, :] = alpha * l_ref[rows, :] + jnp.sum(p, axis=1, keepdims=True)
    acc_ref[rows, :] = alpha * acc_ref[rows, :] + jnp.dot(_bf(p), v_bf, preferred_element_type=F32)
    m_ref[rows, :] = m_new


def _head_lane_mask(n_rows, width, head):
    lane = lax.broadcasted_iota(jnp.int32, (n_rows, width), 1)
    return (lane // HEAD_DIM) == head


def _fox_prompt_kernel(q_ref, k_ref, v_ref, c_ref, o_ref, qs_ref, m_ref, l_ref, acc_ref, *, tq, hpg):
    g = pl.program_id(1)
    qi = pl.program_id(2)
    ki = pl.program_id(3)

    @pl.when(ki == 0)
    def _():
        q = q_ref[0] * (HEAD_DIM ** -0.5)
        for h in range(hpg):
            qs_ref[h * tq:(h + 1) * tq, :] = _bf(jnp.where(_head_lane_mask(tq, GROUP_LANES, h), q, 0.0))
        m_ref[...] = jnp.full_like(m_ref, NEG_INF)
        l_ref[...] = jnp.zeros_like(l_ref)
        acc_ref[...] = jnp.zeros_like(acc_ref)

    @pl.when(ki <= qi)
    def _():
        k_bf = _bf(k_ref[0])
        v_bf = _bf(v_ref[0])
        qpos = qi * tq + lax.broadcasted_iota(jnp.int32, (tq, tq), 0)
        kpos = ki * tq + lax.broadcasted_iota(jnp.int32, (tq, tq), 1)
        causal = qpos >= kpos
        for h in range(hpg):
            rows = slice(h * tq, (h + 1) * tq)
            s = lax.dot_general(qs_ref[rows, :], k_bf, (((1,), (1,)), ((), ())), preferred_element_type=F32)
            s = s - c_ref[0, pl.ds(g * hpg + h, 1), :]
            s = jnp.where(causal, s, NEG_INF)
            _softmax_step(s, v_bf, m_ref, l_ref, acc_ref, rows)

    @pl.when(ki == qi)
    def _():
        out = jnp.zeros((tq, GROUP_LANES), F32)
        for h in range(hpg):
            rows = slice(h * tq, (h + 1) * tq)
            out = out + jnp.where(_head_lane_mask(tq, GROUP_LANES, h), acc_ref[rows, :] / l_ref[rows, :], 0.0)
        o_ref[...] = _bf(out)


def fox_prompt(pab, cT, B, T):
    tq = _pick(T, (256, 128))
    nq = T // tq
    hpg = GROUP_LANES // HEAD_DIM
    G = B_WIDTH // GROUP_LANES
    return pl.pallas_call(
        functools.partial(_fox_prompt_kernel, tq=tq, hpg=hpg),
        grid=(B, G, nq, nq),
        in_specs=[pl.BlockSpec((1, tq, GROUP_LANES), lambda b, g, qi, ki: (1, b * nq + qi, g)),
                  pl.BlockSpec((1, tq, GROUP_LANES), lambda b, g, qi, ki: (1, b * nq + jnp.minimum(ki, qi), G + g)),
                  pl.BlockSpec((1, tq, GROUP_LANES), lambda b, g, qi, ki: (1, b * nq + jnp.minimum(ki, qi), 2 * G + g)),
                  pl.BlockSpec((1, B_HEADS, tq), lambda b, g, qi, ki: (b, 0, jnp.minimum(ki, qi)))],
        out_specs=pl.BlockSpec((tq, GROUP_LANES), lambda b, g, qi, ki: (b * nq + qi, g)),
        out_shape=jax.ShapeDtypeStruct((B * T, B_WIDTH), BF16),
        scratch_shapes=[pltpu.VMEM((hpg * tq, GROUP_LANES), BF16),
                        pltpu.VMEM((hpg * tq, 1), F32),
                        pltpu.VMEM((hpg * tq, 1), F32),
                        pltpu.VMEM((hpg * tq, GROUP_LANES), F32)],
        compiler_params=_cparams("parallel", "parallel", "parallel", "arbitrary"),
    )(pab, pab, pab, cT)


def _fox_sample_kernel(pt_ref, q_ref, kn_ref, vn_ref, kc_ref, vc_ref, c_ref, o_ref,
                       qs_ref, kpad_ref, vpad_ref, m_ref, l_ref, acc_ref, *, T, n_pages, hpg, G):
    p = pl.program_id(1)
    R = hpg * T

    @pl.when(p == 0)
    def _():
        q = q_ref[0] * (HEAD_DIM ** -0.5)
        for g in range(G):
            qg = q[:, g * GROUP_LANES:(g + 1) * GROUP_LANES]
            for h in range(hpg):
                r0 = g * R + h * T
                qs_ref[r0:r0 + T, :] = _bf(jnp.where(_head_lane_mask(T, GROUP_LANES, h), qg, 0.0))
        m_ref[...] = jnp.full_like(m_ref, NEG_INF)
        l_ref[...] = jnp.zeros_like(l_ref)
        acc_ref[...] = jnp.zeros_like(acc_ref)
        kpad_ref[...] = jnp.zeros_like(kpad_ref)
        vpad_ref[...] = jnp.zeros_like(vpad_ref)
        kpad_ref[0:T, :] = kn_ref[0]
        vpad_ref[0:T, :] = vn_ref[0]

    def attend(k_all, v_all, mask):
        for g in range(G):
            k_bf = _bf(k_all[:, g * GROUP_LANES:(g + 1) * GROUP_LANES])
            v_bf = _bf(v_all[:, g * GROUP_LANES:(g + 1) * GROUP_LANES])
            rows = slice(g * R, (g + 1) * R)
            s = lax.dot_general(qs_ref[rows, :], k_bf, (((1,), (1,)), ((), ())), preferred_element_type=F32)
            bias = jnp.concatenate(
                [jnp.broadcast_to(c_ref[0, g * hpg + h:g * hpg + h + 1, :], (T, LANES)) for h in range(hpg)], axis=0)
            s = s - bias
            if mask is not None:
                s = jnp.where(mask, s, NEG_INF)
            _softmax_step(s, v_bf, m_ref, l_ref, acc_ref, rows)

    @pl.when(p < n_pages)
    def _():
        attend(kc_ref[0], vc_ref[0], None)

    @pl.when(p == n_pages)
    def _():
        row = lax.broadcasted_iota(jnp.int32, (R, LANES), 0)
        key = lax.broadcasted_iota(jnp.int32, (R, LANES), 1)
        attend(kpad_ref[...], vpad_ref[...], key <= (row % T))
        outs = []
        for g in range(G):
            out = jnp.zeros((T, GROUP_LANES), F32)
            for h in range(hpg):
                rows = slice(g * R + h * T, g * R + (h + 1) * T)
                out = out + jnp.where(_head_lane_mask(T, GROUP_LANES, h), acc_ref[rows, :] / l_ref[rows, :], 0.0)
            outs.append(out)
        o_ref[0] = _bf(jnp.concatenate(outs, axis=1))


def fox_sample(pab, cT, cache_k, cache_v, page_table, B, T):
    n_pages = page_table.shape[1]
    page = cache_k.shape[1]
    hpg = GROUP_LANES // HEAD_DIM
    G = B_WIDTH // GROUP_LANES
    W = B_WIDTH
    cache_spec = pl.BlockSpec((1, page, W), lambda b, p, pt: (pt[b, jnp.minimum(p, n_pages - 1)], 0, 0))
    gs = pltpu.PrefetchScalarGridSpec(
        num_scalar_prefetch=1,
        grid=(B, n_pages + 1),
        in_specs=[pl.BlockSpec((1, T, W), lambda b, p, pt: (1, b, 0)),
                  pl.BlockSpec((1, T, W), lambda b, p, pt: (1, b, 1)),
                  pl.BlockSpec((1, T, W), lambda b, p, pt: (1, b, 2)),
                  cache_spec, cache_spec,
                  pl.BlockSpec((1, B_HEADS, LANES), lambda b, p, pt: (b, 0, p))],
        out_specs=pl.BlockSpec((1, T, W), lambda b, p, pt: (b, 0, 0)),
        scratch_shapes=[pltpu.VMEM((G * hpg * T, GROUP_LANES), BF16),
                        pltpu.VMEM((page, W), F32),
                        pltpu.VMEM((page, W), F32),
                        pltpu.VMEM((G * hpg * T, 1), F32),
                        pltpu.VMEM((G * hpg * T, 1), F32),
                        pltpu.VMEM((G * hpg * T, GROUP_LANES), F32)])
    out = pl.pallas_call(
        functools.partial(_fox_sample_kernel, T=T, n_pages=n_pages, hpg=hpg, G=G),
        grid_spec=gs,
        out_shape=jax.ShapeDtypeStruct((B, T, W), BF16),
        compiler_params=_cparams("parallel", "arbitrary"),
    )(page_table, pab, pab, pab, cache_k, cache_v, cT)
    return out.reshape(B * T, W)


def _moba_slopes(n_rows, rows_per_head, i, kv_base):
    c = lax.broadcasted_iota(jnp.int32, (n_rows, 1), 0) // rows_per_head
    head = C_REP * (kv_base + c) + i
    return jnp.exp2(-8.0 * (head + 1).astype(F32) / C_HEADS)


def _moba_select(gate, own):
    lane = lax.broadcasted_iota(jnp.int32, gate.shape, 1).astype(F32)
    past = lane < own
    z = jnp.where(past, gate, NEG_INF)
    sel = jnp.zeros(gate.shape, F32)
    for _ in range(MOBA_TOPK):
        m = jnp.max(z, axis=1, keepdims=True)
        idx = jnp.min(jnp.where((z == m) & past, lane, float(LANES)), axis=1, keepdims=True)
        pick = lane == idx
        sel = jnp.where(pick, 1.0, sel)
        z = jnp.where(pick, NEG_INF, z)
    return sel


def _moba_prompt_kernel(q0_ref, q1_ref, q2_ref, q3_ref, k_ref, v_ref, o_ref,
                        kb_ref, vb_ref, km_ref, qs_ref, sel_ref, m_ref, l_ref, acc_ref, *, tq, nb, cpg):
    g = pl.program_id(1)
    qi = pl.program_id(2)
    R = cpg * tq
    blk = MOBA_BLOCK

    @pl.when(qi == 0)
    def _():
        km_ref[...] = jnp.zeros_like(km_ref)
        for n in range(nb):
            kblk = k_ref[0, n * blk:(n + 1) * blk, :]
            kb_ref[n * blk:(n + 1) * blk, :] = _bf(kblk)
            vb_ref[n * blk:(n + 1) * blk, :] = _bf(v_ref[0, n * blk:(n + 1) * blk, :])
            km_ref[n:n + 1, :] = jnp.sum(kblk, axis=0, keepdims=True) * (1.0 / blk)

    own = (qi * tq) // blk
    own_start = pl.multiple_of(own * blk, blk)
    qpos = qi * tq + (lax.broadcasted_iota(jnp.int32, (R, blk), 0) % tq)
    koff = lax.broadcasted_iota(jnp.int32, (R, blk), 1)
    rows = slice(0, R)
    km_hi, km_lo = _split2(km_ref[...])
    for i, q_ref in enumerate((q0_ref, q1_ref, q2_ref, q3_ref)):
        q = q_ref[0] * (HEAD_DIM ** -0.5)
        qs = jnp.concatenate([jnp.where(_head_lane_mask(tq, GROUP_LANES, c), q, 0.0) for c in range(cpg)], axis=0)
        qs_ref[...] = _bf(qs)
        sel_ref[...] = _moba_select(_dot3_nt(qs, km_hi, km_lo), own)
        slope = _moba_slopes(R, tq, i, g * cpg)
        m_ref[...] = jnp.full_like(m_ref, NEG_INF)
        l_ref[...] = jnp.zeros_like(l_ref)
        acc_ref[...] = jnp.zeros_like(acc_ref)

        def attend(start, mask, slope=slope):
            k_bf = kb_ref[pl.ds(start, blk), :]
            v_bf = vb_ref[pl.ds(start, blk), :]
            s = lax.dot_general(qs_ref[...], k_bf, (((1,), (1,)), ((), ())), preferred_element_type=F32)
            dist = qpos - (start + koff)
            s = s - slope * dist.astype(F32)
            s = jnp.where(mask(dist), s, NEG_INF)
            _softmax_step(s, v_bf, m_ref, l_ref, acc_ref, rows)

        attend(own_start, lambda dist: dist >= 0)
        for n in range(nb - 1):
            @pl.when(n < own)
            def _(n=n):
                attend(n * blk, lambda dist: sel_ref[:, n:n + 1] > 0.5)

        out = jnp.zeros((tq, GROUP_LANES), F32)
        for c in range(cpg):
            rs = slice(c * tq, (c + 1) * tq)
            out = out + jnp.where(_head_lane_mask(tq, GROUP_LANES, c), acc_ref[rs, :] / l_ref[rs, :], 0.0)
        o_ref[i] = _bf(out)


def moba_prompt(p1, B, T):
    tq = _pick(T, (128,))
    nq = T // tq
    nb = T // MOBA_BLOCK
    cpg = GROUP_LANES // HEAD_DIM
    G = C_KV_WIDTH // GROUP_LANES
    R = cpg * tq
    qspec = lambda i: pl.BlockSpec((1, tq, GROUP_LANES), lambda b, g, qi, i=i: (i, b * nq + qi, g))
    return pl.pallas_call(
        functools.partial(_moba_prompt_kernel, tq=tq, nb=nb, cpg=cpg),
        grid=(B, G, nq),
        in_specs=[qspec(0), qspec(1), qspec(2), qspec(3),
                  pl.BlockSpec((1, T, GROUP_LANES), lambda b, g, qi: (4, b, g)),
                  pl.BlockSpec((1, T, GROUP_LANES), lambda b, g, qi: (5, b, g))],
        out_specs=pl.BlockSpec((C_REP, tq, GROUP_LANES), lambda b, g, qi: (0, b * nq + qi, g)),
        out_shape=jax.ShapeDtypeStruct((C_REP, B * T, C_KV_WIDTH), BF16),
        scratch_shapes=[pltpu.VMEM((T, GROUP_LANES), BF16),
                        pltpu.VMEM((T, GROUP_LANES), BF16),
                        pltpu.VMEM((LANES, GROUP_LANES), F32),
                        pltpu.VMEM((R, GROUP_LANES), BF16),
                        pltpu.VMEM((R, LANES), F32),
                        pltpu.VMEM((R, 1), F32),
                        pltpu.VMEM((R, 1), F32),
                        pltpu.VMEM((R, GROUP_LANES), F32)],
        compiler_params=_cparams("parallel", "parallel", "arbitrary"),
    )(p1, p1, p1, p1, p1, p1)


def _kmean_paged_kernel(pt_ref, k0_ref, k1_ref, o_ref):
    s = jnp.sum(k0_ref[0], axis=0, keepdims=True) + jnp.sum(k1_ref[0], axis=0, keepdims=True)
    o_ref[0] = jnp.broadcast_to(s * (1.0 / MOBA_BLOCK), o_ref.shape[1:])


def moba_kmean_paged(cache_k, page_table):
    B, n_pages = page_table.shape
    page, W = cache_k.shape[1:]
    nbp = n_pages * page // MOBA_BLOCK
    spec = lambda o: pl.BlockSpec((1, page, W), lambda b, n, pt, o=o: (pt[b, 2 * n + o], 0, 0))
    gs = pltpu.PrefetchScalarGridSpec(
        num_scalar_prefetch=1, grid=(B, nbp),
        in_specs=[spec(0), spec(1)],
        out_specs=pl.BlockSpec((1, 8, W), lambda b, n, pt: (b, n, 0)))
    return pl.pallas_call(
        _kmean_paged_kernel, grid_spec=gs,
        out_shape=jax.ShapeDtypeStruct((B, nbp * 8, W), F32),
        compiler_params=_cparams("parallel", "arbitrary"),
    )(page_table, cache_k, cache_k)


def _moba_sample_kernel(pt_ref, q_ref, kn_ref, vn_ref, km_ref, k0_ref, k1_ref, v0_ref, v1_ref, o_ref,
                        qs_ref, sel_ref, kpad_ref, vpad_ref, m_ref, l_ref, acc_ref, *, T, nbp, cpg, G, q_start):
    s_id = pl.program_id(1)
    R = cpg * T
    page = k0_ref.shape[1]
    blk = MOBA_BLOCK

    def slopes():
        rows = []
        for i in range(C_REP):
            for g in range(G):
                rows.append(_moba_slopes(R, T, i, g * cpg))
        return rows

    def block_index(i, g):
        return (i * G + g) * R

    def attend(k_parts, v_parts, key_pos0, mask_fn):
        sl = slopes()
        for i in range(C_REP):
            for g in range(G):
                r0 = block_index(i, g)
                rows = slice(r0, r0 + R)
                lanes = slice(g * GROUP_LANES, (g + 1) * GROUP_LANES)
                k_bf = jnp.concatenate([_bf(kp[:, lanes]) for kp in k_parts], axis=0)
                v_bf = jnp.concatenate([_bf(vp[:, lanes]) for vp in v_parts], axis=0)
                nkeys = k_bf.shape[0]
                s = lax.dot_general(qs_ref[rows, :], k_bf, (((1,), (1,)), ((), ())), preferred_element_type=F32)
                qpos = q_start + (lax.broadcasted_iota(jnp.int32, (R, nkeys), 0) % T)
                kpos = key_pos0 + lax.broadcasted_iota(jnp.int32, (R, nkeys), 1)
                dist = qpos - kpos
                s = s - sl[i * G + g] * dist.astype(F32)
                s = jnp.where(mask_fn(rows, dist), s, NEG_INF)
                _softmax_step(s, v_bf, m_ref, l_ref, acc_ref, rows)

    @pl.when(s_id == 0)
    def _():
        m_ref[...] = jnp.full_like(m_ref, NEG_INF)
        l_ref[...] = jnp.zeros_like(l_ref)
        acc_ref[...] = jnp.zeros_like(acc_ref)
        kpad_ref[...] = jnp.zeros_like(kpad_ref)
        vpad_ref[...] = jnp.zeros_like(vpad_ref)
        kpad_ref[0:T, :] = kn_ref[0]
        vpad_ref[0:T, :] = vn_ref[0]
        own = q_start // blk
        km = km_ref[0]
        for i in range(C_REP):
            qi = q_ref[i] * (HEAD_DIM ** -0.5)
            for g in range(G):
                lanes = slice(g * GROUP_LANES, (g + 1) * GROUP_LANES)
                qg = qi[:, lanes]
                qs = jnp.concatenate([jnp.where(_head_lane_mask(T, GROUP_LANES, c), qg, 0.0) for c in range(cpg)], axis=0)
                r0 = block_index(i, g)
                qs_ref[r0:r0 + R, :] = _bf(qs)
                km_hi, km_lo = _split2(km[:, lanes])
                sel_ref[r0:r0 + R, :] = _moba_select(_dot3_nt(qs, km_hi, km_lo), own)
        attend([kpad_ref[...]], [vpad_ref[...]], q_start, lambda rows, dist: (dist >= 0) & (dist < T))

    @pl.when(s_id > 0)
    def _():
        n = s_id - 1

        def mask_fn(rows, dist):
            lane = lax.broadcasted_iota(jnp.int32, (R, LANES), 1)
            chosen = jnp.max(jnp.where(lane == n, sel_ref[rows, :], 0.0), axis=1, keepdims=True)
            return chosen > 0.5

        attend([k0_ref[0], k1_ref[0]], [v0_ref[0], v1_ref[0]], n * blk, mask_fn)

    @pl.when(s_id == nbp)
    def _():
        for i in range(C_REP):
            outs = []
            for g in range(G):
                out = jnp.zeros((T, GROUP_LANES), F32)
                for c in range(cpg):
                    r0 = block_index(i, g) + c * T
                    rs = slice(r0, r0 + T)
                    out = out + jnp.where(_head_lane_mask(T, GROUP_LANES, c), acc_ref[rs, :] / l_ref[rs, :], 0.0)
                outs.append(out)
            o_ref[i] = _bf(jnp.concatenate(outs, axis=1))


def moba_sample(p1, kmean, cache_k, cache_v, page_table, B, T, q_start):
    n_pages = page_table.shape[1]
    page, W = cache_k.shape[1:]
    nbp = n_pages * page // MOBA_BLOCK
    cpg = GROUP_LANES // HEAD_DIM
    G = W // GROUP_LANES
    RT = C_REP * G * cpg * T
    cspec = lambda o: pl.BlockSpec(
        (1, page, W), lambda b, s, pt, o=o: (pt[b, 2 * jnp.maximum(s - 1, 0) + o], 0, 0))
    gs = pltpu.PrefetchScalarGridSpec(
        num_scalar_prefetch=1,
        grid=(B, nbp + 1),
        in_specs=[pl.BlockSpec((C_REP, T, W), lambda b, s, pt: (0, b, 0)),
                  pl.BlockSpec((1, T, W), lambda b, s, pt: (4, b, 0)),
                  pl.BlockSpec((1, T, W), lambda b, s, pt: (5, b, 0)),
                  pl.BlockSpec((1, LANES, W), lambda b, s, pt: (b, 0, 0)),
                  cspec(0), cspec(1), cspec(0), cspec(1)],
        out_specs=pl.BlockSpec((C_REP, T, W), lambda b, s, pt: (0, b, 0)),
        scratch_shapes=[pltpu.VMEM((RT, GROUP_LANES), BF16),
                        pltpu.VMEM((RT, LANES), F32),
                        pltpu.VMEM((LANES, W), F32),
                        pltpu.VMEM((LANES, W), F32),
                        pltpu.VMEM((RT, 1), F32),
                        pltpu.VMEM((RT, 1), F32),
                        pltpu.VMEM((RT, GROUP_LANES), F32)])
    return pl.pallas_call(
        functools.partial(_moba_sample_kernel, T=T, nbp=nbp, cpg=cpg, G=G, q_start=q_start),
        grid_spec=gs,
        out_shape=jax.ShapeDtypeStruct((C_REP, B * T, W), BF16),
        compiler_params=_cparams("parallel", "arbitrary"),
    )(page_table, p1, p1, p1, kmean, cache_k, cache_k, cache_v, cache_v)


def _row(v):
    return v.reshape(1, -1).astype(F32)


def _prep_params(norm0_mix_g, w_in0, fox_b_f, rwkv_mu, rwkv_w0, rwkv_w2, rwkv_a0, rwkv_a2, rwkv_g2, rwkv_k_k,
                 rwkv_k_a, rwkv_r_k, rwkv_ln_w, rwkv_ln_b, w_out0, norm0_ffn_g, ffn_w_gate, ffn_w_up,
                 ffn_w_down, norm1_mix_g, w_in1, w_out1, norm1_ffn_g, router_w, router_b, moe_w_gate,
                 moe_w_up, moe_w_down, norm_final_g):
    D = w_in0.shape[0]
    pad_b = A_PROJ - (3 * B_WIDTH + B_HEADS)
    w0 = jnp.concatenate([w_in0, jnp.zeros((D, pad_b), F32)], axis=1)
    wq = w_in1[:, :C_WIDTH].reshape(D, C_KV_HEADS, C_REP, HEAD_DIM).transpose(0, 2, 1, 3).reshape(D, C_WIDTH)
    w1 = jnp.concatenate([wq, w_in1[:, C_WIDTH:]], axis=1)
    wo1 = w_out1.reshape(C_KV_HEADS, C_REP, HEAD_DIM, D).transpose(1, 0, 2, 3).reshape(C_WIDTH, D)
    hd = lax.broadcasted_iota(jnp.int32, (A_WIDTH, A_WIDTH), 0) // HEAD_DIM
    hd2 = lax.broadcasted_iota(jnp.int32, (A_WIDTH, A_WIDTH), 1) // HEAD_DIM
    rw = jnp.concatenate([router_w, jnp.zeros((D, LANES - N_EXPERTS), F32)], axis=1)
    rb = jnp.concatenate([router_b, jnp.zeros((LANES - N_EXPERTS,), F32)])
    E, _, FE = moe_w_gate.shape
    return dict(
        g0=norm0_mix_g, w0=_bf(w0),
        fox_b=jnp.concatenate([fox_b_f, jnp.zeros((LANES - B_HEADS,), F32)]).reshape(1, LANES),
        rwkv=dict(mu=_row(rwkv_mu), w0=_row(rwkv_w0), w2=_split2(rwkv_w2), a0=_row(rwkv_a0), a2=_split2(rwkv_a2),
                  g2=_split2(rwkv_g2), k_k=_row(rwkv_k_k), k_a=_row(rwkv_k_a), r_k=_row(rwkv_r_k),
                  ln_w=_row(rwkv_ln_w), ln_b=_row(rwkv_ln_b), bd=(hd == hd2).astype(BF16)),
        wo0=_bf(w_out0), g0f=norm0_ffn_g,
        ffn_g=_bf(ffn_w_gate)[None], ffn_u=_bf(ffn_w_up)[None], ffn_d=_bf(ffn_w_down),
        g1=norm1_mix_g, w1=_bf(w1), wo1=_bf(wo1), g1f=norm1_ffn_g,
        router=(*_split2(rw), rb.reshape(1, LANES)),
        moe_g=_bf(moe_w_gate), moe_u=_bf(moe_w_up), moe_d=_bf(moe_w_down).reshape(E * FE, D),
        gf=norm_final_g)


def _pair_states(S):
    B = S.shape[0]
    S = S.reshape(B, A_HEADS // 2, 2, HEAD_DIM, HEAD_DIM)
    z = jnp.zeros_like(S[:, :, 0])
    top = jnp.concatenate([S[:, :, 0], z], axis=-1)
    bot = jnp.concatenate([z, S[:, :, 1]], axis=-1)
    return jnp.concatenate([top, bot], axis=-2)


def _unpair_states(S):
    B = S.shape[0]
    a = S[:, :, :HEAD_DIM, :HEAD_DIM]
    b = S[:, :, HEAD_DIM:, HEAD_DIM:]
    return jnp.stack([a, b], axis=2).reshape(B, A_HEADS, HEAD_DIM, HEAD_DIM)


def _run(P, x, S0, shift0, caches, page_table):
    B, T, D = x.shape
    M = B * T
    xt = x.reshape(M, D)

    pab = norm_matmul(xt, P["g0"], P["w0"], A_PROJ, A_PROJ // 2)
    pa = pab[0].reshape(B, T, A_PROJ)
    p_prev = jnp.concatenate([shift0[:, None, :], pa[:, :-1]], axis=1).reshape(M, A_PROJ)
    r, lw, k, v, kn, bb, gg = rwkv_prep(pab, p_prev, P["rwkv"])
    y, s_bd = rwkv_chunk(r, lw, k, v, kn, bb, _pair_states(S0), B, T)
    ya = rwkv_post(y, r, k, v, gg, P["rwkv"])
    S_new = _unpair_states(s_bd)
    shift_new = pa[:, -1]

    pb = pab[1]
    fk = pb[:, B_WIDTH:2 * B_WIDTH].reshape(B, T, B_HEADS, HEAD_DIM)
    fv = pb[:, 2 * B_WIDTH:3 * B_WIDTH].reshape(B, T, B_HEADS, HEAD_DIM)
    logf = fox_logf(pab, P["fox_b"], 3 * B_WIDTH // LANES)[:, :B_HEADS].reshape(B, T, B_HEADS)
    logfT = jnp.swapaxes(logf, 1, 2)
    if caches is None:
        yb = fox_prompt(pab, cumsum_lanes(logfT), B, T)
    else:
        n_pool, page = caches["fox_k"].shape[:2]
        poolT = jnp.swapaxes(caches["fox_logf"], 1, 2)
        newT = jnp.pad(logfT, ((0, 0), (0, 0), (0, LANES - T)))
        cT = cumsum_lanes_paged(page_table, poolT, newT)
        yb = fox_sample(pab, cT, caches["fox_k"].reshape(n_pool, page, B_WIDTH),
                        caches["fox_v"].reshape(n_pool, page, B_WIDTH), page_table, B, T)
    yab = jnp.concatenate([ya, yb], axis=1)[None]
    x1 = matmul_res(yab, P["wo0"], xt, tk=A_WIDTH + B_WIDTH)

    h = swiglu_up(x1, P["g0f"], P["ffn_g"], P["ffn_u"])
    x2 = matmul_res(h[None], P["ffn_d"], x1, tk=_pick(h.shape[1], (1408, 1024, 512, 256, 128)))

    p1 = norm_matmul(x2, P["g1"], P["w1"], C_KV_WIDTH, C_KV_WIDTH)
    mk = p1[4].reshape(B, T, C_KV_HEADS, HEAD_DIM)
    mv = p1[5].reshape(B, T, C_KV_HEADS, HEAD_DIM)
    if caches is None:
        y1 = moba_prompt(p1, B, T)
    else:
        n_pool, page = caches["moba_k"].shape[:2]
        ck = caches["moba_k"].reshape(n_pool, page, C_KV_WIDTH)
        cv = caches["moba_v"].reshape(n_pool, page, C_KV_WIDTH)
        km = moba_kmean_paged(ck, page_table)
        nbp = km.shape[1] // 8
        km = jnp.pad(km[:, ::8], ((0, 0), (0, LANES - nbp), (0, 0)))
        y1 = moba_sample(p1, km, ck, cv, page_table, B, T, page_table.shape[1] * page)
    x3 = matmul_res(y1, P["wo1"], x2, tk=C_KV_WIDTH)

    hm = swiglu_up(x3, P["g1f"], P["moe_g"], P["moe_u"], router=P["router"])
    out = matmul_res(hm[None], P["moe_d"], x3, tk=_pick(hm.shape[1], (1408, 1024, 512, 256, 128)), final_g=P["gf"])
    return out.reshape(B, T, D), S_new, shift_new, fk, fv, logf, mk, mv


def kernel(x_prompt, x_sample, state_rwkv_S, state_rwkv_shift, cache_fox_k, cache_fox_v, cache_fox_logf,
           cache_moba_k, cache_moba_v, page_table, norm0_mix_g, w_in0, fox_b_f, rwkv_mu, rwkv_w0, rwkv_w2,
           rwkv_a0, rwkv_a2, rwkv_g2, rwkv_k_k, rwkv_k_a, rwkv_r_k, rwkv_ln_w, rwkv_ln_b, w_out0, norm0_ffn_g,
           ffn_w_gate, ffn_w_up, ffn_w_down, norm1_mix_g, w_in1, w_out1, norm1_ffn_g, router_w, router_b,
           moe_w_gate, moe_w_up, moe_w_down, norm_final_g):
    P = _prep_params(norm0_mix_g, w_in0, fox_b_f, rwkv_mu, rwkv_w0, rwkv_w2, rwkv_a0, rwkv_a2, rwkv_g2,
                     rwkv_k_k, rwkv_k_a, rwkv_r_k, rwkv_ln_w, rwkv_ln_b, w_out0, norm0_ffn_g, ffn_w_gate,
                     ffn_w_up, ffn_w_down, norm1_mix_g, w_in1, w_out1, norm1_ffn_g, router_w, router_b,
                     moe_w_gate, moe_w_up, moe_w_down, norm_final_g)
    n_prompt = x_prompt.shape[0]
    prompt = _run(P, x_prompt,
                  jnp.zeros((n_prompt, A_HEADS, HEAD_DIM, HEAD_DIM), F32),
                  jnp.zeros((n_prompt, A_PROJ), x_prompt.dtype), None, None)
    caches = dict(fox_k=cache_fox_k, fox_v=cache_fox_v, fox_logf=cache_fox_logf,
                  moba_k=cache_moba_k, moba_v=cache_moba_v)
    sample = _run(P, x_sample, state_rwkv_S, state_rwkv_shift, caches, page_table)
    return (prompt[0], sample[0], *prompt[1:], *sample[1:])
```

```python
import functools

import jax
import jax.numpy as jnp
from jax import lax
from jax.experimental import pallas as pl
from jax.experimental.pallas import tpu as pltpu

F32 = jnp.float32
BF16 = jnp.bfloat16

HEAD_DIM = 64
A_HEADS = 16
A_WIDTH = A_HEADS * HEAD_DIM
A_LORA_W = 64
A_LORA_A = 64
A_LORA_G = 128
A_PROJ = 3 * A_WIDTH + A_LORA_W + A_LORA_A + A_LORA_G
RWKV_GN_EPS = 64e-5
B_HEADS = 16
B_WIDTH = B_HEADS * HEAD_DIM
C_HEADS = 32
C_KV_HEADS = 8
C_REP = C_HEADS // C_KV_HEADS
C_WIDTH = C_HEADS * HEAD_DIM
C_KV_WIDTH = C_KV_HEADS * HEAD_DIM
MOBA_BLOCK = 256
MOBA_TOPK = 3
N_EXPERTS = 8
NORM_EPS = 1e-6

LANES = 128
GROUP_LANES = 256
VMEM_LIMIT = 56 * 1024 * 1024
NEG_INF = float("-inf")
LOG2E = 1.4426950408889634
NT_DIMS = (((1,), (1,)), ((), ()))


def _cparams(*sem):
    return pltpu.CompilerParams(dimension_semantics=sem, vmem_limit_bytes=VMEM_LIMIT)


def _bf(x):
    return x.astype(BF16)


def _dot(a, b):
    return jnp.dot(_bf(a), _bf(b), preferred_element_type=F32)


def _dot_nt(a, b):
    return lax.dot_general(_bf(a), _bf(b), NT_DIMS, preferred_element_type=F32)


def _dot_tn(a, b):
    return lax.dot_general(_bf(a), _bf(b), (((0,), (0,)), ((), ())), preferred_element_type=F32)


def _split2(x):
    hi = _bf(x)
    lo = _bf(x - hi.astype(F32))
    return hi, lo


def _split3(x):
    hi = _bf(x)
    r1 = x - hi.astype(F32)
    mid = _bf(r1)
    lo = _bf(r1 - mid.astype(F32))
    return hi, mid, lo


def _dot3(a, b_hi, b_lo):
    a_hi, a_lo = _split2(a)
    return (jnp.dot(a_hi, b_hi, preferred_element_type=F32)
            + (jnp.dot(a_hi, b_lo, preferred_element_type=F32)
               + jnp.dot(a_lo, b_hi, preferred_element_type=F32)))


def _dot3_nt(a, b_hi, b_lo):
    a_hi, a_lo = _split2(a)
    return (lax.dot_general(a_hi, b_hi, NT_DIMS, preferred_element_type=F32)
            + (lax.dot_general(a_hi, b_lo, NT_DIMS, preferred_element_type=F32)
               + lax.dot_general(a_lo, b_hi, NT_DIMS, preferred_element_type=F32)))


def _dot_exact_rhs(a, b_exact):
    a_hi, a_lo = _split2(a)
    return (jnp.dot(a_hi, b_exact, preferred_element_type=F32)
            + jnp.dot(a_lo, b_exact, preferred_element_type=F32))


def _dot_exact_lhs3(a_exact, x):
    hi, mid, lo = _split3(x)
    return (jnp.dot(a_exact, hi, preferred_element_type=F32)
            + (jnp.dot(a_exact, mid, preferred_element_type=F32)
               + jnp.dot(a_exact, lo, preferred_element_type=F32)))


def _sigmoid(x):
    return 1.0 / (1.0 + jnp.exp(-x))


def _pick(n, prefs):
    for p in prefs:
        if n % p == 0:
            return p
    return n


def _norm_mm_kernel(x_ref, g_ref, w_ref, o_ref, xn_ref):
    @pl.when(pl.program_id(1) == 0)
    def _():
        x = x_ref[...]
        ms = jnp.mean(x * x, axis=-1, keepdims=True)
        xn_ref[...] = _bf(x * lax.rsqrt(ms + NORM_EPS) * g_ref[...])

    o_ref[0] = jnp.dot(xn_ref[...], w_ref[...], preferred_element_type=F32)


def norm_matmul(x, g, w, cw, tn):
    M, K = x.shape
    N = w.shape[1]
    nc = N // cw
    tm = _pick(M, (1024, 512, 256, 128, 8))
    per = cw // tn
    return pl.pallas_call(
        _norm_mm_kernel,
        grid=(M // tm, N // tn),
        in_specs=[pl.BlockSpec((tm, K), lambda i, j: (i, 0)),
                  pl.BlockSpec((1, K), lambda i, j: (0, 0)),
                  pl.BlockSpec((K, tn), lambda i, j: (0, j))],
        out_specs=pl.BlockSpec((1, tm, tn), lambda i, j: (j // per, i, j % per)),
        out_shape=jax.ShapeDtypeStruct((nc, M, cw), F32),
        scratch_shapes=[pltpu.VMEM((tm, K), BF16)],
        compiler_params=_cparams("parallel", "arbitrary"),
        name="norm_matmul",
    )(x, g.reshape(1, K), w)


def _mm_res_kernel(a_ref, w_ref, r_ref, *rest, nk, final_norm):
    if final_norm:
        g_ref, o_ref, acc_ref = rest
    else:
        o_ref, acc_ref = rest
    k = pl.program_id(2)

    @pl.when(k == 0)
    def _():
        acc_ref[...] = r_ref[...]

    acc_ref[...] += jnp.dot(a_ref[0], w_ref[...], preferred_element_type=F32)

    @pl.when(k == nk - 1)
    def _():
        y = acc_ref[...]
        if final_norm:
            ms = jnp.mean(y * y, axis=-1, keepdims=True)
            y = y * lax.rsqrt(ms + NORM_EPS) * g_ref[...]
        o_ref[...] = y


def matmul_res(a, w, res, tk, final_g=None):
    ka, M, kw = a.shape
    N = w.shape[1]
    tm = _pick(M, (512, 256, 128, 8))
    tn = N if final_g is not None else _pick(N, (1024, 512, 256, 128))
    per = kw // tk
    nk = ka * per
    in_specs = [pl.BlockSpec((1, tm, tk), lambda i, j, k: (k // per, i, k % per)),
                pl.BlockSpec((tk, tn), lambda i, j, k: (k, j)),
                pl.BlockSpec((tm, tn), lambda i, j, k: (i, j))]
    args = [a, w, res]
    if final_g is not None:
        in_specs.append(pl.BlockSpec((1, tn), lambda i, j, k: (0, 0)))
        args.append(final_g.reshape(1, N))
    return pl.pallas_call(
        functools.partial(_mm_res_kernel, nk=nk, final_norm=final_g is not None),
        grid=(M // tm, N // tn, nk),
        in_specs=in_specs,
        out_specs=pl.BlockSpec((tm, tn), lambda i, j, k: (i, j)),
        out_shape=jax.ShapeDtypeStruct((M, N), F32),
        scratch_shapes=[pltpu.VMEM((tm, tn), F32)],
        compiler_params=_cparams("parallel", "parallel", "arbitrary"),
        name="matmul_res",
    )(*args)


def _swiglu_up_kernel(x_ref, g_ref, wg_ref, wu_ref, *rest, routed):
    if routed:
        rwh_ref, rwl_ref, rb_ref, h_ref, xn_ref, gate_ref = rest
    else:
        h_ref, xn_ref = rest
    e = pl.program_id(1)

    @pl.when((e == 0) & (pl.program_id(2) == 0))
    def _():
        x = x_ref[...]
        ms = jnp.mean(x * x, axis=-1, keepdims=True)
        xn = x * lax.rsqrt(ms + NORM_EPS) * g_ref[...]
        xn_ref[...] = _bf(xn)
        if routed:
            logits = _dot3(xn, rwh_ref[...], rwl_ref[...]) + rb_ref[...]
            lane = lax.broadcasted_iota(jnp.int32, logits.shape, 1).astype(F32)
            live = lane < N_EXPERTS
            z = jnp.where(live, logits, NEG_INF)
            m1 = jnp.max(z, axis=1, keepdims=True)
            i1 = jnp.min(jnp.where(z == m1, lane, float(LANES)), axis=1, keepdims=True)
            z2 = jnp.where(lane == i1, NEG_INF, z)
            m2 = jnp.max(z2, axis=1, keepdims=True)
            i2 = jnp.min(jnp.where(z2 == m2, lane, float(LANES)), axis=1, keepdims=True)
            e2 = jnp.exp(m2 - m1)
            den = 1.0 + e2
            gate_ref[...] = jnp.where(lane == i1, 1.0 / den, jnp.where(lane == i2, e2 / den, 0.0))

    xn = xn_ref[...]
    a = jnp.dot(xn, wg_ref[0], preferred_element_type=F32)
    b = jnp.dot(xn, wu_ref[0], preferred_element_type=F32)
    h = a * _sigmoid(a) * b
    if routed:
        gates = gate_ref[...]
        lane = lax.broadcasted_iota(jnp.int32, gates.shape, 1)
        h = h * jnp.sum(jnp.where(lane == e, gates, 0.0), axis=1, keepdims=True)
    h_ref[...] = _bf(h)


def swiglu_up(x, g, wg, wu, router=None):
    M, K = x.shape
    E, _, F = wg.shape
    tm = _pick(M, (1024, 512, 256, 128, 8))
    tn = _pick(F, (512, 256, 128))
    per = F // tn
    routed = router is not None
    in_specs = [pl.BlockSpec((tm, K), lambda i, e, j: (i, 0)),
                pl.BlockSpec((1, K), lambda i, e, j: (0, 0)),
                pl.BlockSpec((1, K, tn), lambda i, e, j: (e, 0, j)),
                pl.BlockSpec((1, K, tn), lambda i, e, j: (e, 0, j))]
    args = [x, g.reshape(1, K), wg, wu]
    scratch = [pltpu.VMEM((tm, K), BF16)]
    if routed:
        in_specs += [pl.BlockSpec((K, LANES), lambda i, e, j: (0, 0)),
                     pl.BlockSpec((K, LANES), lambda i, e, j: (0, 0)),
                     pl.BlockSpec((1, LANES), lambda i, e, j: (0, 0))]
        args += list(router)
        scratch.append(pltpu.VMEM((tm, LANES), F32))
    return pl.pallas_call(
        functools.partial(_swiglu_up_kernel, routed=routed),
        grid=(M // tm, E, per),
        in_specs=in_specs,
        out_specs=pl.BlockSpec((tm, tn), lambda i, e, j: (i, e * per + j)),
        out_shape=jax.ShapeDtypeStruct((M, E * F), BF16),
        scratch_shapes=scratch,
        compiler_params=_cparams("parallel", "arbitrary", "arbitrary"),
        name="moe_up" if routed else "ffn_up",
    )(*args)


def _rwkv_prep_kernel(p_ref, pp_ref, mu_ref, w0_ref, w2h_ref, w2l_ref, a0_ref, a2h_ref, a2l_ref,
                      g2h_ref, g2l_ref, kk_ref, ka_ref, bd_ref,
                      r_ref, lw_ref, k_ref, v_ref, kn_ref, b_ref, g_ref):
    p = p_ref[0]
    ps = p + (pp_ref[...] - p) * mu_ref[...]
    W = A_WIDTH
    r = ps[:, 0:W]
    k = ps[:, W:2 * W]
    v = ps[:, 2 * W:3 * W]
    o = 3 * W
    wd = ps[:, o:o + A_LORA_W]
    ad = ps[:, o + A_LORA_W:o + A_LORA_W + A_LORA_A]
    gd = ps[:, o + A_LORA_W + A_LORA_A:]
    z = -(w0_ref[...] + _dot3(jnp.tanh(wd), w2h_ref[...], w2l_ref[...]))
    softplus = jnp.maximum(z, 0.0) + jnp.log(1.0 + jnp.exp(-jnp.abs(z)))
    lw_ref[...] = -jnp.exp(-softplus - 0.5)
    a = _sigmoid(a0_ref[...] + _dot3(ad, a2h_ref[...], a2l_ref[...]))
    g_ref[...] = _dot3(_sigmoid(gd), g2h_ref[...], g2l_ref[...])
    kk = k * kk_ref[...]
    ss = _dot_exact_rhs(kk * kk, bd_ref[...])
    kn = kk / jnp.maximum(jnp.sqrt(ss), 1e-12)
    r_ref[...] = r
    v_ref[...] = v
    kn_ref[...] = kn
    b_ref[...] = kn * a
    k_ref[...] = k * (1.0 + (a - 1.0) * ka_ref[...])


def rwkv_prep(pab, p_prev, prm):
    M = p_prev.shape[0]
    tm = _pick(M, (256, 128, 8))
    W = A_WIDTH
    row = lambda n: pl.BlockSpec((1, n), lambda i: (0, 0))
    mat = lambda a, b: pl.BlockSpec((a, b), lambda i: (0, 0))
    out = pl.BlockSpec((tm, W), lambda i: (i, 0))
    return pl.pallas_call(
        _rwkv_prep_kernel,
        grid=(M // tm,),
        in_specs=[pl.BlockSpec((1, tm, A_PROJ), lambda i: (0, i, 0)),
                  pl.BlockSpec((tm, A_PROJ), lambda i: (i, 0)),
                  row(A_PROJ), row(W), mat(A_LORA_W, W), mat(A_LORA_W, W),
                  row(W), mat(A_LORA_A, W), mat(A_LORA_A, W),
                  mat(A_LORA_G, W), mat(A_LORA_G, W), row(W), row(W), mat(W, W)],
        out_specs=[out] * 7,
        out_shape=[jax.ShapeDtypeStruct((M, W), F32)] * 7,
        compiler_params=_cparams("parallel"),
        name="rwkv_prep",
    )(pab, p_prev, prm["mu"], prm["w0"], *prm["w2"], prm["a0"], *prm["a2"], *prm["g2"],
      prm["k_k"], prm["k_a"], prm["bd"])


def _rwkv_chunk_kernel(r_ref, lw_ref, k_ref, v_ref, kn_ref, b_ref, s0_ref, y_ref, s_ref, *, C, nc, Bb):
    C2 = 2 * C
    lane = lax.broadcasted_iota(jnp.int32, (C2, LANES), 1)
    rowi = lax.broadcasted_iota(jnp.int32, (C2, LANES), 0)
    mask2 = ((rowi >= C) == (lane >= HEAD_DIM)).astype(F32)
    ri = lax.broadcasted_iota(jnp.int32, (C2, C2), 0)
    ci = lax.broadcasted_iota(jnp.int32, (C2, C2), 1)
    same = (ri >= C) == (ci >= C)
    strict = same & (ri > ci)
    incl = same & (ri >= ci)
    eye = (ri == ci).astype(F32)
    ti = lax.broadcasted_iota(jnp.int32, (C, C), 0)
    tj = lax.broadcasted_iota(jnp.int32, (C, C), 1)
    tri = (ti >= tj).astype(BF16)
    n_levels = max(C.bit_length() - 2, 0)
    streams = range(Bb)

    def stack(x):
        return jnp.concatenate([x, x], axis=0) * mask2

    def each(f, *lists):
        return [f(*[l[i] for l in lists]) for i in streams]

    @pl.when(pl.program_id(2) == 0)
    def _():
        s_ref[...] = s0_ref[...]

    def chunk(c, states):
        sl = pl.ds(pl.multiple_of(c * C, C), C)
        S = list(states)
        load = lambda ref: [ref[i, sl, :] for i in streams]
        r, lw, k, v, kn, b = (load(ref) for ref in (r_ref, lw_ref, k_ref, v_ref, kn_ref, b_ref))
        cs = each(lambda x: _dot_exact_lhs3(tri, x), lw)
        gt = each(lambda x: x[C - 1:C, :], cs)
        e_neg = each(lambda x: jnp.exp(-x), cs)
        e_rem = each(lambda g, x: jnp.exp(g - x), gt, cs)
        KT = each(lambda a, x, l: stack(a * jnp.exp(x - l)), kn, cs, lw)
        BI = each(lambda a, e: stack(a * e), b, e_neg)
        KI = each(lambda a, e: stack(a * e), k, e_neg)
        RT = each(lambda a, x: stack(a * jnp.exp(x)), r, cs)
        V2 = each(stack, v)
        KG = each(lambda a, e: stack(a * e), k, e_rem)
        BG = each(lambda a, e: stack(a * e), b, e_rem)
        a_kb = each(_dot_nt, KT, BI)
        a_kv = each(_dot_nt, KT, KI)
        a_rb = each(_dot_nt, RT, BI)
        a_rk = each(_dot_nt, RT, KI)
        a_rb = each(lambda x: jnp.where(incl, x, 0.0), a_rb)
        a_rk = each(lambda x: jnp.where(incl, x, 0.0), a_rk)
        pw = each(lambda x: -jnp.where(strict, x, 0.0), a_kb)
        tinv = each(lambda x: eye + x, pw)
        for _ in range(n_levels):
            pw = each(_dot, pw, pw)
            tinv = each(lambda t, p: t + _dot(t, p), tinv, pw)
        kp = each(_dot, tinv, KT)
        av = each(lambda x, vv: _dot(jnp.where(strict, x, 0.0), vv), a_kv, V2)
        w1 = each(_dot, tinv, av)
        rp = each(lambda x, a, p: x - _dot(a, p), RT, a_rb, kp)
        y1 = each(lambda a, vv, ab, w: _dot(a, vv) - _dot(ab, w), a_rk, V2, a_rb, w1)
        mlow = each(_dot_tn, kp, BG)
        nt = each(lambda vv, kg, w, bg: _dot_tn(vv, kg) - _dot_tn(w, bg), V2, KG, w1, BG)
        y2 = each(lambda p, s, y: _dot3_nt(p, *_split2(s)) + y, rp, S, y1)
        for i in streams:
            y_ref[i, sl, :] = y2[i][:C] + y2[i][C:]
        new = each(lambda s, g, m, n: s * jnp.exp(g) - _dot3(s, *_split2(m)) + n, S, gt, mlow, nt)
        return tuple(new)

    final = lax.fori_loop(0, nc, chunk, tuple(s_ref[i, 0] for i in streams))
    for i in streams:
        s_ref[i, 0] = final[i]


def rwkv_chunk(r, lw, k, v, kn, b, s_bd, B, T):
    C = min(T, 64)
    Tt = _pick(T, (512, 256, 128, 64)) if T > C else T
    nc = Tt // C
    Bb = _pick(B, (4, 2, 1)) if T > C else _pick(B, (8, 4, 2, 1))
    HP = A_WIDTH // LANES
    seq = pl.BlockSpec((Bb, Tt, LANES), lambda i, h, t: (i, t, h))
    st = pl.BlockSpec((Bb, 1, LANES, LANES), lambda i, h, t: (i, h, 0, 0))
    r3 = lambda x: x.reshape(B, T, A_WIDTH)
    y, s = pl.pallas_call(
        functools.partial(_rwkv_chunk_kernel, C=C, nc=nc, Bb=Bb),
        grid=(B // Bb, HP, T // Tt),
        in_specs=[seq] * 6 + [st],
        out_specs=[seq, st],
        out_shape=[jax.ShapeDtypeStruct((B, T, A_WIDTH), F32),
                   jax.ShapeDtypeStruct((B, HP, LANES, LANES), F32)],
        compiler_params=_cparams("parallel", "parallel", "arbitrary"),
        name="rwkv_chunk",
    )(r3(r), r3(lw), r3(k), r3(v), r3(kn), r3(b), s_bd)
    return y.reshape(B * T, A_WIDTH), s


def _rwkv_post_kernel(y_ref, r_ref, k_ref, v_ref, g_ref, rk_ref, lnw_ref, lnb_ref, bd_ref, o_ref):
    y = y_ref[...]
    bd = bd_ref[...]
    inv_n = 1.0 / HEAD_DIM
    mean = _dot_exact_rhs(y, bd) * inv_n
    d = y - mean
    var = _dot_exact_rhs(d * d, bd) * inv_n
    yn = d * lax.rsqrt(var + RWKV_GN_EPS) * lnw_ref[...] + lnb_ref[...]
    bonus = _dot_exact_rhs(r_ref[...] * k_ref[...] * rk_ref[...], bd)
    o_ref[...] = _bf((yn + bonus * v_ref[...]) * g_ref[...])


def rwkv_post(y, r, k, v, g, prm):
    M, W = y.shape
    tm = _pick(M, (256, 128, 8))
    blk = pl.BlockSpec((tm, W), lambda i: (i, 0))
    row = pl.BlockSpec((1, W), lambda i: (0, 0))
    return pl.pallas_call(
        _rwkv_post_kernel,
        grid=(M // tm,),
        in_specs=[blk] * 5 + [row] * 3 + [pl.BlockSpec((W, W), lambda i: (0, 0))],
        out_specs=blk,
        out_shape=jax.ShapeDtypeStruct((M, W), BF16),
        compiler_params=_cparams("parallel"),
        name="rwkv_post",
    )(y, r, k, v, g, prm["r_k"], prm["ln_w"], prm["ln_b"], prm["bd"])


def _logf_kernel(f_ref, b_ref, o_ref):
    z = f_ref[0] + b_ref[...]
    o_ref[...] = jnp.minimum(z, 0.0) - jnp.log(1.0 + jnp.exp(-jnp.abs(z)))


def fox_logf(pab, bias_row, col_block):
    M = pab.shape[1]
    tm = _pick(M, (1024, 512, 256, 128, 8))
    return pl.pallas_call(
        _logf_kernel,
        grid=(M // tm,),
        in_specs=[pl.BlockSpec((1, tm, LANES), lambda i: (1, i, col_block)),
                  pl.BlockSpec((1, LANES), lambda i: (0, 0))],
        out_specs=pl.BlockSpec((tm, LANES), lambda i: (i, 0)),
        out_shape=jax.ShapeDtypeStruct((M, LANES), F32),
        compiler_params=_cparams("parallel"),
        name="fox_logf",
    )(pab, bias_row)


def _cumsum_kernel(x_ref, o_ref, carry_ref):
    @pl.when(pl.program_id(1) == 0)
    def _():
        carry_ref[...] = jnp.zeros_like(carry_ref)

    ti = lax.broadcasted_iota(jnp.int32, (LANES, LANES), 0)
    tj = lax.broadcasted_iota(jnp.int32, (LANES, LANES), 1)
    triu = (ti <= tj).astype(BF16)
    hi, mid, lo = _split3(x_ref[0])
    cs = (jnp.dot(hi, triu, preferred_element_type=F32)
          + (jnp.dot(mid, triu, preferred_element_type=F32)
             + jnp.dot(lo, triu, preferred_element_type=F32))) + carry_ref[...]
    o_ref[0] = cs
    carry_ref[...] = jnp.broadcast_to(cs[:, LANES - 1:LANES], cs.shape)


def cumsum_lanes(xT):
    B, H, L = xT.shape
    return pl.pallas_call(
        _cumsum_kernel,
        grid=(B, L // LANES),
        in_specs=[pl.BlockSpec((1, H, LANES), lambda b, p: (b, 0, p))],
        out_specs=pl.BlockSpec((1, H, LANES), lambda b, p: (b, 0, p)),
        out_shape=jax.ShapeDtypeStruct((B, H, L), F32),
        scratch_shapes=[pltpu.VMEM((H, LANES), F32)],
        compiler_params=_cparams("parallel", "arbitrary"),
        name="fox_cumsum",
    )(xT)


def _head_lane_mask(n_rows, width, head):
    lane = lax.broadcasted_iota(jnp.int32, (n_rows, width), 1)
    return (lane // HEAD_DIM) == head


def _two_pass_attend(qs, n_past, add_past, diag_start, add_diag, kb_ref, vb_ref, s_ref, acc_ref, l_ref, m_ref):
    blk = MOBA_BLOCK
    rep = blk // LANES
    kd = kb_ref[pl.ds(diag_start, blk), :]
    s_d = add_diag(lax.dot_general(qs, kd, NT_DIMS, preferred_element_type=F32))
    s_ref[s_ref.shape[0] - 1] = s_d
    l_ref[...] = s_d

    def pass1(n, carry):
        kb = kb_ref[pl.ds(pl.multiple_of(n * blk, blk), blk), :]
        s = add_past(n, lax.dot_general(qs, kb, NT_DIMS, preferred_element_type=F32))
        s_ref[n] = s
        l_ref[...] = jnp.maximum(l_ref[...], s)
        return carry

    lax.fori_loop(0, n_past, pass1, 0)
    m_ref[...] = jnp.broadcast_to(jnp.max(l_ref[...], axis=1, keepdims=True), m_ref.shape)

    def probs(n):
        return jnp.exp2(s_ref[n] - jnp.concatenate([m_ref[...]] * rep, axis=1))

    p_d = probs(s_ref.shape[0] - 1)
    l_ref[...] = p_d
    acc_ref[...] = jnp.dot(_bf(p_d), vb_ref[pl.ds(diag_start, blk), :], preferred_element_type=F32)

    def pass2(n, carry):
        p = probs(n)
        l_ref[...] += p
        acc_ref[...] += jnp.dot(_bf(p), vb_ref[pl.ds(pl.multiple_of(n * blk, blk), blk), :],
                                preferred_element_type=F32)
        return carry

    lax.fori_loop(0, n_past, pass2, 0)
    return acc_ref[...] / jnp.sum(l_ref[...], axis=1, keepdims=True)


def _attend_scratch(n_blocks, R):
    return [pltpu.VMEM((n_blocks + 1, R, MOBA_BLOCK), F32),
            pltpu.VMEM((R, GROUP_LANES), F32),
            pltpu.VMEM((R, MOBA_BLOCK), F32),
            pltpu.VMEM((R, LANES), F32)]


def _fox_prompt_kernel(q_ref, k_ref, v_ref, c_ref, o_ref, kb_ref, vb_ref, s_ref, acc_ref, l_ref, m_ref, *, tq, hpg):
    g = pl.program_id(1)
    qi = pl.program_id(2)

    @pl.when(qi == 0)
    def _():
        kb_ref[...] = _bf(k_ref[0])
        vb_ref[...] = _bf(v_ref[0])

    q = q_ref[0] * (HEAD_DIM ** -0.5 * LOG2E)
    qs = _bf(jnp.concatenate([jnp.where(_head_lane_mask(tq, GROUP_LANES, h), q, 0.0) for h in range(hpg)], axis=0))
    causal = lax.broadcasted_iota(jnp.int32, (tq, tq), 0) >= lax.broadcasted_iota(jnp.int32, (tq, tq), 1)

    def add_bias(n, s, mask):
        parts = []
        for h in range(hpg):
            c_row = c_ref[0, pl.ds(g * hpg + h, 1), pl.ds(pl.multiple_of(n * tq, tq), tq)] * LOG2E
            sh = s[h * tq:(h + 1) * tq, :] - c_row
            parts.append(jnp.where(causal, sh, NEG_INF) if mask else sh)
        return jnp.concatenate(parts, axis=0)

    o = _two_pass_attend(qs, qi, lambda n, s: add_bias(n, s, False), pl.multiple_of(qi * tq, tq),
                         lambda s: add_bias(qi, s, True), kb_ref, vb_ref, s_ref, acc_ref, l_ref, m_ref)
    out = jnp.zeros((tq, GROUP_LANES), F32)
    for h in range(hpg):
        out = out + jnp.where(_head_lane_mask(tq, GROUP_LANES, h), o[h * tq:(h + 1) * tq, :], 0.0)
    o_ref[...] = _bf(out)


def fox_prompt(pab, cT, B, T):
    tq = MOBA_BLOCK
    nq = T // tq
    hpg = GROUP_LANES // HEAD_DIM
    G = B_WIDTH // GROUP_LANES
    return pl.pallas_call(
        functools.partial(_fox_prompt_kernel, tq=tq, hpg=hpg),
        grid=(B, G, nq),
        in_specs=[pl.BlockSpec((1, tq, GROUP_LANES), lambda b, g, qi: (1, b * nq + qi, g)),
                  pl.BlockSpec((1, T, GROUP_LANES), lambda b, g, qi: (1, b, G + g)),
                  pl.BlockSpec((1, T, GROUP_LANES), lambda b, g, qi: (1, b, 2 * G + g)),
                  pl.BlockSpec((1, B_HEADS, T), lambda b, g, qi: (b, 0, 0))],
        out_specs=pl.BlockSpec((tq, GROUP_LANES), lambda b, g, qi: (b * nq + qi, g)),
        out_shape=jax.ShapeDtypeStruct((B * T, B_WIDTH), BF16),
        scratch_shapes=[pltpu.VMEM((T, GROUP_LANES), BF16),
                        pltpu.VMEM((T, GROUP_LANES), BF16)] + _attend_scratch(nq - 1, hpg * tq),
        compiler_params=_cparams("parallel", "parallel", "arbitrary"),
        name="fox_prompt",
    )(pab, pab, pab, cT)


def _moba_slopes(n_rows, rows_per_head, i, kv_base):
    c = lax.broadcasted_iota(jnp.int32, (n_rows, 1), 0) // rows_per_head
    head = C_REP * (kv_base + c) + i
    return jnp.exp2(-8.0 * (head + 1).astype(F32) / C_HEADS)


def _top_blocks(z, live, lanef):
    sel = jnp.zeros(z.shape, F32)
    for _ in range(MOBA_TOPK):
        m = jnp.max(z, axis=1, keepdims=True)
        idx = jnp.min(jnp.where((z == m) & live, lanef, float(LANES)), axis=1, keepdims=True)
        pick = lanef == idx
        sel = jnp.where(pick, 1.0, sel)
        z = jnp.where(pick, NEG_INF, z)
    return sel


def _moba_prompt_kernel(q0_ref, q1_ref, q2_ref, q3_ref, k_ref, v_ref, o_ref,
                        kb_ref, vb_ref, km_ref, rt_ref, s_ref, acc_ref, l_ref, m_ref, *, tq, nb, cpg):
    g = pl.program_id(1)
    qi = pl.program_id(2)
    R = cpg * tq
    blk = MOBA_BLOCK

    @pl.when(qi == 0)
    def _():
        km_ref[...] = jnp.zeros_like(km_ref)
        for n in range(nb):
            kblk = k_ref[0, n * blk:(n + 1) * blk, :]
            kb_ref[n * blk:(n + 1) * blk, :] = _bf(kblk)
            vb_ref[n * blk:(n + 1) * blk, :] = _bf(v_ref[0, n * blk:(n + 1) * blk, :])
            km_ref[n:n + 1, :] = jnp.sum(kblk, axis=0, keepdims=True) * (1.0 / blk)

    own = (qi * tq) // blk
    row_tok = lax.broadcasted_iota(jnp.int32, (R, 1), 0) % tq
    qpos = (qi * tq + row_tok).astype(F32)
    koff = lax.broadcasted_iota(jnp.int32, (R, blk), 1)
    causal = (qi * tq - own * blk + row_tok) >= koff
    lanef = lax.broadcasted_iota(jnp.int32, (R, LANES), 1).astype(F32)
    past = lanef < own
    km_hi, km_lo = _split2(km_ref[...])
    for i, q_ref in enumerate((q0_ref, q1_ref, q2_ref, q3_ref)):
        q = q_ref[0] * (HEAD_DIM ** -0.5 * LOG2E)
        qs = jnp.concatenate([jnp.where(_head_lane_mask(tq, GROUP_LANES, c), q, 0.0) for c in range(cpg)], axis=0)
        sel = _top_blocks(jnp.where(past, _dot3_nt(qs, km_hi, km_lo), NEG_INF), past, lanef)
        slope = _moba_slopes(R, tq, i, g * cpg) * LOG2E
        b0 = slope * koff.astype(F32)
        slope_rep = jnp.broadcast_to(slope, (R, LANES))
        sq_rep = slope_rep * qpos
        sel_bf = _bf(sel)
        for n in range(nb - 1):
            spread = (lax.broadcasted_iota(jnp.int32, (LANES, LANES), 0) == n).astype(BF16)
            chosen = jnp.dot(sel_bf, spread, preferred_element_type=F32) > 0.5
            rt_ref[n] = jnp.where(chosen, slope_rep * float(n * blk) - sq_rep, NEG_INF)
        own_term = b0 + slope * ((own * blk).astype(F32) - qpos)

        def add_past(n, s, b0=b0):
            return s + b0 + jnp.concatenate([rt_ref[n]] * (blk // LANES), axis=1)

        def add_own(s, own_term=own_term):
            return jnp.where(causal, s + own_term, NEG_INF)

        o = _two_pass_attend(_bf(qs), own, add_past, pl.multiple_of(own * blk, blk), add_own,
                             kb_ref, vb_ref, s_ref, acc_ref, l_ref, m_ref)
        out = jnp.zeros((tq, GROUP_LANES), F32)
        for c in range(cpg):
            out = out + jnp.where(_head_lane_mask(tq, GROUP_LANES, c), o[c * tq:(c + 1) * tq, :], 0.0)
        o_ref[i] = _bf(out)


def moba_prompt(p1, B, T):
    tq = LANES
    nq = T // tq
    nb = T // MOBA_BLOCK
    cpg = GROUP_LANES // HEAD_DIM
    G = C_KV_WIDTH // GROUP_LANES
    R = cpg * tq
    qspec = lambda i: pl.BlockSpec((1, tq, GROUP_LANES), lambda b, g, qi, i=i: (i, b * nq + qi, g))
    return pl.pallas_call(
        functools.partial(_moba_prompt_kernel, tq=tq, nb=nb, cpg=cpg),
        grid=(B, G, nq),
        in_specs=[qspec(0), qspec(1), qspec(2), qspec(3),
                  pl.BlockSpec((1, T, GROUP_LANES), lambda b, g, qi: (4, b, g)),
                  pl.BlockSpec((1, T, GROUP_LANES), lambda b, g, qi: (5, b, g))],
        out_specs=pl.BlockSpec((C_REP, tq, GROUP_LANES), lambda b, g, qi: (0, b * nq + qi, g)),
        out_shape=jax.ShapeDtypeStruct((C_REP, B * T, C_KV_WIDTH), BF16),
        scratch_shapes=[pltpu.VMEM((T, GROUP_LANES), BF16),
                        pltpu.VMEM((T, GROUP_LANES), BF16),
                        pltpu.VMEM((LANES, GROUP_LANES), F32),
                        pltpu.VMEM((nb, R, LANES), F32)] + _attend_scratch(nb - 1, R),
        compiler_params=_cparams("parallel", "parallel", "arbitrary"),
        name="moba_prompt",
    )(p1, p1, p1, p1, p1, p1)


def _flat_online_update(s_list, v_list, m_ref, l_ref, acc_ref):
    m_old = m_ref[...]
    m_new = m_old
    for s in s_list:
        m_new = jnp.maximum(m_new, jnp.max(s, axis=1, keepdims=True))
    alpha = jnp.exp(m_old - m_new)
    l_new = alpha * l_ref[...]
    acc = alpha * acc_ref[...]
    for s, v in zip(s_list, v_list):
        p = jnp.exp(s - m_new)
        l_new = l_new + jnp.sum(p, axis=1, keepdims=True)
        acc = acc + jnp.dot(_bf(p), v, preferred_element_type=F32)
    m_ref[...] = m_new
    l_ref[...] = l_new
    acc_ref[...] = acc


def _flat_cumsum(x, carry, heads):
    rows = x.shape[0]
    lane = lax.broadcasted_iota(jnp.int32, x.shape, 1)
    d = heads
    while d < LANES:
        x = x + jnp.where(lane >= d, pltpu.roll(x, d, 1), 0.0)
        d *= 2
    tot = jnp.where(lane >= LANES - heads, x, 0.0)
    d = heads
    while d < LANES:
        tot = tot + pltpu.roll(tot, LANES - d, 1)
        d *= 2
    if rows == 1:
        return x + carry, carry + tot
    ri = lax.broadcasted_iota(jnp.int32, (rows, rows), 0)
    ci = lax.broadcasted_iota(jnp.int32, (rows, rows), 1)
    ex = _dot_exact_lhs3((ri > ci).astype(BF16), tot)
    c = x + ex + carry
    new_carry = carry + ex[rows - 1:rows, :] + tot[rows - 1:rows, :]
    return c, new_carry


def _rows_to_lanes(c):
    return jnp.concatenate([c[r:r + 1, :] for r in range(c.shape[0])], axis=1)


def _fox_sample_kernel(pt_ref, q_ref, kn_ref, vn_ref, lfn_ref, *rest, T, NP, n_steps):
    k_refs = rest[0:NP]
    v_refs = rest[NP:2 * NP]
    lf_refs = rest[2 * NP:3 * NP]
    o_ref, m_ref, l_ref, acc_ref, carry_ref = rest[3 * NP:]
    s_id = pl.program_id(1)
    H = B_HEADS
    R = H * T

    @pl.when(s_id == 0)
    def _():
        m_ref[...] = jnp.full_like(m_ref, NEG_INF)
        l_ref[...] = jnp.zeros_like(l_ref)
        acc_ref[...] = jnp.zeros_like(acc_ref)
        carry_ref[...] = jnp.zeros_like(carry_ref)

    qb = _bf(q_ref[0] * (HEAD_DIM ** -0.5))
    row_head = lax.broadcasted_iota(jnp.int32, (R, LANES), 0) // T
    lane = lax.broadcasted_iota(jnp.int32, (R, LANES), 1)
    same_head = (lane % H) == row_head
    tile = k_refs[0].shape[1] // LANES
    head_mask = jnp.concatenate([jnp.where(same_head, 0.0, NEG_INF)] * tile, axis=1)

    carry = carry_ref[...]
    s_list, v_list = [], []
    for j in range(NP):
        c, carry = _flat_cumsum(lf_refs[j][0], carry, H)
        s = _dot_nt(qb, k_refs[j][0])
        s_list.append(s - _rows_to_lanes(c) + head_mask)
        v_list.append(_bf(v_refs[j][0]))
    carry_ref[...] = carry

    @pl.when(s_id < n_steps - 1)
    def _():
        _flat_online_update(s_list, v_list, m_ref, l_ref, acc_ref)

    @pl.when(s_id == n_steps - 1)
    def _():
        c_new, _ = _flat_cumsum(lfn_ref[0], carry, H)
        row_tok = lax.broadcasted_iota(jnp.int32, (R, LANES), 0) % T
        ok = same_head & ((lane // H) <= row_tok)
        s_new = jnp.where(ok, _dot_nt(qb, kn_ref[0]) - c_new, NEG_INF)
        _flat_online_update(s_list + [s_new], v_list + [_bf(vn_ref[0])], m_ref, l_ref, acc_ref)
        o_ref[0] = acc_ref[...] / l_ref[...]


def fox_sample(q_flat, kn_flat, vn_flat, lfn_flat, cache_k, cache_v, cache_lf, page_table):
    B, n_pages = page_table.shape
    R = q_flat.shape[1]
    T = R // B_HEADS
    rows = cache_k.shape[1]
    NP = 4
    n_steps = n_pages // NP
    page = lambda j: (lambda b, s, pt: (pt[b, s * NP + j], 0, 0))
    per_b = lambda b, s, pt: (b, 0, 0)
    in_specs = [pl.BlockSpec((1, R, HEAD_DIM), per_b),
                pl.BlockSpec((1, T * B_HEADS, HEAD_DIM), per_b),
                pl.BlockSpec((1, T * B_HEADS, HEAD_DIM), per_b),
                pl.BlockSpec((1, 1, LANES), per_b)]
    in_specs += [pl.BlockSpec((1, rows, HEAD_DIM), page(j)) for j in range(NP)] * 2
    in_specs += [pl.BlockSpec((1, rows // LANES, LANES), page(j)) for j in range(NP)]
    gs = pltpu.PrefetchScalarGridSpec(
        num_scalar_prefetch=1, grid=(B, n_steps), in_specs=in_specs,
        out_specs=pl.BlockSpec((1, R, HEAD_DIM), per_b),
        scratch_shapes=[pltpu.VMEM((R, 1), F32), pltpu.VMEM((R, 1), F32),
                        pltpu.VMEM((R, HEAD_DIM), F32), pltpu.VMEM((1, LANES), F32)])
    return pl.pallas_call(
        functools.partial(_fox_sample_kernel, T=T, NP=NP, n_steps=n_steps),
        grid_spec=gs,
        out_shape=jax.ShapeDtypeStruct((B, R, HEAD_DIM), F32),
        compiler_params=_cparams("parallel", "arbitrary"),
        name="fox_sample",
    )(page_table, q_flat, kn_flat, vn_flat, lfn_flat, *([cache_k] * NP), *([cache_v] * NP), *([cache_lf] * NP))


def _moba_sample_kernel(pt_ref, q_ref, kn_ref, vn_ref, *rest, T, n_pages, q_start):
    k_refs = rest[0:n_pages]
    v_refs = rest[n_pages:2 * n_pages]
    o_ref, m_ref, l_ref, acc_ref = rest[2 * n_pages:]
    Hkv = C_KV_HEADS
    R = C_HEADS * T
    rows_per_kv = C_REP * T
    page = k_refs[0].shape[1] // Hkv
    ppb = MOBA_BLOCK // page
    nbp = n_pages // ppb
    own = q_start // MOBA_BLOCK
    blk_cols = MOBA_BLOCK * Hkv
    per_row = LANES // Hkv

    q = q_ref[0] * (HEAD_DIM ** -0.5)
    qb = _bf(q)
    rowi = lax.broadcasted_iota(jnp.int32, (R, 1), 0)
    row_kv = rowi // rows_per_kv
    row_sub = (rowi // T) % C_REP
    row_tok = rowi % T
    slope = jnp.exp2(-8.0 * (C_REP * row_kv + row_sub + 1).astype(F32) / C_HEADS)
    qpos = (q_start + row_tok).astype(F32)
    lane = lax.broadcasted_iota(jnp.int32, (R, LANES), 1)
    same_kv = (lane % Hkv) == row_kv

    sums = []
    for n in range(nbp):
        acc = jnp.zeros((Hkv, HEAD_DIM), F32)
        for j in range(ppb):
            acc = acc + jnp.sum(k_refs[n * ppb + j][0].reshape(page, Hkv, HEAD_DIM), axis=0)
        sums.append(acc * (1.0 / MOBA_BLOCK))
    km = jnp.concatenate(sums + [jnp.zeros((LANES - nbp * Hkv, HEAD_DIM), F32)], axis=0)
    gate = _dot3_nt(q, *_split2(km))
    live = same_kv & (lane < nbp * Hkv) & ((lane // Hkv) < own)
    sel = _top_blocks(jnp.where(live, gate, NEG_INF), live, lane.astype(F32))

    m_ref[...] = jnp.full_like(m_ref, NEG_INF)
    l_ref[...] = jnp.zeros_like(l_ref)
    acc_ref[...] = jnp.zeros_like(acc_ref)

    tok_new = lane // Hkv
    ok = same_kv & (tok_new <= row_tok) & (tok_new < T)
    dist = (row_tok - tok_new).astype(F32)
    s_new = jnp.where(ok, _dot_nt(qb, kn_ref[0]) - slope * dist, NEG_INF)
    _flat_online_update([s_new], [_bf(vn_ref[0])], m_ref, l_ref, acc_ref)

    col_tok = jnp.concatenate([(lane // Hkv + r * per_row) for r in range(blk_cols // LANES)], axis=1).astype(F32)
    col_term = jnp.concatenate([jnp.where(same_kv, 0.0, NEG_INF)] * (blk_cols // LANES), axis=1) + slope * col_tok
    for n in range(nbp):
        chosen = jnp.max(jnp.where((lane // Hkv) == n, sel, 0.0), axis=1, keepdims=True) > 0.5
        row_term = jnp.where(chosen, slope * (n * MOBA_BLOCK - qpos), NEG_INF)
        kb = jnp.concatenate([_bf(k_refs[n * ppb + j][0]) for j in range(ppb)], axis=0)
        vb = jnp.concatenate([_bf(v_refs[n * ppb + j][0]) for j in range(ppb)], axis=0)
        s = _dot_nt(qb, kb) + col_term + row_term
        _flat_online_update([s], [vb], m_ref, l_ref, acc_ref)
    o_ref[0] = acc_ref[...] / l_ref[...]


def moba_sample(q_flat, kn_flat, vn_flat, cache_k, cache_v, page_table, q_start):
    B, n_pages = page_table.shape
    R = q_flat.shape[1]
    T = R // C_HEADS
    rows = cache_k.shape[1]
    page = lambda j: (lambda b, pt: (pt[b, j], 0, 0))
    per_b = lambda b, pt: (b, 0, 0)
    in_specs = [pl.BlockSpec((1, R, HEAD_DIM), per_b),
                pl.BlockSpec((1, LANES, HEAD_DIM), per_b),
                pl.BlockSpec((1, LANES, HEAD_DIM), per_b)]
    in_specs += [pl.BlockSpec((1, rows, HEAD_DIM), page(j)) for j in range(n_pages)] * 2
    gs = pltpu.PrefetchScalarGridSpec(
        num_scalar_prefetch=1, grid=(B,), in_specs=in_specs,
        out_specs=pl.BlockSpec((1, R, HEAD_DIM), per_b),
        scratch_shapes=[pltpu.VMEM((R, 1), F32), pltpu.VMEM((R, 1), F32), pltpu.VMEM((R, HEAD_DIM), F32)])
    return pl.pallas_call(
        functools.partial(_moba_sample_kernel, T=T, n_pages=n_pages, q_start=q_start),
        grid_spec=gs,
        out_shape=jax.ShapeDtypeStruct((B, R, HEAD_DIM), F32),
        compiler_params=_cparams("arbitrary"),
        name="moba_sample",
    )(page_table, q_flat, kn_flat, vn_flat, *([cache_k] * n_pages), *([cache_v] * n_pages))


def _row(v):
    return v.reshape(1, -1).astype(F32)


def _prep_params(norm0_mix_g, w_in0, fox_b_f, rwkv_mu, rwkv_w0, rwkv_w2, rwkv_a0, rwkv_a2, rwkv_g2, rwkv_k_k,
                 rwkv_k_a, rwkv_r_k, rwkv_ln_w, rwkv_ln_b, w_out0, norm0_ffn_g, ffn_w_gate, ffn_w_up,
                 ffn_w_down, norm1_mix_g, w_in1, w_out1, norm1_ffn_g, router_w, router_b, moe_w_gate,
                 moe_w_up, moe_w_down, norm_final_g):
    D = w_in0.shape[0]
    pad_b = A_PROJ - (3 * B_WIDTH + B_HEADS)
    w0 = jnp.concatenate([w_in0, jnp.zeros((D, pad_b), F32)], axis=1)
    wq = w_in1[:, :C_WIDTH].reshape(D, C_KV_HEADS, C_REP, HEAD_DIM).transpose(0, 2, 1, 3).reshape(D, C_WIDTH)
    w1 = jnp.concatenate([wq, w_in1[:, C_WIDTH:]], axis=1)
    wo1 = w_out1.reshape(C_KV_HEADS, C_REP, HEAD_DIM, D).transpose(1, 0, 2, 3).reshape(C_WIDTH, D)
    hd = lax.broadcasted_iota(jnp.int32, (A_WIDTH, A_WIDTH), 0) // HEAD_DIM
    hd2 = lax.broadcasted_iota(jnp.int32, (A_WIDTH, A_WIDTH), 1) // HEAD_DIM
    rw = jnp.concatenate([router_w, jnp.zeros((D, LANES - N_EXPERTS), F32)], axis=1)
    rb = jnp.concatenate([router_b, jnp.zeros((LANES - N_EXPERTS,), F32)])
    E, _, FE = moe_w_gate.shape
    return dict(
        g0=norm0_mix_g, w0=_bf(w0),
        fox_b=jnp.concatenate([fox_b_f, jnp.zeros((LANES - B_HEADS,), F32)]).reshape(1, LANES),
        rwkv=dict(mu=_row(rwkv_mu), w0=_row(rwkv_w0), w2=_split2(rwkv_w2), a0=_row(rwkv_a0), a2=_split2(rwkv_a2),
                  g2=_split2(rwkv_g2), k_k=_row(rwkv_k_k), k_a=_row(rwkv_k_a), r_k=_row(rwkv_r_k),
                  ln_w=_row(rwkv_ln_w), ln_b=_row(rwkv_ln_b), bd=(hd == hd2).astype(BF16)),
        wo0=_bf(w_out0), g0f=norm0_ffn_g,
        ffn_g=_bf(ffn_w_gate)[None], ffn_u=_bf(ffn_w_up)[None], ffn_d=_bf(ffn_w_down),
        g1=norm1_mix_g, w1=_bf(w1), wo1=_bf(wo1), g1f=norm1_ffn_g,
        router=(*_split2(rw), rb.reshape(1, LANES)),
        moe_g=_bf(moe_w_gate), moe_u=_bf(moe_w_up), moe_d=_bf(moe_w_down).reshape(E * FE, D),
        gf=norm_final_g)


def _pair_states(S):
    B = S.shape[0]
    S = S.reshape(B, A_HEADS // 2, 2, HEAD_DIM, HEAD_DIM)
    z = jnp.zeros_like(S[:, :, 0])
    top = jnp.concatenate([S[:, :, 0], z], axis=-1)
    bot = jnp.concatenate([z, S[:, :, 1]], axis=-1)
    return jnp.concatenate([top, bot], axis=-2)


def _unpair_states(S):
    B = S.shape[0]
    a = S[:, :, :HEAD_DIM, :HEAD_DIM]
    b = S[:, :, HEAD_DIM:, HEAD_DIM:]
    return jnp.stack([a, b], axis=2).reshape(B, A_HEADS, HEAD_DIM, HEAD_DIM)


def _run(P, x, S0, shift0, caches, page_table):
    B, T, D = x.shape
    M = B * T
    xt = x.reshape(M, D)

    pab = norm_matmul(xt, P["g0"], P["w0"], A_PROJ, A_PROJ // 2)
    pa = pab[0].reshape(B, T, A_PROJ)
    p_prev = jnp.concatenate([shift0[:, None, :], pa[:, :-1]], axis=1).reshape(M, A_PROJ)
    r, lw, k, v, kn, bb, gg = rwkv_prep(pab, p_prev, P["rwkv"])
    y, s_bd = rwkv_chunk(r, lw, k, v, kn, bb, _pair_states(S0), B, T)
    ya = rwkv_post(y, r, k, v, gg, P["rwkv"])
    S_new = _unpair_states(s_bd)
    shift_new = pa[:, -1]

    pb = pab[1]
    fk = pb[:, B_WIDTH:2 * B_WIDTH].reshape(B, T, B_HEADS, HEAD_DIM)
    fv = pb[:, 2 * B_WIDTH:3 * B_WIDTH].reshape(B, T, B_HEADS, HEAD_DIM)
    logf = fox_logf(pab, P["fox_b"], 3 * B_WIDTH // LANES)[:, :B_HEADS].reshape(B, T, B_HEADS)
    if caches is None:
        yb = fox_prompt(pab, cumsum_lanes(jnp.swapaxes(logf, 1, 2)), B, T)
    else:
        n_pool, page = caches["fox_k"].shape[:2]
        flat = lambda c: c.reshape(n_pool, page * B_HEADS, HEAD_DIM)
        q_flat = pb[:, :B_WIDTH].reshape(B, T, B_HEADS, HEAD_DIM).transpose(0, 2, 1, 3).reshape(B, B_HEADS * T, HEAD_DIM)
        o = fox_sample(q_flat, fk.reshape(B, T * B_HEADS, HEAD_DIM), fv.reshape(B, T * B_HEADS, HEAD_DIM),
                       logf.reshape(B, 1, T * B_HEADS), flat(caches["fox_k"]), flat(caches["fox_v"]),
                       caches["fox_logf"].reshape(n_pool, page * B_HEADS // LANES, LANES), page_table)
        yb = _bf(o.reshape(B, B_HEADS, T, HEAD_DIM).transpose(0, 2, 1, 3).reshape(M, B_WIDTH))
    yab = jnp.concatenate([ya, yb], axis=1)[None]
    x1 = matmul_res(yab, P["wo0"], xt, tk=A_WIDTH + B_WIDTH)

    h = swiglu_up(x1, P["g0f"], P["ffn_g"], P["ffn_u"])
    x2 = matmul_res(h[None], P["ffn_d"], x1, tk=_pick(h.shape[1], (1408, 1024, 512, 256, 128)))

    p1 = norm_matmul(x2, P["g1"], P["w1"], C_KV_WIDTH, C_KV_WIDTH)
    mk = p1[4].reshape(B, T, C_KV_HEADS, HEAD_DIM)
    mv = p1[5].reshape(B, T, C_KV_HEADS, HEAD_DIM)
    if caches is None:
        y1 = moba_prompt(p1, B, T)
    else:
        n_pool, page = caches["moba_k"].shape[:2]
        flat = lambda c: c.reshape(n_pool, page * C_KV_HEADS, HEAD_DIM)
        q_flat = (p1[:C_REP].reshape(C_REP, B, T, C_KV_HEADS, HEAD_DIM).transpose(1, 3, 0, 2, 4)
                  .reshape(B, C_HEADS * T, HEAD_DIM))
        new_rows = lambda a: jnp.pad(a.reshape(B, T * C_KV_HEADS, HEAD_DIM),
                                     ((0, 0), (0, LANES - T * C_KV_HEADS), (0, 0)))
        o = moba_sample(q_flat, new_rows(p1[4]), new_rows(p1[5]), flat(caches["moba_k"]),
                        flat(caches["moba_v"]), page_table, page_table.shape[1] * page)
        y1 = _bf(o.reshape(B, C_KV_HEADS, C_REP, T, HEAD_DIM).transpose(2, 0, 3, 1, 4)
                 .reshape(C_REP, M, C_KV_WIDTH))
    x3 = matmul_res(y1, P["wo1"], x2, tk=C_KV_WIDTH)

    hm = swiglu_up(x3, P["g1f"], P["moe_g"], P["moe_u"], router=P["router"])
    out = matmul_res(hm[None], P["moe_d"], x3, tk=_pick(hm.shape[1], (1408, 1024, 512, 256, 128)), final_g=P["gf"])
    return out.reshape(B, T, D), S_new, shift_new, fk, fv, logf, mk, mv


def kernel(x_prompt, x_sample, state_rwkv_S, state_rwkv_shift, cache_fox_k, cache_fox_v, cache_fox_logf,
           cache_moba_k, cache_moba_v, page_table, norm0_mix_g, w_in0, fox_b_f, rwkv_mu, rwkv_w0, rwkv_w2,
           rwkv_a0, rwkv_a2, rwkv_g2, rwkv_k_k, rwkv_k_a, rwkv_r_k, rwkv_ln_w, rwkv_ln_b, w_out0, norm0_ffn_g,
           ffn_w_gate, ffn_w_up, ffn_w_down, norm1_mix_g, w_in1, w_out1, norm1_ffn_g, router_w, router_b,
           moe_w_gate, moe_w_up, moe_w_down, norm_final_g):
    P = _prep_params(norm0_mix_g, w_in0, fox_b_f, rwkv_mu, rwkv_w0, rwkv_w2, rwkv_a0, rwkv_a2, rwkv_g2,
                     rwkv_k_k, rwkv_k_a, rwkv_r_k, rwkv_ln_w, rwkv_ln_b, w_out0, norm0_ffn_g, ffn_w_gate,
                     ffn_w_up, ffn_w_down, norm1_mix_g, w_in1, w_out1, norm1_ffn_g, router_w, router_b,
                     moe_w_gate, moe_w_up, moe_w_down, norm_final_g)
    n_prompt = x_prompt.shape[0]
    prompt = _run(P, x_prompt,
                  jnp.zeros((n_prompt, A_HEADS, HEAD_DIM, HEAD_DIM), F32),
                  jnp.zeros((n_prompt, A_PROJ), x_prompt.dtype), None, None)
    caches = dict(fox_k=cache_fox_k, fox_v=cache_fox_v, fox_logf=cache_fox_logf,
                  moba_k=cache_moba_k, moba_v=cache_moba_v)
    sample = _run(P, x_sample, state_rwkv_S, state_rwkv_shift, caches, page_table)
    return (prompt[0], sample[0], *prompt[1:], *sample[1:])
```

```python
import functools

import jax
import jax.numpy as jnp
from jax import lax
from jax.experimental import pallas as pl
from jax.experimental.pallas import tpu as pltpu

F32 = jnp.float32
BF16 = jnp.bfloat16

HEAD_DIM = 64
A_HEADS = 16
A_WIDTH = A_HEADS * HEAD_DIM
A_LORA_W = 64
A_LORA_A = 64
A_LORA_G = 128
A_PROJ = 3 * A_WIDTH + A_LORA_W + A_LORA_A + A_LORA_G
RWKV_GN_EPS = 64e-5
B_HEADS = 16
B_WIDTH = B_HEADS * HEAD_DIM
C_HEADS = 32
C_KV_HEADS = 8
C_REP = C_HEADS // C_KV_HEADS
C_WIDTH = C_HEADS * HEAD_DIM
C_KV_WIDTH = C_KV_HEADS * HEAD_DIM
MOBA_BLOCK = 256
MOBA_TOPK = 3
N_EXPERTS = 8
NORM_EPS = 1e-6

LANES = 128
GROUP_LANES = 256
VMEM_LIMIT = 56 * 1024 * 1024
NEG_INF = float("-inf")
LOG2E = 1.4426950408889634
NB_PAD = 16
NT_DIMS = (((1,), (1,)), ((), ()))


def _cparams(*sem):
    return pltpu.CompilerParams(dimension_semantics=sem, vmem_limit_bytes=VMEM_LIMIT)


def _bf(x):
    return x.astype(BF16)


def _dot(a, b):
    return jnp.dot(_bf(a), _bf(b), preferred_element_type=F32)


def _dot_nt(a, b):
    return lax.dot_general(_bf(a), _bf(b), NT_DIMS, preferred_element_type=F32)


def _dot_tn(a, b):
    return lax.dot_general(_bf(a), _bf(b), (((0,), (0,)), ((), ())), preferred_element_type=F32)


def _split2(x):
    hi = _bf(x)
    lo = _bf(x - hi.astype(F32))
    return hi, lo


def _split3(x):
    hi = _bf(x)
    r1 = x - hi.astype(F32)
    mid = _bf(r1)
    lo = _bf(r1 - mid.astype(F32))
    return hi, mid, lo


def _dot3(a, b_hi, b_lo):
    a_hi, a_lo = _split2(a)
    return (jnp.dot(a_hi, b_hi, preferred_element_type=F32)
            + (jnp.dot(a_hi, b_lo, preferred_element_type=F32)
               + jnp.dot(a_lo, b_hi, preferred_element_type=F32)))


def _dot3_nt(a, b_hi, b_lo):
    a_hi, a_lo = _split2(a)
    return (lax.dot_general(a_hi, b_hi, NT_DIMS, preferred_element_type=F32)
            + (lax.dot_general(a_hi, b_lo, NT_DIMS, preferred_element_type=F32)
               + lax.dot_general(a_lo, b_hi, NT_DIMS, preferred_element_type=F32)))


def _dot_exact_rhs(a, b_exact):
    a_hi, a_lo = _split2(a)
    return (jnp.dot(a_hi, b_exact, preferred_element_type=F32)
            + jnp.dot(a_lo, b_exact, preferred_element_type=F32))


def _dot_exact_lhs3(a_exact, x):
    hi, mid, lo = _split3(x)
    return (jnp.dot(a_exact, hi, preferred_element_type=F32)
            + (jnp.dot(a_exact, mid, preferred_element_type=F32)
               + jnp.dot(a_exact, lo, preferred_element_type=F32)))


def _sigmoid(x):
    return 1.0 / (1.0 + jnp.exp(-x))


def _pick(n, prefs):
    for p in prefs:
        if n % p == 0:
            return p
    return n


def _norm_mm_kernel(x_ref, g_ref, w_ref, o_ref, xn_ref):
    @pl.when(pl.program_id(1) == 0)
    def _():
        x = x_ref[...]
        ms = jnp.mean(x * x, axis=-1, keepdims=True)
        xn_ref[...] = _bf(x * lax.rsqrt(ms + NORM_EPS) * g_ref[...])

    o_ref[0] = jnp.dot(xn_ref[...], w_ref[...], preferred_element_type=F32)


def norm_matmul(x, g, w, cw, tn):
    M, K = x.shape
    N = w.shape[1]
    nc = N // cw
    tm = _pick(M, (1024, 512, 256, 128, 8))
    per = cw // tn
    return pl.pallas_call(
        _norm_mm_kernel,
        grid=(M // tm, N // tn),
        in_specs=[pl.BlockSpec((tm, K), lambda i, j: (i, 0)),
                  pl.BlockSpec((1, K), lambda i, j: (0, 0)),
                  pl.BlockSpec((K, tn), lambda i, j: (0, j))],
        out_specs=pl.BlockSpec((1, tm, tn), lambda i, j: (j // per, i, j % per)),
        out_shape=jax.ShapeDtypeStruct((nc, M, cw), F32),
        scratch_shapes=[pltpu.VMEM((tm, K), BF16)],
        compiler_params=_cparams("parallel", "arbitrary"),
        name="norm_matmul",
    )(x, g.reshape(1, K), w)


def _mm_res_kernel(a_ref, w_ref, r_ref, *rest, nk, final_norm):
    if final_norm:
        g_ref, o_ref, acc_ref = rest
    else:
        o_ref, acc_ref = rest
    k = pl.program_id(2)

    @pl.when(k == 0)
    def _():
        acc_ref[...] = r_ref[...]

    acc_ref[...] += jnp.dot(a_ref[0], w_ref[...], preferred_element_type=F32)

    @pl.when(k == nk - 1)
    def _():
        y = acc_ref[...]
        if final_norm:
            ms = jnp.mean(y * y, axis=-1, keepdims=True)
            y = y * lax.rsqrt(ms + NORM_EPS) * g_ref[...]
        o_ref[...] = y


def matmul_res(a, w, res, tk, final_g=None):
    ka, M, kw = a.shape
    N = w.shape[1]
    tm = _pick(M, (512, 256, 128, 8))
    tn = N if final_g is not None else _pick(N, (1024, 512, 256, 128))
    per = kw // tk
    nk = ka * per
    in_specs = [pl.BlockSpec((1, tm, tk), lambda i, j, k: (k // per, i, k % per)),
                pl.BlockSpec((tk, tn), lambda i, j, k: (k, j)),
                pl.BlockSpec((tm, tn), lambda i, j, k: (i, j))]
    args = [a, w, res]
    if final_g is not None:
        in_specs.append(pl.BlockSpec((1, tn), lambda i, j, k: (0, 0)))
        args.append(final_g.reshape(1, N))
    return pl.pallas_call(
        functools.partial(_mm_res_kernel, nk=nk, final_norm=final_g is not None),
        grid=(M // tm, N // tn, nk),
        in_specs=in_specs,
        out_specs=pl.BlockSpec((tm, tn), lambda i, j, k: (i, j)),
        out_shape=jax.ShapeDtypeStruct((M, N), F32),
        scratch_shapes=[pltpu.VMEM((tm, tn), F32)],
        compiler_params=_cparams("parallel", "parallel", "arbitrary"),
        name="matmul_res",
    )(*args)


def _swiglu_up_kernel(x_ref, g_ref, wg_ref, wu_ref, *rest, routed):
    if routed:
        rwh_ref, rwl_ref, rb_ref, h_ref, xn_ref, gate_ref = rest
    else:
        h_ref, xn_ref = rest
    e = pl.program_id(1)

    @pl.when((e == 0) & (pl.program_id(2) == 0))
    def _():
        x = x_ref[...]
        ms = jnp.mean(x * x, axis=-1, keepdims=True)
        xn = x * lax.rsqrt(ms + NORM_EPS) * g_ref[...]
        xn_ref[...] = _bf(xn)
        if routed:
            logits = _dot3(xn, rwh_ref[...], rwl_ref[...]) + rb_ref[...]
            lane = lax.broadcasted_iota(jnp.int32, logits.shape, 1).astype(F32)
            live = lane < N_EXPERTS
            z = jnp.where(live, logits, NEG_INF)
            m1 = jnp.max(z, axis=1, keepdims=True)
            i1 = jnp.min(jnp.where(z == m1, lane, float(LANES)), axis=1, keepdims=True)
            z2 = jnp.where(lane == i1, NEG_INF, z)
            m2 = jnp.max(z2, axis=1, keepdims=True)
            i2 = jnp.min(jnp.where(z2 == m2, lane, float(LANES)), axis=1, keepdims=True)
            e2 = jnp.exp(m2 - m1)
            den = 1.0 + e2
            gate_ref[...] = jnp.where(lane == i1, 1.0 / den, jnp.where(lane == i2, e2 / den, 0.0))

    xn = xn_ref[...]
    a = jnp.dot(xn, wg_ref[0], preferred_element_type=F32)
    b = jnp.dot(xn, wu_ref[0], preferred_element_type=F32)
    h = a * _sigmoid(a) * b
    if routed:
        gates = gate_ref[...]
        lane = lax.broadcasted_iota(jnp.int32, gates.shape, 1)
        h = h * jnp.sum(jnp.where(lane == e, gates, 0.0), axis=1, keepdims=True)
    h_ref[...] = _bf(h)


def swiglu_up(x, g, wg, wu, router=None):
    M, K = x.shape
    E, _, F = wg.shape
    tm = _pick(M, (1024, 512, 256, 128, 8))
    tn = _pick(F, (512, 256, 128))
    per = F // tn
    routed = router is not None
    in_specs = [pl.BlockSpec((tm, K), lambda i, e, j: (i, 0)),
                pl.BlockSpec((1, K), lambda i, e, j: (0, 0)),
                pl.BlockSpec((1, K, tn), lambda i, e, j: (e, 0, j)),
                pl.BlockSpec((1, K, tn), lambda i, e, j: (e, 0, j))]
    args = [x, g.reshape(1, K), wg, wu]
    scratch = [pltpu.VMEM((tm, K), BF16)]
    if routed:
        in_specs += [pl.BlockSpec((K, LANES), lambda i, e, j: (0, 0)),
                     pl.BlockSpec((K, LANES), lambda i, e, j: (0, 0)),
                     pl.BlockSpec((1, LANES), lambda i, e, j: (0, 0))]
        args += list(router)
        scratch.append(pltpu.VMEM((tm, LANES), F32))
    return pl.pallas_call(
        functools.partial(_swiglu_up_kernel, routed=routed),
        grid=(M // tm, E, per),
        in_specs=in_specs,
        out_specs=pl.BlockSpec((tm, tn), lambda i, e, j: (i, e * per + j)),
        out_shape=jax.ShapeDtypeStruct((M, E * F), BF16),
        scratch_shapes=scratch,
        compiler_params=_cparams("parallel", "arbitrary", "arbitrary"),
        name="moe_up" if routed else "ffn_up",
    )(*args)


def _rwkv_prep_kernel(p_ref, pp_ref, mu_ref, w0_ref, w2h_ref, w2l_ref, a0_ref, a2h_ref, a2l_ref,
                      g2h_ref, g2l_ref, kk_ref, ka_ref, bd_ref,
                      r_ref, lw_ref, k_ref, v_ref, kn_ref, b_ref, g_ref):
    p = p_ref[0]
    ps = p + (pp_ref[...] - p) * mu_ref[...]
    W = A_WIDTH
    r = ps[:, 0:W]
    k = ps[:, W:2 * W]
    v = ps[:, 2 * W:3 * W]
    o = 3 * W
    wd = ps[:, o:o + A_LORA_W]
    ad = ps[:, o + A_LORA_W:o + A_LORA_W + A_LORA_A]
    gd = ps[:, o + A_LORA_W + A_LORA_A:]
    z = -(w0_ref[...] + _dot3(jnp.tanh(wd), w2h_ref[...], w2l_ref[...]))
    softplus = jnp.maximum(z, 0.0) + jnp.log(1.0 + jnp.exp(-jnp.abs(z)))
    lw_ref[...] = -jnp.exp(-softplus - 0.5)
    a = _sigmoid(a0_ref[...] + _dot3(ad, a2h_ref[...], a2l_ref[...]))
    g_ref[...] = _dot3(_sigmoid(gd), g2h_ref[...], g2l_ref[...])
    kk = k * kk_ref[...]
    ss = _dot_exact_rhs(kk * kk, bd_ref[...])
    kn = kk / jnp.maximum(jnp.sqrt(ss), 1e-12)
    r_ref[...] = r
    v_ref[...] = v
    kn_ref[...] = kn
    b_ref[...] = kn * a
    k_ref[...] = k * (1.0 + (a - 1.0) * ka_ref[...])


def rwkv_prep(pab, p_prev, prm):
    M = p_prev.shape[0]
    tm = _pick(M, (256, 128, 8))
    W = A_WIDTH
    row = lambda n: pl.BlockSpec((1, n), lambda i: (0, 0))
    mat = lambda a, b: pl.BlockSpec((a, b), lambda i: (0, 0))
    out = pl.BlockSpec((tm, W), lambda i: (i, 0))
    return pl.pallas_call(
        _rwkv_prep_kernel,
        grid=(M // tm,),
        in_specs=[pl.BlockSpec((1, tm, A_PROJ), lambda i: (0, i, 0)),
                  pl.BlockSpec((tm, A_PROJ), lambda i: (i, 0)),
                  row(A_PROJ), row(W), mat(A_LORA_W, W), mat(A_LORA_W, W),
                  row(W), mat(A_LORA_A, W), mat(A_LORA_A, W),
                  mat(A_LORA_G, W), mat(A_LORA_G, W), row(W), row(W), mat(W, W)],
        out_specs=[out] * 7,
        out_shape=[jax.ShapeDtypeStruct((M, W), F32)] * 7,
        compiler_params=_cparams("parallel"),
        name="rwkv_prep",
    )(pab, p_prev, prm["mu"], prm["w0"], *prm["w2"], prm["a0"], *prm["a2"], *prm["g2"],
      prm["k_k"], prm["k_a"], prm["bd"])


def _rwkv_chunk_kernel(r_ref, lw_ref, k_ref, v_ref, kn_ref, b_ref, s0_ref, y_ref, s_ref, *, C, nc, Bb):
    C2 = 2 * C
    lane = lax.broadcasted_iota(jnp.int32, (C2, LANES), 1)
    rowi = lax.broadcasted_iota(jnp.int32, (C2, LANES), 0)
    mask2 = ((rowi >= C) == (lane >= HEAD_DIM)).astype(F32)
    ri = lax.broadcasted_iota(jnp.int32, (C2, C2), 0)
    ci = lax.broadcasted_iota(jnp.int32, (C2, C2), 1)
    same = (ri >= C) == (ci >= C)
    strict = same & (ri > ci)
    incl = same & (ri >= ci)
    eye = (ri == ci).astype(F32)
    ti = lax.broadcasted_iota(jnp.int32, (C, C), 0)
    tj = lax.broadcasted_iota(jnp.int32, (C, C), 1)
    tri = (ti >= tj).astype(BF16)
    n_levels = max(C.bit_length() - 2, 0)
    streams = range(Bb)

    def stack(x):
        return jnp.concatenate([x, x], axis=0) * mask2

    def each(f, *lists):
        return [f(*[l[i] for l in lists]) for i in streams]

    @pl.when(pl.program_id(2) == 0)
    def _():
        s_ref[...] = s0_ref[...]

    def chunk(c, states):
        sl = pl.ds(pl.multiple_of(c * C, C), C)
        S = list(states)
        load = lambda ref: [ref[i, sl, :] for i in streams]
        r, lw, k, v, kn, b = (load(ref) for ref in (r_ref, lw_ref, k_ref, v_ref, kn_ref, b_ref))
        cs = each(lambda x: _dot_exact_lhs3(tri, x), lw)
        gt = each(lambda x: x[C - 1:C, :], cs)
        e_neg = each(lambda x: jnp.exp(-x), cs)
        e_rem = each(lambda g, x: jnp.exp(g - x), gt, cs)
        KT = each(lambda a, x, l: stack(a * jnp.exp(x - l)), kn, cs, lw)
        BI = each(lambda a, e: stack(a * e), b, e_neg)
        KI = each(lambda a, e: stack(a * e), k, e_neg)
        RT = each(lambda a, x: stack(a * jnp.exp(x)), r, cs)
        V2 = each(stack, v)
        KG = each(lambda a, e: stack(a * e), k, e_rem)
        BG = each(lambda a, e: stack(a * e), b, e_rem)
        a_kb = each(_dot_nt, KT, BI)
        a_kv = each(_dot_nt, KT, KI)
        a_rb = each(_dot_nt, RT, BI)
        a_rk = each(_dot_nt, RT, KI)
        a_rb = each(lambda x: jnp.where(incl, x, 0.0), a_rb)
        a_rk = each(lambda x: jnp.where(incl, x, 0.0), a_rk)
        pw = each(lambda x: -jnp.where(strict, x, 0.0), a_kb)
        tinv = each(lambda x: eye + x, pw)
        for _ in range(n_levels):
            pw = each(_dot, pw, pw)
            tinv = each(lambda t, p: t + _dot(t, p), tinv, pw)
        kp = each(_dot, tinv, KT)
        av = each(lambda x, vv: _dot(jnp.where(strict, x, 0.0), vv), a_kv, V2)
        w1 = each(_dot, tinv, av)
        rp = each(lambda x, a, p: x - _dot(a, p), RT, a_rb, kp)
        y1 = each(lambda a, vv, ab, w: _dot(a, vv) - _dot(ab, w), a_rk, V2, a_rb, w1)
        mlow = each(_dot_tn, kp, BG)
        nt = each(lambda vv, kg, w, bg: _dot_tn(vv, kg) - _dot_tn(w, bg), V2, KG, w1, BG)
        y2 = each(lambda p, s, y: _dot3_nt(p, *_split2(s)) + y, rp, S, y1)
        for i in streams:
            y_ref[i, sl, :] = y2[i][:C] + y2[i][C:]
        new = each(lambda s, g, m, n: s * jnp.exp(g) - _dot3(s, *_split2(m)) + n, S, gt, mlow, nt)
        return tuple(new)

    final = lax.fori_loop(0, nc, chunk, tuple(s_ref[i, 0] for i in streams))
    for i in streams:
        s_ref[i, 0] = final[i]


def rwkv_chunk(r, lw, k, v, kn, b, s_bd, B, T):
    C = min(T, 64)
    Tt = _pick(T, (512, 256, 128, 64)) if T > C else T
    nc = Tt // C
    Bb = _pick(B, (4, 2, 1)) if T > C else _pick(B, (8, 4, 2, 1))
    HP = A_WIDTH // LANES
    seq = pl.BlockSpec((Bb, Tt, LANES), lambda i, h, t: (i, t, h))
    st = pl.BlockSpec((Bb, 1, LANES, LANES), lambda i, h, t: (i, h, 0, 0))
    r3 = lambda x: x.reshape(B, T, A_WIDTH)
    y, s = pl.pallas_call(
        functools.partial(_rwkv_chunk_kernel, C=C, nc=nc, Bb=Bb),
        grid=(B // Bb, HP, T // Tt),
        in_specs=[seq] * 6 + [st],
        out_specs=[seq, st],
        out_shape=[jax.ShapeDtypeStruct((B, T, A_WIDTH), F32),
                   jax.ShapeDtypeStruct((B, HP, LANES, LANES), F32)],
        compiler_params=_cparams("parallel", "parallel", "arbitrary"),
        name="rwkv_chunk",
    )(r3(r), r3(lw), r3(k), r3(v), r3(kn), r3(b), s_bd)
    return y.reshape(B * T, A_WIDTH), s


def _rwkv_post_kernel(y_ref, r_ref, k_ref, v_ref, g_ref, rk_ref, lnw_ref, lnb_ref, bd_ref, o_ref):
    y = y_ref[...]
    bd = bd_ref[...]
    inv_n = 1.0 / HEAD_DIM
    mean = _dot_exact_rhs(y, bd) * inv_n
    d = y - mean
    var = _dot_exact_rhs(d * d, bd) * inv_n
    yn = d * lax.rsqrt(var + RWKV_GN_EPS) * lnw_ref[...] + lnb_ref[...]
    bonus = _dot_exact_rhs(r_ref[...] * k_ref[...] * rk_ref[...], bd)
    o_ref[...] = _bf((yn + bonus * v_ref[...]) * g_ref[...])


def rwkv_post(y, r, k, v, g, prm):
    M, W = y.shape
    tm = _pick(M, (256, 128, 8))
    blk = pl.BlockSpec((tm, W), lambda i: (i, 0))
    row = pl.BlockSpec((1, W), lambda i: (0, 0))
    return pl.pallas_call(
        _rwkv_post_kernel,
        grid=(M // tm,),
        in_specs=[blk] * 5 + [row] * 3 + [pl.BlockSpec((W, W), lambda i: (0, 0))],
        out_specs=blk,
        out_shape=jax.ShapeDtypeStruct((M, W), BF16),
        compiler_params=_cparams("parallel"),
        name="rwkv_post",
    )(y, r, k, v, g, prm["r_k"], prm["ln_w"], prm["ln_b"], prm["bd"])


def _logf_kernel(f_ref, b_ref, o_ref):
    z = f_ref[0] + b_ref[...]
    o_ref[...] = jnp.minimum(z, 0.0) - jnp.log(1.0 + jnp.exp(-jnp.abs(z)))


def fox_logf(pab, bias_row, col_block):
    M = pab.shape[1]
    tm = _pick(M, (1024, 512, 256, 128, 8))
    return pl.pallas_call(
        _logf_kernel,
        grid=(M // tm,),
        in_specs=[pl.BlockSpec((1, tm, LANES), lambda i: (1, i, col_block)),
                  pl.BlockSpec((1, LANES), lambda i: (0, 0))],
        out_specs=pl.BlockSpec((tm, LANES), lambda i: (i, 0)),
        out_shape=jax.ShapeDtypeStruct((M, LANES), F32),
        compiler_params=_cparams("parallel"),
        name="fox_logf",
    )(pab, bias_row)


def _cumsum_kernel(x_ref, o_ref, carry_ref):
    @pl.when(pl.program_id(1) == 0)
    def _():
        carry_ref[...] = jnp.zeros_like(carry_ref)

    ti = lax.broadcasted_iota(jnp.int32, (LANES, LANES), 0)
    tj = lax.broadcasted_iota(jnp.int32, (LANES, LANES), 1)
    triu = (ti <= tj).astype(BF16)
    hi, mid, lo = _split3(x_ref[0])
    cs = (jnp.dot(hi, triu, preferred_element_type=F32)
          + (jnp.dot(mid, triu, preferred_element_type=F32)
             + jnp.dot(lo, triu, preferred_element_type=F32))) + carry_ref[...]
    o_ref[0] = cs
    carry_ref[...] = jnp.broadcast_to(cs[:, LANES - 1:LANES], cs.shape)


def cumsum_lanes(xT):
    B, H, L = xT.shape
    return pl.pallas_call(
        _cumsum_kernel,
        grid=(B, L // LANES),
        in_specs=[pl.BlockSpec((1, H, LANES), lambda b, p: (b, 0, p))],
        out_specs=pl.BlockSpec((1, H, LANES), lambda b, p: (b, 0, p)),
        out_shape=jax.ShapeDtypeStruct((B, H, L), F32),
        scratch_shapes=[pltpu.VMEM((H, LANES), F32)],
        compiler_params=_cparams("parallel", "arbitrary"),
        name="fox_cumsum",
    )(xT)


def _head_lane_mask(n_rows, width, head):
    lane = lax.broadcasted_iota(jnp.int32, (n_rows, width), 1)
    return (lane // HEAD_DIM) == head


def _two_pass_attend(qs, n_past, add_past, diag_start, add_diag, kb_ref, vb_ref, s_ref, acc_ref, l_ref, m_ref):
    blk = MOBA_BLOCK
    rep = blk // LANES
    kd = kb_ref[pl.ds(diag_start, blk), :]
    s_d = add_diag(lax.dot_general(qs, kd, NT_DIMS, preferred_element_type=F32))
    s_ref[s_ref.shape[0] - 1] = s_d
    l_ref[...] = s_d

    def pass1(n, carry):
        kb = kb_ref[pl.ds(pl.multiple_of(n * blk, blk), blk), :]
        s = add_past(n, lax.dot_general(qs, kb, NT_DIMS, preferred_element_type=F32))
        s_ref[n] = s
        l_ref[...] = jnp.maximum(l_ref[...], s)
        return carry

    lax.fori_loop(0, n_past, pass1, 0)
    m_ref[...] = jnp.broadcast_to(jnp.max(l_ref[...], axis=1, keepdims=True), m_ref.shape)

    def probs(n):
        return jnp.exp2(s_ref[n] - jnp.concatenate([m_ref[...]] * rep, axis=1))

    p_d = probs(s_ref.shape[0] - 1)
    l_ref[...] = p_d
    acc_ref[...] = jnp.dot(_bf(p_d), vb_ref[pl.ds(diag_start, blk), :], preferred_element_type=F32)

    def pass2(n, carry):
        p = probs(n)
        l_ref[...] += p
        acc_ref[...] += jnp.dot(_bf(p), vb_ref[pl.ds(pl.multiple_of(n * blk, blk), blk), :],
                                preferred_element_type=F32)
        return carry

    lax.fori_loop(0, n_past, pass2, 0)
    return acc_ref[...] / jnp.sum(l_ref[...], axis=1, keepdims=True)


def _attend_scratch(n_blocks, R):
    return [pltpu.VMEM((n_blocks + 1, R, MOBA_BLOCK), F32),
            pltpu.VMEM((R, GROUP_LANES), F32),
            pltpu.VMEM((R, MOBA_BLOCK), F32),
            pltpu.VMEM((R, LANES), F32)]


def _fox_prompt_kernel(q_ref, k_ref, v_ref, c_ref, o_ref, kb_ref, vb_ref, s_ref, acc_ref, l_ref, m_ref, *, tq, hpg):
    g = pl.program_id(1)
    qi = pl.program_id(2)

    @pl.when(qi == 0)
    def _():
        kb_ref[...] = _bf(k_ref[0])
        vb_ref[...] = _bf(v_ref[0])

    q = q_ref[0] * (HEAD_DIM ** -0.5 * LOG2E)
    qs = _bf(jnp.concatenate([jnp.where(_head_lane_mask(tq, GROUP_LANES, h), q, 0.0) for h in range(hpg)], axis=0))
    causal = lax.broadcasted_iota(jnp.int32, (tq, tq), 0) >= lax.broadcasted_iota(jnp.int32, (tq, tq), 1)

    def add_bias(n, s, mask):
        parts = []
        for h in range(hpg):
            c_row = c_ref[0, pl.ds(g * hpg + h, 1), pl.ds(pl.multiple_of(n * tq, tq), tq)] * LOG2E
            sh = s[h * tq:(h + 1) * tq, :] - c_row
            parts.append(jnp.where(causal, sh, NEG_INF) if mask else sh)
        return jnp.concatenate(parts, axis=0)

    o = _two_pass_attend(qs, qi, lambda n, s: add_bias(n, s, False), pl.multiple_of(qi * tq, tq),
                         lambda s: add_bias(qi, s, True), kb_ref, vb_ref, s_ref, acc_ref, l_ref, m_ref)
    out = jnp.zeros((tq, GROUP_LANES), F32)
    for h in range(hpg):
        out = out + jnp.where(_head_lane_mask(tq, GROUP_LANES, h), o[h * tq:(h + 1) * tq, :], 0.0)
    o_ref[...] = _bf(out)


def fox_prompt(pab, cT, B, T):
    tq = MOBA_BLOCK
    nq = T // tq
    hpg = GROUP_LANES // HEAD_DIM
    G = B_WIDTH // GROUP_LANES
    return pl.pallas_call(
        functools.partial(_fox_prompt_kernel, tq=tq, hpg=hpg),
        grid=(B, G, nq),
        in_specs=[pl.BlockSpec((1, tq, GROUP_LANES), lambda b, g, qi: (1, b * nq + qi, g)),
                  pl.BlockSpec((1, T, GROUP_LANES), lambda b, g, qi: (1, b, G + g)),
                  pl.BlockSpec((1, T, GROUP_LANES), lambda b, g, qi: (1, b, 2 * G + g)),
                  pl.BlockSpec((1, B_HEADS, T), lambda b, g, qi: (b, 0, 0))],
        out_specs=pl.BlockSpec((tq, GROUP_LANES), lambda b, g, qi: (b * nq + qi, g)),
        out_shape=jax.ShapeDtypeStruct((B * T, B_WIDTH), BF16),
        scratch_shapes=[pltpu.VMEM((T, GROUP_LANES), BF16),
                        pltpu.VMEM((T, GROUP_LANES), BF16)] + _attend_scratch(nq - 1, hpg * tq),
        compiler_params=_cparams("parallel", "parallel", "arbitrary"),
        name="fox_prompt",
    )(pab, pab, pab, cT)


def _moba_slopes(n_rows, rows_per_head, i, kv_base):
    c = lax.broadcasted_iota(jnp.int32, (n_rows, 1), 0) // rows_per_head
    head = C_REP * (kv_base + c) + i
    return jnp.exp2(-8.0 * (head + 1).astype(F32) / C_HEADS)


def _top_blocks(z, live, idxf, axis=1):
    sel = jnp.zeros(z.shape, F32)
    for _ in range(MOBA_TOPK):
        m = jnp.max(z, axis=axis, keepdims=True)
        idx = jnp.min(jnp.where((z == m) & live, idxf, float(LANES)), axis=axis, keepdims=True)
        pick = idxf == idx
        sel = jnp.where(pick, 1.0, sel)
        z = jnp.where(pick, NEG_INF, z)
    return sel


def _moba_prompt_kernel(q0_ref, q1_ref, q2_ref, q3_ref, k_ref, v_ref, o_ref,
                        kb_ref, vb_ref, km_ref, s_ref, acc_ref, l_ref, m_ref, *, tq, nb, cpg):
    g = pl.program_id(1)
    qi = pl.program_id(2)
    R = cpg * tq
    blk = MOBA_BLOCK

    @pl.when(qi == 0)
    def _():
        km_ref[...] = jnp.zeros_like(km_ref)
        for n in range(nb):
            kblk = k_ref[0, n * blk:(n + 1) * blk, :]
            kb_ref[n * blk:(n + 1) * blk, :] = _bf(kblk)
            vb_ref[n * blk:(n + 1) * blk, :] = _bf(v_ref[0, n * blk:(n + 1) * blk, :])
            km_ref[n:n + 1, :] = jnp.sum(kblk, axis=0, keepdims=True) * (1.0 / blk)

    own = (qi * tq) // blk
    row_tok = lax.broadcasted_iota(jnp.int32, (R, 1), 0) % tq
    qpos = (qi * tq + row_tok).astype(F32)
    koff = lax.broadcasted_iota(jnp.int32, (R, blk), 1)
    causal = (qi * tq - own * blk + row_tok) >= koff
    blockf = lax.broadcasted_iota(jnp.int32, (NB_PAD, R), 0).astype(F32)
    past = blockf < own
    km = km_ref[0:NB_PAD, :]
    spread_row = lax.broadcasted_iota(jnp.int32, (NB_PAD, LANES), 0)
    for i, q_ref in enumerate((q0_ref, q1_ref, q2_ref, q3_ref)):
        q = q_ref[0] * (HEAD_DIM ** -0.5 * LOG2E)
        qs = jnp.concatenate([jnp.where(_head_lane_mask(tq, GROUP_LANES, c), q, 0.0) for c in range(cpg)], axis=0)
        gate = _dot3_nt(km, *_split2(qs))
        sel_bf = _bf(_top_blocks(jnp.where(past, gate, NEG_INF), past, blockf, axis=0))
        slope = _moba_slopes(R, tq, i, g * cpg) * LOG2E
        b0 = slope * koff.astype(F32)
        slope_rep = jnp.broadcast_to(slope, (R, LANES))
        sq_rep = slope_rep * qpos
        own_term = b0 + slope * ((own * blk).astype(F32) - qpos)

        def add_past(n, s, b0=b0, sel_bf=sel_bf, slope_rep=slope_rep, sq_rep=sq_rep):
            chosen = _dot_tn(sel_bf, (spread_row == n).astype(BF16)) > 0.5
            rt = jnp.where(chosen, slope_rep * (n * blk).astype(F32) - sq_rep, NEG_INF)
            return s + b0 + jnp.concatenate([rt] * (blk // LANES), axis=1)

        def add_own(s, own_term=own_term):
            return jnp.where(causal, s + own_term, NEG_INF)

        o = _two_pass_attend(_bf(qs), own, add_past, pl.multiple_of(own * blk, blk), add_own,
                             kb_ref, vb_ref, s_ref, acc_ref, l_ref, m_ref)
        out = jnp.zeros((tq, GROUP_LANES), F32)
        for c in range(cpg):
            out = out + jnp.where(_head_lane_mask(tq, GROUP_LANES, c), o[c * tq:(c + 1) * tq, :], 0.0)
        o_ref[i] = _bf(out)


def moba_prompt(p1, B, T):
    tq = LANES
    nq = T // tq
    nb = T // MOBA_BLOCK
    assert nb <= NB_PAD
    cpg = GROUP_LANES // HEAD_DIM
    G = C_KV_WIDTH // GROUP_LANES
    R = cpg * tq
    qspec = lambda i: pl.BlockSpec((1, tq, GROUP_LANES), lambda b, g, qi, i=i: (i, b * nq + qi, g))
    return pl.pallas_call(
        functools.partial(_moba_prompt_kernel, tq=tq, nb=nb, cpg=cpg),
        grid=(B, G, nq),
        in_specs=[qspec(0), qspec(1), qspec(2), qspec(3),
                  pl.BlockSpec((1, T, GROUP_LANES), lambda b, g, qi: (4, b, g)),
                  pl.BlockSpec((1, T, GROUP_LANES), lambda b, g, qi: (5, b, g))],
        out_specs=pl.BlockSpec((C_REP, tq, GROUP_LANES), lambda b, g, qi: (0, b * nq + qi, g)),
        out_shape=jax.ShapeDtypeStruct((C_REP, B * T, C_KV_WIDTH), BF16),
        scratch_shapes=[pltpu.VMEM((T, GROUP_LANES), BF16),
                        pltpu.VMEM((T, GROUP_LANES), BF16),
                        pltpu.VMEM((LANES, GROUP_LANES), F32)] + _attend_scratch(nb - 1, R),
        compiler_params=_cparams("parallel", "parallel", "arbitrary"),
        name="moba_prompt",
    )(p1, p1, p1, p1, p1, p1)


def _flat_online_update(s_list, v_list, m_ref, l_ref, acc_ref):
    m_old = m_ref[...]
    m_new = m_old
    for s in s_list:
        m_new = jnp.maximum(m_new, jnp.max(s, axis=1, keepdims=True))
    alpha = jnp.exp(m_old - m_new)
    l_new = alpha * l_ref[...]
    acc = alpha * acc_ref[...]
    for s, v in zip(s_list, v_list):
        p = jnp.exp(s - m_new)
        l_new = l_new + jnp.sum(p, axis=1, keepdims=True)
        acc = acc + jnp.dot(_bf(p), v, preferred_element_type=F32)
    m_ref[...] = m_new
    l_ref[...] = l_new
    acc_ref[...] = acc


def _flat_cumsum(x, carry, heads):
    rows = x.shape[0]
    lane = lax.broadcasted_iota(jnp.int32, x.shape, 1)
    d = heads
    while d < LANES:
        x = x + jnp.where(lane >= d, pltpu.roll(x, d, 1), 0.0)
        d *= 2
    tot = jnp.where(lane >= LANES - heads, x, 0.0)
    d = heads
    while d < LANES:
        tot = tot + pltpu.roll(tot, LANES - d, 1)
        d *= 2
    if rows == 1:
        return x + carry, carry + tot
    ri = lax.broadcasted_iota(jnp.int32, (rows, rows), 0)
    ci = lax.broadcasted_iota(jnp.int32, (rows, rows), 1)
    ex = _dot_exact_lhs3((ri > ci).astype(BF16), tot)
    c = x + ex + carry
    new_carry = carry + ex[rows - 1:rows, :] + tot[rows - 1:rows, :]
    return c, new_carry


def _rows_to_lanes(c):
    return jnp.concatenate([c[r:r + 1, :] for r in range(c.shape[0])], axis=1)


def _fox_sample_kernel(pt_ref, q_ref, kn_ref, vn_ref, lfn_ref, *rest, T, NP, n_steps):
    k_refs = rest[0:NP]
    v_refs = rest[NP:2 * NP]
    lf_refs = rest[2 * NP:3 * NP]
    o_ref, m_ref, l_ref, acc_ref, carry_ref = rest[3 * NP:]
    s_id = pl.program_id(1)
    H = B_HEADS
    R = H * T

    @pl.when(s_id == 0)
    def _():
        m_ref[...] = jnp.full_like(m_ref, NEG_INF)
        l_ref[...] = jnp.zeros_like(l_ref)
        acc_ref[...] = jnp.zeros_like(acc_ref)
        carry_ref[...] = jnp.zeros_like(carry_ref)

    qb = _bf(q_ref[0] * (HEAD_DIM ** -0.5))
    row_head = lax.broadcasted_iota(jnp.int32, (R, LANES), 0) // T
    lane = lax.broadcasted_iota(jnp.int32, (R, LANES), 1)
    same_head = (lane % H) == row_head
    n_rows = k_refs[0].shape[1] * H
    head_mask = jnp.concatenate([jnp.where(same_head, 0.0, NEG_INF)] * (n_rows // LANES), axis=1)

    carry = carry_ref[...]
    s_list, v_list = [], []
    for j in range(NP):
        c, carry = _flat_cumsum(lf_refs[j][0], carry, H)
        s = _dot_nt(qb, k_refs[j][0].reshape(n_rows, HEAD_DIM))
        s_list.append(s - _rows_to_lanes(c) + head_mask)
        v_list.append(_bf(v_refs[j][0].reshape(n_rows, HEAD_DIM)))
    carry_ref[...] = carry

    @pl.when(s_id < n_steps - 1)
    def _():
        _flat_online_update(s_list, v_list, m_ref, l_ref, acc_ref)

    @pl.when(s_id == n_steps - 1)
    def _():
        c_new, _ = _flat_cumsum(lfn_ref[0], carry, H)
        row_tok = lax.broadcasted_iota(jnp.int32, (R, LANES), 0) % T
        ok = same_head & ((lane // H) <= row_tok)
        s_new = jnp.where(ok, _dot_nt(qb, kn_ref[0]) - c_new, NEG_INF)
        _flat_online_update(s_list + [s_new], v_list + [_bf(vn_ref[0])], m_ref, l_ref, acc_ref)
        o_ref[0] = acc_ref[...] / l_ref[...]


def fox_sample(q_flat, kn_flat, vn_flat, lfn_flat, cache_k, cache_v, cache_lf, page_table):
    B, n_pages = page_table.shape
    R = q_flat.shape[1]
    T = R // B_HEADS
    page_shape = cache_k.shape[1:]
    NP = 4
    n_steps = n_pages // NP
    page = lambda j: (lambda b, s, pt: (pt[b, s * NP + j], 0, 0))
    page4 = lambda j: (lambda b, s, pt: (pt[b, s * NP + j], 0, 0, 0))
    per_b = lambda b, s, pt: (b, 0, 0)
    in_specs = [pl.BlockSpec((1, R, HEAD_DIM), per_b),
                pl.BlockSpec((1, T * B_HEADS, HEAD_DIM), per_b),
                pl.BlockSpec((1, T * B_HEADS, HEAD_DIM), per_b),
                pl.BlockSpec((1, 1, LANES), per_b)]
    in_specs += [pl.BlockSpec((1, *page_shape), page4(j)) for j in range(NP)] * 2
    in_specs += [pl.BlockSpec((1, *cache_lf.shape[1:]), page(j)) for j in range(NP)]
    gs = pltpu.PrefetchScalarGridSpec(
        num_scalar_prefetch=1, grid=(B, n_steps), in_specs=in_specs,
        out_specs=pl.BlockSpec((1, R, HEAD_DIM), per_b),
        scratch_shapes=[pltpu.VMEM((R, 1), F32), pltpu.VMEM((R, 1), F32),
                        pltpu.VMEM((R, HEAD_DIM), F32), pltpu.VMEM((1, LANES), F32)])
    return pl.pallas_call(
        functools.partial(_fox_sample_kernel, T=T, NP=NP, n_steps=n_steps),
        grid_spec=gs,
        out_shape=jax.ShapeDtypeStruct((B, R, HEAD_DIM), F32),
        compiler_params=_cparams("parallel", "arbitrary"),
        name="fox_sample",
    )(page_table, q_flat, kn_flat, vn_flat, lfn_flat, *([cache_k] * NP), *([cache_v] * NP), *([cache_lf] * NP))


def _moba_sample_kernel(pt_ref, q_ref, kn_ref, vn_ref, *rest, T, n_pages, q_start):
    k_refs = rest[0:n_pages]
    v_refs = rest[n_pages:2 * n_pages]
    o_ref, m_ref, l_ref, acc_ref = rest[2 * n_pages:]
    Hkv = C_KV_HEADS
    R = C_HEADS * T
    rows_per_kv = C_REP * T
    page = k_refs[0].shape[1]
    page_rows = page * Hkv
    ppb = MOBA_BLOCK // page
    nbp = n_pages // ppb
    own = q_start // MOBA_BLOCK
    blk_cols = MOBA_BLOCK * Hkv
    per_row = LANES // Hkv

    q = q_ref[0] * (HEAD_DIM ** -0.5)
    qb = _bf(q)
    rowi = lax.broadcasted_iota(jnp.int32, (R, 1), 0)
    row_kv = rowi // rows_per_kv
    row_sub = (rowi // T) % C_REP
    row_tok = rowi % T
    slope = jnp.exp2(-8.0 * (C_REP * row_kv + row_sub + 1).astype(F32) / C_HEADS)
    qpos = (q_start + row_tok).astype(F32)
    lane = lax.broadcasted_iota(jnp.int32, (R, LANES), 1)
    same_kv = (lane % Hkv) == row_kv

    sums = []
    for n in range(nbp):
        acc = jnp.zeros((Hkv, HEAD_DIM), F32)
        for j in range(ppb):
            acc = acc + jnp.sum(k_refs[n * ppb + j][0], axis=0)
        sums.append(acc * (1.0 / MOBA_BLOCK))
    km = jnp.concatenate(sums + [jnp.zeros((LANES - nbp * Hkv, HEAD_DIM), F32)], axis=0)
    gate = _dot3_nt(q, *_split2(km))
    live = same_kv & (lane < nbp * Hkv) & ((lane // Hkv) < own)
    sel = _top_blocks(jnp.where(live, gate, NEG_INF), live, lane.astype(F32))

    m_ref[...] = jnp.full_like(m_ref, NEG_INF)
    l_ref[...] = jnp.zeros_like(l_ref)
    acc_ref[...] = jnp.zeros_like(acc_ref)

    tok_new = lane // Hkv
    ok = same_kv & (tok_new <= row_tok) & (tok_new < T)
    dist = (row_tok - tok_new).astype(F32)
    s_new = jnp.where(ok, _dot_nt(qb, kn_ref[0]) - slope * dist, NEG_INF)
    _flat_online_update([s_new], [_bf(vn_ref[0])], m_ref, l_ref, acc_ref)

    col_tok = jnp.concatenate([(lane // Hkv + r * per_row) for r in range(blk_cols // LANES)], axis=1).astype(F32)
    col_term = jnp.concatenate([jnp.where(same_kv, 0.0, NEG_INF)] * (blk_cols // LANES), axis=1) + slope * col_tok
    for n in range(nbp):
        chosen = jnp.max(jnp.where((lane // Hkv) == n, sel, 0.0), axis=1, keepdims=True) > 0.5
        row_term = jnp.where(chosen, slope * (n * MOBA_BLOCK - qpos), NEG_INF)
        flat = lambda ref: _bf(ref[0].reshape(page_rows, HEAD_DIM))
        kb = jnp.concatenate([flat(k_refs[n * ppb + j]) for j in range(ppb)], axis=0)
        vb = jnp.concatenate([flat(v_refs[n * ppb + j]) for j in range(ppb)], axis=0)
        s = _dot_nt(qb, kb) + col_term + row_term
        _flat_online_update([s], [vb], m_ref, l_ref, acc_ref)
    o_ref[0] = acc_ref[...] / l_ref[...]


def moba_sample(q_flat, kn_flat, vn_flat, cache_k, cache_v, page_table, q_start):
    B, n_pages = page_table.shape
    R = q_flat.shape[1]
    T = R // C_HEADS
    page = lambda j: (lambda b, pt: (pt[b, j], 0, 0, 0))
    per_b = lambda b, pt: (b, 0, 0)
    in_specs = [pl.BlockSpec((1, R, HEAD_DIM), per_b),
                pl.BlockSpec((1, LANES, HEAD_DIM), per_b),
                pl.BlockSpec((1, LANES, HEAD_DIM), per_b)]
    in_specs += [pl.BlockSpec((1, *cache_k.shape[1:]), page(j)) for j in range(n_pages)] * 2
    gs = pltpu.PrefetchScalarGridSpec(
        num_scalar_prefetch=1, grid=(B,), in_specs=in_specs,
        out_specs=pl.BlockSpec((1, R, HEAD_DIM), per_b),
        scratch_shapes=[pltpu.VMEM((R, 1), F32), pltpu.VMEM((R, 1), F32), pltpu.VMEM((R, HEAD_DIM), F32)])
    return pl.pallas_call(
        functools.partial(_moba_sample_kernel, T=T, n_pages=n_pages, q_start=q_start),
        grid_spec=gs,
        out_shape=jax.ShapeDtypeStruct((B, R, HEAD_DIM), F32),
        compiler_params=_cparams("arbitrary"),
        name="moba_sample",
    )(page_table, q_flat, kn_flat, vn_flat, *([cache_k] * n_pages), *([cache_v] * n_pages))


def _block_diag_rows(x, n_heads):
    lane = lax.broadcasted_iota(jnp.int32, x.shape, 1) // HEAD_DIM
    return jnp.concatenate([jnp.where(lane == h, x, 0.0) for h in range(n_heads)], axis=0)


def _gather_heads(out, n_heads, T):
    lane = lax.broadcasted_iota(jnp.int32, (T, out.shape[1]), 1) // HEAD_DIM
    y = jnp.zeros((T, out.shape[1]), F32)
    for h in range(n_heads):
        y = y + jnp.where(lane == h, out[h * T:(h + 1) * T, :], 0.0)
    return y


def _pad_rows(x, n):
    return jnp.concatenate([x, jnp.zeros((n - x.shape[0], x.shape[1]), x.dtype)], axis=0)


def _softmax_pv(s_list, s_new, v_refs, v_new, width):
    m = jnp.max(s_new, axis=1, keepdims=True)
    for s in s_list:
        m = jnp.maximum(m, jnp.max(s, axis=1, keepdims=True))
    p = jnp.exp(s_new - m)
    l = jnp.sum(p, axis=1, keepdims=True)
    acc = jnp.dot(_bf(p), v_new, preferred_element_type=F32)
    for s, v_ref in zip(s_list, v_refs):
        p = jnp.exp(s - m)
        l = l + jnp.sum(p, axis=1, keepdims=True)
        acc = acc + _dot_nt(p, v_ref[0].reshape(width, v_ref.shape[-1]))
    return acc / l


def _fox_decode_kernel(pt_ref, q_ref, kn_ref, vn_ref, lfn_ref, *rest, T, NP):
    k_refs = rest[0:NP]
    v_refs = rest[NP:2 * NP]
    lf_refs = rest[2 * NP:3 * NP]
    o_ref = rest[3 * NP]
    H, W = B_HEADS, B_WIDTH
    R = H * T
    page = k_refs[0].shape[-1]
    qbd = _bf(_block_diag_rows(q_ref[0] * (HEAD_DIM ** -0.5), H))
    ti = lax.broadcasted_iota(jnp.int32, (page, page), 0)
    tj = lax.broadcasted_iota(jnp.int32, (page, page), 1)
    triu = (ti <= tj).astype(BF16)

    def cumsum(x, carry):
        hi, mid, lo = _split3(x)
        cs = (jnp.dot(hi, triu, preferred_element_type=F32)
              + (jnp.dot(mid, triu, preferred_element_type=F32)
                 + jnp.dot(lo, triu, preferred_element_type=F32))) + carry
        return cs, jnp.broadcast_to(cs[:, page - 1:page], cs.shape)

    def per_row(c):
        return jnp.broadcast_to(c[:, None, :], (H, T, page)).reshape(R, page)

    carry = jnp.zeros((H, page), F32)
    s_list = []
    for j in range(NP):
        cs, carry = cumsum(lf_refs[j][0], carry)
        s = jnp.dot(qbd, _bf(k_refs[j][0].reshape(W, page)), preferred_element_type=F32)
        s_list.append(s - per_row(cs))
    cs_new, _ = cumsum(lfn_ref[0], carry)
    row_tok = lax.broadcasted_iota(jnp.int32, (R, page), 0) % T
    key = lax.broadcasted_iota(jnp.int32, (R, page), 1)
    s_new = _dot_nt(qbd, _pad_rows(kn_ref[0], page)) - per_row(cs_new)
    s_new = jnp.where(key <= row_tok, s_new, NEG_INF)
    out = _softmax_pv(s_list, s_new, v_refs, _bf(_pad_rows(vn_ref[0], page)), W)
    o_ref[0] = _bf(_gather_heads(out, H, T))


def fox_decode(pab, lfn_t, cache_k, cache_v, cache_lf, page_table, T):
    B, NP = page_table.shape
    W = B_WIDTH
    page4 = lambda j: (lambda b, pt: (pt[b, j], 0, 0, 0))
    page3 = lambda j: (lambda b, pt: (pt[b, j], 0, 0))
    in_specs = [pl.BlockSpec((1, T, W), lambda b, pt: (1, b, 0)),
                pl.BlockSpec((1, T, W), lambda b, pt: (1, b, 1)),
                pl.BlockSpec((1, T, W), lambda b, pt: (1, b, 2)),
                pl.BlockSpec((1, *lfn_t.shape[1:]), lambda b, pt: (b, 0, 0))]
    in_specs += [pl.BlockSpec((1, *cache_k.shape[1:]), page4(j)) for j in range(NP)] * 2
    in_specs += [pl.BlockSpec((1, *cache_lf.shape[1:]), page3(j)) for j in range(NP)]
    gs = pltpu.PrefetchScalarGridSpec(
        num_scalar_prefetch=1, grid=(B,), in_specs=in_specs,
        out_specs=pl.BlockSpec((1, T, W), lambda b, pt: (b, 0, 0)))
    return pl.pallas_call(
        functools.partial(_fox_decode_kernel, T=T, NP=NP),
        grid_spec=gs,
        out_shape=jax.ShapeDtypeStruct((B, T, W), BF16),
        compiler_params=_cparams("arbitrary"),
        name="fox_decode",
    )(page_table, pab, pab, pab, lfn_t, *([cache_k] * NP), *([cache_v] * NP), *([cache_lf] * NP))


def _moba_decode_kernel(pt_ref, q_ref, kn_ref, vn_ref, *rest, T, NP, q_start):
    k_refs = rest[0:NP]
    v_refs = rest[NP:2 * NP]
    o_ref = rest[2 * NP]
    Hkv, W = C_KV_HEADS, C_KV_WIDTH
    page = k_refs[0].shape[-1]
    ppb = MOBA_BLOCK // page
    nbp = NP // ppb
    own = q_start // MOBA_BLOCK
    RG = Hkv * T
    R = C_REP * RG

    qbd = jnp.concatenate([_block_diag_rows(q_ref[i] * (HEAD_DIM ** -0.5), Hkv) for i in range(C_REP)], axis=0)
    qbd_bf = _bf(qbd)
    rowi = lax.broadcasted_iota(jnp.int32, (R, 1), 0)
    row_tok = rowi % T
    head = C_REP * ((rowi // T) % Hkv) + rowi // RG
    slope = jnp.exp2(-8.0 * (head + 1).astype(F32) / C_HEADS)
    qpos = (q_start + row_tok).astype(F32)
    lane = lax.broadcasted_iota(jnp.int32, (R, page), 1)
    lanef = lane.astype(F32)

    kps = [k_refs[j][0].reshape(W, page) for j in range(NP)]
    col = lax.broadcasted_iota(jnp.int32, (W, page), 1)
    km = jnp.zeros((W, page), F32)
    for n in range(nbp):
        tot = jnp.sum(kps[n * ppb], axis=1, keepdims=True)
        for j in range(1, ppb):
            tot = tot + jnp.sum(kps[n * ppb + j], axis=1, keepdims=True)
        km = jnp.where(col == n, tot * (1.0 / MOBA_BLOCK), km)
    live = lane < min(own, nbp)
    sel = _top_blocks(jnp.where(live, _dot3(qbd, *_split2(km)), NEG_INF), live, lanef)

    b0 = slope * lanef
    s_list = []
    for n in range(nbp):
        chosen = jnp.max(jnp.where(lane == n, sel, 0.0), axis=1, keepdims=True) > 0.5
        for j in range(n * ppb, (n + 1) * ppb):
            row_term = jnp.where(chosen, slope * (j * page - qpos), NEG_INF)
            s_list.append(jnp.dot(qbd_bf, _bf(kps[j]), preferred_element_type=F32) + b0 + row_term)
    s_new = _dot_nt(qbd_bf, _pad_rows(kn_ref[0], page)) - slope * (row_tok - lane).astype(F32)
    s_new = jnp.where(lane <= row_tok, s_new, NEG_INF)
    out = _softmax_pv(s_list, s_new, v_refs, _bf(_pad_rows(vn_ref[0], page)), W)
    for i in range(C_REP):
        o_ref[i] = _bf(_gather_heads(out[i * RG:(i + 1) * RG, :], Hkv, T))


def moba_decode(p1, cache_k, cache_v, page_table, T, q_start):
    B, NP = page_table.shape
    W = C_KV_WIDTH
    page4 = lambda j: (lambda b, pt: (pt[b, j], 0, 0, 0))
    in_specs = [pl.BlockSpec((C_REP, T, W), lambda b, pt: (0, b, 0)),
                pl.BlockSpec((1, T, W), lambda b, pt: (C_REP, b, 0)),
                pl.BlockSpec((1, T, W), lambda b, pt: (C_REP + 1, b, 0))]
    in_specs += [pl.BlockSpec((1, *cache_k.shape[1:]), page4(j)) for j in range(NP)] * 2
    gs = pltpu.PrefetchScalarGridSpec(
        num_scalar_prefetch=1, grid=(B,), in_specs=in_specs,
        out_specs=pl.BlockSpec((C_REP, T, W), lambda b, pt: (0, b, 0)))
    return pl.pallas_call(
        functools.partial(_moba_decode_kernel, T=T, NP=NP, q_start=q_start),
        grid_spec=gs,
        out_shape=jax.ShapeDtypeStruct((C_REP, B * T, W), BF16),
        compiler_params=_cparams("arbitrary"),
        name="moba_decode",
    )(page_table, p1, p1, p1, *([cache_k] * NP), *([cache_v] * NP))


def _row(v):
    return v.reshape(1, -1).astype(F32)


def _prep_params(norm0_mix_g, w_in0, fox_b_f, rwkv_mu, rwkv_w0, rwkv_w2, rwkv_a0, rwkv_a2, rwkv_g2, rwkv_k_k,
                 rwkv_k_a, rwkv_r_k, rwkv_ln_w, rwkv_ln_b, w_out0, norm0_ffn_g, ffn_w_gate, ffn_w_up,
                 ffn_w_down, norm1_mix_g, w_in1, w_out1, norm1_ffn_g, router_w, router_b, moe_w_gate,
                 moe_w_up, moe_w_down, norm_final_g):
    D = w_in0.shape[0]
    pad_b = A_PROJ - (3 * B_WIDTH + B_HEADS)
    w0 = jnp.concatenate([w_in0, jnp.zeros((D, pad_b), F32)], axis=1)
    wq = w_in1[:, :C_WIDTH].reshape(D, C_KV_HEADS, C_REP, HEAD_DIM).transpose(0, 2, 1, 3).reshape(D, C_WIDTH)
    w1 = jnp.concatenate([wq, w_in1[:, C_WIDTH:]], axis=1)
    wo1 = w_out1.reshape(C_KV_HEADS, C_REP, HEAD_DIM, D).transpose(1, 0, 2, 3).reshape(C_WIDTH, D)
    hd = lax.broadcasted_iota(jnp.int32, (A_WIDTH, A_WIDTH), 0) // HEAD_DIM
    hd2 = lax.broadcasted_iota(jnp.int32, (A_WIDTH, A_WIDTH), 1) // HEAD_DIM
    rw = jnp.concatenate([router_w, jnp.zeros((D, LANES - N_EXPERTS), F32)], axis=1)
    rb = jnp.concatenate([router_b, jnp.zeros((LANES - N_EXPERTS,), F32)])
    E, _, FE = moe_w_gate.shape
    return dict(
        g0=norm0_mix_g, w0=_bf(w0),
        fox_b=jnp.concatenate([fox_b_f, jnp.zeros((LANES - B_HEADS,), F32)]).reshape(1, LANES),
        rwkv=dict(mu=_row(rwkv_mu), w0=_row(rwkv_w0), w2=_split2(rwkv_w2), a0=_row(rwkv_a0), a2=_split2(rwkv_a2),
                  g2=_split2(rwkv_g2), k_k=_row(rwkv_k_k), k_a=_row(rwkv_k_a), r_k=_row(rwkv_r_k),
                  ln_w=_row(rwkv_ln_w), ln_b=_row(rwkv_ln_b), bd=(hd == hd2).astype(BF16)),
        wo0=_bf(w_out0), g0f=norm0_ffn_g,
        ffn_g=_bf(ffn_w_gate)[None], ffn_u=_bf(ffn_w_up)[None], ffn_d=_bf(ffn_w_down),
        g1=norm1_mix_g, w1=_bf(w1), wo1=_bf(wo1), g1f=norm1_ffn_g,
        router=(*_split2(rw), rb.reshape(1, LANES)),
        moe_g=_bf(moe_w_gate), moe_u=_bf(moe_w_up), moe_d=_bf(moe_w_down).reshape(E * FE, D),
        gf=norm_final_g)


def _pair_states(S):
    B = S.shape[0]
    S = S.reshape(B, A_HEADS // 2, 2, HEAD_DIM, HEAD_DIM)
    z = jnp.zeros_like(S[:, :, 0])
    top = jnp.concatenate([S[:, :, 0], z], axis=-1)
    bot = jnp.concatenate([z, S[:, :, 1]], axis=-1)
    return jnp.concatenate([top, bot], axis=-2)


def _unpair_states(S):
    B = S.shape[0]
    a = S[:, :, :HEAD_DIM, :HEAD_DIM]
    b = S[:, :, HEAD_DIM:, HEAD_DIM:]
    return jnp.stack([a, b], axis=2).reshape(B, A_HEADS, HEAD_DIM, HEAD_DIM)


def _run(P, x, S0, shift0, caches, page_table):
    B, T, D = x.shape
    M = B * T
    xt = x.reshape(M, D)

    pab = norm_matmul(xt, P["g0"], P["w0"], A_PROJ, A_PROJ // 2)
    pa = pab[0].reshape(B, T, A_PROJ)
    p_prev = jnp.concatenate([shift0[:, None, :], pa[:, :-1]], axis=1).reshape(M, A_PROJ)
    r, lw, k, v, kn, bb, gg = rwkv_prep(pab, p_prev, P["rwkv"])
    y, s_bd = rwkv_chunk(r, lw, k, v, kn, bb, _pair_states(S0), B, T)
    ya = rwkv_post(y, r, k, v, gg, P["rwkv"])
    S_new = _unpair_states(s_bd)
    shift_new = pa[:, -1]

    pb = pab[1]
    fk = pb[:, B_WIDTH:2 * B_WIDTH].reshape(B, T, B_HEADS, HEAD_DIM)
    fv = pb[:, 2 * B_WIDTH:3 * B_WIDTH].reshape(B, T, B_HEADS, HEAD_DIM)
    logf = fox_logf(pab, P["fox_b"], 3 * B_WIDTH // LANES)[:, :B_HEADS].reshape(B, T, B_HEADS)
    if caches is None:
        yb = fox_prompt(pab, cumsum_lanes(jnp.swapaxes(logf, 1, 2)), B, T)
    else:
        page = caches["fox_k"].shape[1]
        by_head = lambda c: jnp.transpose(c, (0, 2, 3, 1))
        lfn_t = jnp.pad(jnp.swapaxes(logf, 1, 2), ((0, 0), (0, 0), (0, page - T)))
        yb = fox_decode(pab, lfn_t, by_head(caches["fox_k"]), by_head(caches["fox_v"]),
                        jnp.swapaxes(caches["fox_logf"], 1, 2), page_table, T).reshape(M, B_WIDTH)
    yab = jnp.concatenate([ya, yb], axis=1)[None]
    x1 = matmul_res(yab, P["wo0"], xt, tk=A_WIDTH + B_WIDTH)

    h = swiglu_up(x1, P["g0f"], P["ffn_g"], P["ffn_u"])
    x2 = matmul_res(h[None], P["ffn_d"], x1, tk=_pick(h.shape[1], (1408, 1024, 512, 256, 128)))

    p1 = norm_matmul(x2, P["g1"], P["w1"], C_KV_WIDTH, C_KV_WIDTH)
    mk = p1[4].reshape(B, T, C_KV_HEADS, HEAD_DIM)
    mv = p1[5].reshape(B, T, C_KV_HEADS, HEAD_DIM)
    if caches is None:
        y1 = moba_prompt(p1, B, T)
    else:
        page = caches["moba_k"].shape[1]
        by_head = lambda c: jnp.transpose(c, (0, 2, 3, 1))
        y1 = moba_decode(p1, by_head(caches["moba_k"]), by_head(caches["moba_v"]), page_table, T,
                         page_table.shape[1] * page)
    x3 = matmul_res(y1, P["wo1"], x2, tk=C_KV_WIDTH)

    hm = swiglu_up(x3, P["g1f"], P["moe_g"], P["moe_u"], router=P["router"])
    out = matmul_res(hm[None], P["moe_d"], x3, tk=_pick(hm.shape[1], (1408, 1024, 512, 256, 128)), final_g=P["gf"])
    return out.reshape(B, T, D), S_new, shift_new, fk, fv, logf, mk, mv


def kernel(x_prompt, x_sample, state_rwkv_S, state_rwkv_shift, cache_fox_k, cache_fox_v, cache_fox_logf,
           cache_moba_k, cache_moba_v, page_table, norm0_mix_g, w_in0, fox_b_f, rwkv_mu, rwkv_w0, rwkv_w2,
           rwkv_a0, rwkv_a2, rwkv_g2, rwkv_k_k, rwkv_k_a, rwkv_r_k, rwkv_ln_w, rwkv_ln_b, w_out0, norm0_ffn_g,
           ffn_w_gate, ffn_w_up, ffn_w_down, norm1_mix_g, w_in1, w_out1, norm1_ffn_g, router_w, router_b,
           moe_w_gate, moe_w_up, moe_w_down, norm_final_g):
    P = _prep_params(norm0_mix_g, w_in0, fox_b_f, rwkv_mu, rwkv_w0, rwkv_w2, rwkv_a0, rwkv_a2, rwkv_g2,
                     rwkv_k_k, rwkv_k_a, rwkv_r_k, rwkv_ln_w, rwkv_ln_b, w_out0, norm0_ffn_g, ffn_w_gate,
                     ffn_w_up, ffn_w_down, norm1_mix_g, w_in1, w_out1, norm1_ffn_g, router_w, router_b,
                     moe_w_gate, moe_w_up, moe_w_down, norm_final_g)
    n_prompt = x_prompt.shape[0]
    prompt = _run(P, x_prompt,
                  jnp.zeros((n_prompt, A_HEADS, HEAD_DIM, HEAD_DIM), F32),
                  jnp.zeros((n_prompt, A_PROJ), x_prompt.dtype), None, None)
    caches = dict(fox_k=cache_fox_k, fox_v=cache_fox_v, fox_logf=cache_fox_logf,
                  moba_k=cache_moba_k, moba_v=cache_moba_v)
    sample = _run(P, x_sample, state_rwkv_S, state_rwkv_shift, caches, page_table)
    return (prompt[0], sample[0], *prompt[1:], *sample[1:])
```

```python
import functools

import jax
import jax.numpy as jnp
from jax import lax
from jax.experimental import pallas as pl
from jax.experimental.pallas import tpu as pltpu

F32 = jnp.float32
BF16 = jnp.bfloat16

HEAD_DIM = 64
A_HEADS = 16
A_WIDTH = A_HEADS * HEAD_DIM
A_LORA_W = 64
A_LORA_A = 64
A_LORA_G = 128
A_PROJ = 3 * A_WIDTH + A_LORA_W + A_LORA_A + A_LORA_G
RWKV_GN_EPS = 64e-5
B_HEADS = 16
B_WIDTH = B_HEADS * HEAD_DIM
C_HEADS = 32
C_KV_HEADS = 8
C_REP = C_HEADS // C_KV_HEADS
C_WIDTH = C_HEADS * HEAD_DIM
C_KV_WIDTH = C_KV_HEADS * HEAD_DIM
MOBA_BLOCK = 256
MOBA_TOPK = 3
N_EXPERTS = 8
NORM_EPS = 1e-6

LANES = 128
GROUP_LANES = 256
VMEM_LIMIT = 56 * 1024 * 1024
NEG_INF = float("-inf")
LOG2E = 1.4426950408889634
NB_PAD = 16
NT_DIMS = (((1,), (1,)), ((), ()))


def _cparams(*sem):
    return pltpu.CompilerParams(dimension_semantics=sem, vmem_limit_bytes=VMEM_LIMIT)


def _bf(x):
    return x.astype(BF16)


def _dot(a, b):
    return jnp.dot(_bf(a), _bf(b), preferred_element_type=F32)


def _dot_nt(a, b):
    return lax.dot_general(_bf(a), _bf(b), NT_DIMS, preferred_element_type=F32)


def _dot_tn(a, b):
    return lax.dot_general(_bf(a), _bf(b), (((0,), (0,)), ((), ())), preferred_element_type=F32)


def _split2(x):
    hi = _bf(x)
    lo = _bf(x - hi.astype(F32))
    return hi, lo


def _split3(x):
    hi = _bf(x)
    r1 = x - hi.astype(F32)
    mid = _bf(r1)
    lo = _bf(r1 - mid.astype(F32))
    return hi, mid, lo


def _dot3(a, b_hi, b_lo):
    a_hi, a_lo = _split2(a)
    return (jnp.dot(a_hi, b_hi, preferred_element_type=F32)
            + (jnp.dot(a_hi, b_lo, preferred_element_type=F32)
               + jnp.dot(a_lo, b_hi, preferred_element_type=F32)))


def _dot3_nt(a, b_hi, b_lo):
    a_hi, a_lo = _split2(a)
    return (lax.dot_general(a_hi, b_hi, NT_DIMS, preferred_element_type=F32)
            + (lax.dot_general(a_hi, b_lo, NT_DIMS, preferred_element_type=F32)
               + lax.dot_general(a_lo, b_hi, NT_DIMS, preferred_element_type=F32)))


def _dot_exact_rhs(a, b_exact):
    a_hi, a_lo = _split2(a)
    return (jnp.dot(a_hi, b_exact, preferred_element_type=F32)
            + jnp.dot(a_lo, b_exact, preferred_element_type=F32))


def _dot_exact_lhs3(a_exact, x):
    hi, mid, lo = _split3(x)
    return (jnp.dot(a_exact, hi, preferred_element_type=F32)
            + (jnp.dot(a_exact, mid, preferred_element_type=F32)
               + jnp.dot(a_exact, lo, preferred_element_type=F32)))


def _sigmoid(x):
    return 1.0 / (1.0 + jnp.exp(-x))


def _pick(n, prefs):
    for p in prefs:
        if n % p == 0:
            return p
    return n


def _norm_mm_kernel(x_ref, g_ref, w_ref, o_ref, xn_ref):
    @pl.when(pl.program_id(1) == 0)
    def _():
        x = x_ref[...]
        ms = jnp.mean(x * x, axis=-1, keepdims=True)
        xn_ref[...] = _bf(x * lax.rsqrt(ms + NORM_EPS) * g_ref[...])

    o_ref[0] = jnp.dot(xn_ref[...], w_ref[...], preferred_element_type=F32)


def norm_matmul(x, g, w, cw, tn):
    M, K = x.shape
    N = w.shape[1]
    nc = N // cw
    tm = _pick(M, (1024, 512, 256, 128, 8))
    per = cw // tn
    return pl.pallas_call(
        _norm_mm_kernel,
        grid=(M // tm, N // tn),
        in_specs=[pl.BlockSpec((tm, K), lambda i, j: (i, 0)),
                  pl.BlockSpec((1, K), lambda i, j: (0, 0)),
                  pl.BlockSpec((K, tn), lambda i, j: (0, j))],
        out_specs=pl.BlockSpec((1, tm, tn), lambda i, j: (j // per, i, j % per)),
        out_shape=jax.ShapeDtypeStruct((nc, M, cw), F32),
        scratch_shapes=[pltpu.VMEM((tm, K), BF16)],
        compiler_params=_cparams("parallel", "arbitrary"),
        name="norm_matmul",
    )(x, g.reshape(1, K), w)


def _mm_res_kernel(a_ref, w_ref, r_ref, *rest, nk, final_norm):
    if final_norm:
        g_ref, o_ref, acc_ref = rest
    else:
        o_ref, acc_ref = rest
    k = pl.program_id(2)

    @pl.when(k == 0)
    def _():
        acc_ref[...] = r_ref[...]

    acc_ref[...] += jnp.dot(a_ref[0], w_ref[...], preferred_element_type=F32)

    @pl.when(k == nk - 1)
    def _():
        y = acc_ref[...]
        if final_norm:
            ms = jnp.mean(y * y, axis=-1, keepdims=True)
            y = y * lax.rsqrt(ms + NORM_EPS) * g_ref[...]
        o_ref[...] = y


def matmul_res(a, w, res, tk, final_g=None):
    ka, M, kw = a.shape
    N = w.shape[1]
    tm = _pick(M, (512, 256, 128, 8))
    tn = N if final_g is not None else _pick(N, (1024, 512, 256, 128))
    per = kw // tk
    nk = ka * per
    in_specs = [pl.BlockSpec((1, tm, tk), lambda i, j, k: (k // per, i, k % per)),
                pl.BlockSpec((tk, tn), lambda i, j, k: (k, j)),
                pl.BlockSpec((tm, tn), lambda i, j, k: (i, j))]
    args = [a, w, res]
    if final_g is not None:
        in_specs.append(pl.BlockSpec((1, tn), lambda i, j, k: (0, 0)))
        args.append(final_g.reshape(1, N))
    return pl.pallas_call(
        functools.partial(_mm_res_kernel, nk=nk, final_norm=final_g is not None),
        grid=(M // tm, N // tn, nk),
        in_specs=in_specs,
        out_specs=pl.BlockSpec((tm, tn), lambda i, j, k: (i, j)),
        out_shape=jax.ShapeDtypeStruct((M, N), F32),
        scratch_shapes=[pltpu.VMEM((tm, tn), F32)],
        compiler_params=_cparams("parallel", "parallel", "arbitrary"),
        name="matmul_res",
    )(*args)


def _swiglu_up_kernel(x_ref, g_ref, wg_ref, wu_ref, *rest, routed):
    if routed:
        rwh_ref, rwl_ref, rb_ref, h_ref, xn_ref, gate_ref = rest
    else:
        h_ref, xn_ref = rest
    e = pl.program_id(1)

    @pl.when((e == 0) & (pl.program_id(2) == 0))
    def _():
        x = x_ref[...]
        ms = jnp.mean(x * x, axis=-1, keepdims=True)
        xn = x * lax.rsqrt(ms + NORM_EPS) * g_ref[...]
        xn_ref[...] = _bf(xn)
        if routed:
            logits = _dot3(xn, rwh_ref[...], rwl_ref[...]) + rb_ref[...]
            lane = lax.broadcasted_iota(jnp.int32, logits.shape, 1).astype(F32)
            live = lane < N_EXPERTS
            z = jnp.where(live, logits, NEG_INF)
            m1 = jnp.max(z, axis=1, keepdims=True)
            i1 = jnp.min(jnp.where(z == m1, lane, float(LANES)), axis=1, keepdims=True)
            z2 = jnp.where(lane == i1, NEG_INF, z)
            m2 = jnp.max(z2, axis=1, keepdims=True)
            i2 = jnp.min(jnp.where(z2 == m2, lane, float(LANES)), axis=1, keepdims=True)
            e2 = jnp.exp(m2 - m1)
            den = 1.0 + e2
            gate_ref[...] = jnp.where(lane == i1, 1.0 / den, jnp.where(lane == i2, e2 / den, 0.0))

    xn = xn_ref[...]
    a = jnp.dot(xn, wg_ref[0], preferred_element_type=F32)
    b = jnp.dot(xn, wu_ref[0], preferred_element_type=F32)
    h = a * _sigmoid(a) * b
    if routed:
        gates = gate_ref[...]
        lane = lax.broadcasted_iota(jnp.int32, gates.shape, 1)
        h = h * jnp.sum(jnp.where(lane == e, gates, 0.0), axis=1, keepdims=True)
    h_ref[...] = _bf(h)


def swiglu_up(x, g, wg, wu, router=None):
    M, K = x.shape
    E, _, F = wg.shape
    tm = _pick(M, (1024, 512, 256, 128, 8))
    tn = _pick(F, (512, 256, 128))
    per = F // tn
    routed = router is not None
    in_specs = [pl.BlockSpec((tm, K), lambda i, e, j: (i, 0)),
                pl.BlockSpec((1, K), lambda i, e, j: (0, 0)),
                pl.BlockSpec((1, K, tn), lambda i, e, j: (e, 0, j)),
                pl.BlockSpec((1, K, tn), lambda i, e, j: (e, 0, j))]
    args = [x, g.reshape(1, K), wg, wu]
    scratch = [pltpu.VMEM((tm, K), BF16)]
    if routed:
        in_specs += [pl.BlockSpec((K, LANES), lambda i, e, j: (0, 0)),
                     pl.BlockSpec((K, LANES), lambda i, e, j: (0, 0)),
                     pl.BlockSpec((1, LANES), lambda i, e, j: (0, 0))]
        args += list(router)
        scratch.append(pltpu.VMEM((tm, LANES), F32))
    return pl.pallas_call(
        functools.partial(_swiglu_up_kernel, routed=routed),
        grid=(M // tm, E, per),
        in_specs=in_specs,
        out_specs=pl.BlockSpec((tm, tn), lambda i, e, j: (i, e * per + j)),
        out_shape=jax.ShapeDtypeStruct((M, E * F), BF16),
        scratch_shapes=scratch,
        compiler_params=_cparams("parallel", "arbitrary", "arbitrary"),
        name="moe_up" if routed else "ffn_up",
    )(*args)


MOE_RUN = 128
E_PAD = 16
TOP_K = 2


def _moe_cap(TM):
    return TOP_K * TM + N_EXPERTS * MOE_RUN


def _run_sizes(TM):
    top = -(-TM // MOE_RUN) * MOE_RUN
    sizes, s = [], MOE_RUN
    while s <= top:
        sizes.append(s)
        s *= 2
    return tuple(reversed(sizes))


def _moe_route_kernel(x_ref, g_ref, rwh_ref, rwl_ref, rb_ref, xn_ref, rows_ref, cols_ref, meta_ref):
    x = x_ref[...]
    TM = x.shape[0]
    ms = jnp.mean(x * x, axis=-1, keepdims=True)
    xn = x * lax.rsqrt(ms + NORM_EPS) * g_ref[...]
    xn_ref[...] = _bf(xn)
    hi, lo = _split2(xn)
    rwh = rwh_ref[...]
    logits = (lax.dot_general(rwh, hi, NT_DIMS, preferred_element_type=F32)
              + (lax.dot_general(rwh, lo, NT_DIMS, preferred_element_type=F32)
                 + lax.dot_general(rwl_ref[...], hi, NT_DIMS, preferred_element_type=F32))) + rb_ref[...]
    ef = lax.broadcasted_iota(jnp.int32, (E_PAD, TM), 0).astype(F32)
    z = jnp.where(ef < N_EXPERTS, logits, NEG_INF)
    m1 = jnp.max(z, axis=0, keepdims=True)
    i1 = jnp.min(jnp.where(z == m1, ef, float(E_PAD)), axis=0, keepdims=True)
    z2 = jnp.where(ef == i1, NEG_INF, z)
    m2 = jnp.max(z2, axis=0, keepdims=True)
    i2 = jnp.min(jnp.where(z2 == m2, ef, float(E_PAD)), axis=0, keepdims=True)
    e2 = jnp.exp(m2 - m1)
    den = 1.0 + e2
    member = jnp.where((ef == i1) | (ef == i2), 1.0, 0.0)
    before = (lax.broadcasted_iota(jnp.int32, (TM, TM), 0) < lax.broadcasted_iota(jnp.int32, (TM, TM), 1))
    rank = jnp.dot(_bf(member), before.astype(BF16), preferred_element_type=F32)
    cnt = jnp.sum(member, axis=1, keepdims=True)
    run = jnp.floor((cnt + (MOE_RUN - 1)) * (1.0 / MOE_RUN)) * MOE_RUN
    run_b = jnp.broadcast_to(run, (E_PAD, LANES))
    lower = (lax.broadcasted_iota(jnp.int32, (E_PAD, E_PAD), 0)
             > lax.broadcasted_iota(jnp.int32, (E_PAD, E_PAD), 1)).astype(BF16)
    off = jnp.dot(lower, _bf(run_b), preferred_element_type=F32)
    slot = off[:, 0:1] + rank
    slot1 = jnp.sum(jnp.where(ef == i1, slot, 0.0), axis=0, keepdims=True)
    slot2 = jnp.sum(jnp.where(ef == i2, slot, 0.0), axis=0, keepdims=True)
    info = jnp.concatenate([slot1, slot2, 1.0 / den, e2 / den, jnp.zeros((E_PAD - 4, TM), F32)], axis=0)
    rows_ref[0] = info
    ident = (lax.broadcasted_iota(jnp.int32, (E_PAD, LANES), 0)
             == lax.broadcasted_iota(jnp.int32, (E_PAD, LANES), 1)).astype(BF16)
    tn_dims = (((0,), (0,)), ((), ()))
    h3, m3, l3 = _split3(info)
    cols_ref[0] = (lax.dot_general(h3, ident, tn_dims, preferred_element_type=F32)
                   + (lax.dot_general(m3, ident, tn_dims, preferred_element_type=F32)
                      + lax.dot_general(l3, ident, tn_dims, preferred_element_type=F32)))
    lane = lax.broadcasted_iota(jnp.int32, (E_PAD, LANES), 1)
    meta_ref[0] = jnp.where(lane == 0, off, jnp.where(lane == 1, run_b, 0.0))


def moe_route(x, g, router):
    M, D = x.shape
    TM = _pick(M, (1024, 512, 256, 128, 64))
    NT = M // TM
    rwh, rwl, rb = router
    full = lambda a: pl.BlockSpec(a.shape, lambda i: (0,) * a.ndim)
    return pl.pallas_call(
        _moe_route_kernel,
        grid=(NT,),
        in_specs=[pl.BlockSpec((TM, D), lambda i: (i, 0)), pl.BlockSpec((1, D), lambda i: (0, 0)),
                  full(rwh), full(rwl), full(rb)],
        out_specs=[pl.BlockSpec((TM, D), lambda i: (i, 0)),
                   pl.BlockSpec((1, E_PAD, TM), lambda i: (i, 0, 0)),
                   pl.BlockSpec((1, TM, LANES), lambda i: (i, 0, 0)),
                   pl.BlockSpec((1, E_PAD, LANES), lambda i: (i, 0, 0))],
        out_shape=[jax.ShapeDtypeStruct((M, D), BF16),
                   jax.ShapeDtypeStruct((NT, E_PAD, TM), F32),
                   jax.ShapeDtypeStruct((NT, TM, LANES), F32),
                   jax.ShapeDtypeStruct((NT, E_PAD, LANES), F32)],
        compiler_params=_cparams("parallel"),
        name="moe_route",
    )(x, g.reshape(1, D), rwh, rwl, rb)


def _for_each_run_chunk(off_ref, run_ref, sizes, body):
    i = pl.program_id(0)
    e = pl.program_id(2)
    base = off_ref[i, e]
    n = run_ref[i, e]
    for size in sizes:
        @pl.when((n & size) != 0)
        def _(size=size):
            body(pl.multiple_of(base + (n & (-2 * size)), MOE_RUN), size)


def _moe_up_kernel(off_ref, run_ref, xn_ref, rows_ref, wg_ref, wu_ref, hs_ref, xs_ref, *, sizes, GC):
    j = pl.program_id(1)
    e = pl.program_id(2)
    TM = xn_ref.shape[0]
    CAP = xs_ref.shape[0]

    @pl.when((j == 0) & (e == 0))
    def _():
        s1 = rows_ref[0, 0:1, :]
        s2 = rows_ref[0, 1:2, :]
        xnb = xn_ref[...]
        for c in range(CAP // GC):
            srow = (c * GC + lax.broadcasted_iota(jnp.int32, (GC, TM), 0)).astype(F32)
            onehot = jnp.where((srow == s1) | (srow == s2), 1.0, 0.0)
            xs_ref[c * GC:(c + 1) * GC, :] = _bf(jnp.dot(_bf(onehot), xnb, preferred_element_type=F32))

    @pl.when(e == 0)
    def _():
        hs_ref[...] = jnp.zeros_like(hs_ref)

    def body(start, size):
        xr = xs_ref[pl.ds(start, size), :]
        a = jnp.dot(xr, wg_ref[0], preferred_element_type=F32)
        b = jnp.dot(xr, wu_ref[0], preferred_element_type=F32)
        hs_ref[0, pl.ds(start, size), :] = _bf(a * _sigmoid(a) * b)

    _for_each_run_chunk(off_ref, run_ref, sizes, body)


def moe_up_sparse(xn, rows, off, run, wg, wu):
    M, D = xn.shape
    NT, _, TM = rows.shape
    E, _, F = wg.shape
    CAP = _moe_cap(TM)
    tn = _pick(F, (256, 128))
    gs = pltpu.PrefetchScalarGridSpec(
        num_scalar_prefetch=2, grid=(NT, F // tn, E),
        in_specs=[pl.BlockSpec((TM, D), lambda i, j, e, o, r: (i, 0)),
                  pl.BlockSpec((1, E_PAD, TM), lambda i, j, e, o, r: (i, 0, 0)),
                  pl.BlockSpec((1, D, tn), lambda i, j, e, o, r: (e, 0, j)),
                  pl.BlockSpec((1, D, tn), lambda i, j, e, o, r: (e, 0, j))],
        out_specs=pl.BlockSpec((1, CAP, tn), lambda i, j, e, o, r: (i, 0, j)),
        scratch_shapes=[pltpu.VMEM((CAP, D), BF16)])
    return pl.pallas_call(
        functools.partial(_moe_up_kernel, sizes=_run_sizes(TM), GC=_pick(CAP, (512, 256, 128))),
        grid_spec=gs,
        out_shape=jax.ShapeDtypeStruct((NT, CAP, F), BF16),
        compiler_params=_cparams("arbitrary", "arbitrary", "arbitrary"),
        name="moe_up",
    )(off, run, xn, rows, wg, wu)


def _moe_down_kernel(off_ref, run_ref, hs_ref, wd_ref, x_ref, cols_ref, rows_ref, o_ref, ys_ref, *,
                     sizes, GC, n_e, n_kf):
    e = pl.program_id(2)
    kf = pl.program_id(3)
    TM = x_ref.shape[0]
    CAP = ys_ref.shape[0]

    @pl.when((e == 0) & (kf == 0))
    def _():
        ys_ref[...] = jnp.zeros_like(ys_ref)

    def body(start, size):
        ys_ref[pl.ds(start, size), :] += jnp.dot(hs_ref[0, pl.ds(start, size), :], wd_ref[0],
                                                 preferred_element_type=F32)

    _for_each_run_chunk(off_ref, run_ref, sizes, body)

    @pl.when((e == n_e - 1) & (kf == n_kf - 1))
    def _():
        info = cols_ref[0]
        s1, s2 = info[:, 0:1], info[:, 1:2]
        r1, r2, p1, p2 = (rows_ref[0, k:k + 1, :] for k in range(4))
        acc = x_ref[...]
        for c in range(CAP // GC):
            srow = (c * GC + lax.broadcasted_iota(jnp.int32, (GC, TM), 0)).astype(F32)
            gate = jnp.sum(jnp.where(srow == r1, p1, 0.0) + jnp.where(srow == r2, p2, 0.0), axis=1, keepdims=True)
            scol = (c * GC + lax.broadcasted_iota(jnp.int32, (TM, GC), 1)).astype(F32)
            onehot = jnp.where((scol == s1) | (scol == s2), 1.0, 0.0)
            acc = acc + jnp.dot(_bf(onehot), _bf(ys_ref[c * GC:(c + 1) * GC, :] * gate), preferred_element_type=F32)
        o_ref[...] = acc


def moe_down_sparse(hs, cols, rows, off, run, wd, x):
    NT, CAP, F = hs.shape
    TM = cols.shape[1]
    E, _, D = wd.shape
    tn = _pick(D, (512, 256, 128))
    tk = _pick(F, (1408, 1024, 512, 256, 128))
    n_kf = F // tk
    gs = pltpu.PrefetchScalarGridSpec(
        num_scalar_prefetch=2, grid=(NT, D // tn, E, n_kf),
        in_specs=[pl.BlockSpec((1, CAP, tk), lambda i, n, e, k, o, r: (i, 0, k)),
                  pl.BlockSpec((1, tk, tn), lambda i, n, e, k, o, r: (e, k, n)),
                  pl.BlockSpec((TM, tn), lambda i, n, e, k, o, r: (i, n)),
                  pl.BlockSpec((1, TM, LANES), lambda i, n, e, k, o, r: (i, 0, 0)),
                  pl.BlockSpec((1, E_PAD, TM), lambda i, n, e, k, o, r: (i, 0, 0))],
        out_specs=pl.BlockSpec((TM, tn), lambda i, n, e, k, o, r: (i, n)),
        scratch_shapes=[pltpu.VMEM((CAP, tn), F32)])
    return pl.pallas_call(
        functools.partial(_moe_down_kernel, sizes=_run_sizes(TM), GC=_pick(CAP, (512, 256, 128)), n_e=E, n_kf=n_kf),
        grid_spec=gs,
        out_shape=jax.ShapeDtypeStruct(x.shape, F32),
        compiler_params=_cparams("arbitrary", "arbitrary", "arbitrary", "arbitrary"),
        name="moe_down",
    )(off, run, hs, wd, x, cols, rows)


def _rmsnorm_kernel(x_ref, g_ref, o_ref):
    x = x_ref[...]
    ms = jnp.mean(x * x, axis=-1, keepdims=True)
    o_ref[...] = x * lax.rsqrt(ms + NORM_EPS) * g_ref[...]


def rmsnorm(x, g):
    M, D = x.shape
    tm = _pick(M, (512, 256, 128, 8))
    return pl.pallas_call(
        _rmsnorm_kernel,
        grid=(M // tm,),
        in_specs=[pl.BlockSpec((tm, D), lambda i: (i, 0)), pl.BlockSpec((1, D), lambda i: (0, 0))],
        out_specs=pl.BlockSpec((tm, D), lambda i: (i, 0)),
        out_shape=jax.ShapeDtypeStruct((M, D), F32),
        compiler_params=_cparams("parallel"),
        name="final_norm",
    )(x, g.reshape(1, D))


def _rwkv_prep_kernel(p_ref, prev_ref, s0_ref, mu_ref, w0_ref, w2h_ref, w2l_ref, a0_ref, a2h_ref, a2l_ref,
                      g2h_ref, g2l_ref, kk_ref, ka_ref, bd_ref,
                      r_ref, lw_ref, k_ref, v_ref, kn_ref, b_ref, g_ref, *, T):
    p = p_ref[0]
    tm = p.shape[0]
    row = lax.broadcasted_iota(jnp.int32, (tm, 1), 0)
    shifted = jnp.where(row == 0, prev_ref[0, 7:8, :], pltpu.roll(p, 1, 0))
    nseq = s0_ref.shape[0]
    s0_rows = jnp.broadcast_to(s0_ref[...], (nseq, tm // nseq, p.shape[1])).reshape(tm, p.shape[1])
    first = ((pl.program_id(0) * tm + row) % T) == 0
    ps = p + (jnp.where(first, s0_rows, shifted) - p) * mu_ref[...]
    W = A_WIDTH
    r = ps[:, 0:W]
    k = ps[:, W:2 * W]
    v = ps[:, 2 * W:3 * W]
    o = 3 * W
    wd = ps[:, o:o + A_LORA_W]
    ad = ps[:, o + A_LORA_W:o + A_LORA_W + A_LORA_A]
    gd = ps[:, o + A_LORA_W + A_LORA_A:]
    z = -(w0_ref[...] + _dot3(jnp.tanh(wd), w2h_ref[...], w2l_ref[...]))
    softplus = jnp.maximum(z, 0.0) + jnp.log(1.0 + jnp.exp(-jnp.abs(z)))
    lw_ref[...] = -jnp.exp(-softplus - 0.5)
    a = _sigmoid(a0_ref[...] + _dot3(ad, a2h_ref[...], a2l_ref[...]))
    g_ref[...] = _dot3(_sigmoid(gd), g2h_ref[...], g2l_ref[...])
    kk = k * kk_ref[...]
    ss = _dot_exact_rhs(kk * kk, bd_ref[...])
    kn = kk / jnp.maximum(jnp.sqrt(ss), 1e-12)
    r_ref[...] = r
    v_ref[...] = v
    kn_ref[...] = kn
    b_ref[...] = kn * a
    k_ref[...] = k * (1.0 + (a - 1.0) * ka_ref[...])


def rwkv_prep(pab, shift0, prm, T):
    M = pab.shape[1]
    tm = _pick(M, (256, 128, 8))
    assert T % tm == 0 or tm % T == 0
    nseq = max(tm // T, 1)
    s0_map = (lambda i: (i, 0, 0)) if tm >= T else (lambda i: ((i * tm) // T, 0, 0))
    W = A_WIDTH
    row = lambda n: pl.BlockSpec((1, n), lambda i: (0, 0))
    mat = lambda a, b: pl.BlockSpec((a, b), lambda i: (0, 0))
    out = pl.BlockSpec((tm, W), lambda i: (i, 0))
    return pl.pallas_call(
        functools.partial(_rwkv_prep_kernel, T=T),
        grid=(M // tm,),
        in_specs=[pl.BlockSpec((1, tm, A_PROJ), lambda i: (0, i, 0)),
                  pl.BlockSpec((1, 8, A_PROJ), lambda i: (0, jnp.maximum(i * (tm // 8) - 1, 0), 0)),
                  pl.BlockSpec((nseq, 1, A_PROJ), s0_map),
                  row(A_PROJ), row(W), mat(A_LORA_W, W), mat(A_LORA_W, W),
                  row(W), mat(A_LORA_A, W), mat(A_LORA_A, W),
                  mat(A_LORA_G, W), mat(A_LORA_G, W), row(W), row(W), mat(W, W)],
        out_specs=[out] * 7,
        out_shape=[jax.ShapeDtypeStruct((M, W), F32)] * 7,
        compiler_params=_cparams("parallel"),
        name="rwkv_prep",
    )(pab, pab, shift0[:, None, :], prm["mu"], prm["w0"], *prm["w2"], prm["a0"], *prm["a2"], *prm["g2"],
      prm["k_k"], prm["k_a"], prm["bd"])


def _rwkv_chunk_kernel(r_ref, lw_ref, k_ref, v_ref, kn_ref, b_ref, s0_ref, y_ref, s_ref, *, C, nc, Bb):
    C2 = 2 * C
    lane = lax.broadcasted_iota(jnp.int32, (C2, LANES), 1)
    rowi = lax.broadcasted_iota(jnp.int32, (C2, LANES), 0)
    mask2 = ((rowi >= C) == (lane >= HEAD_DIM)).astype(F32)
    ri = lax.broadcasted_iota(jnp.int32, (C2, C2), 0)
    ci = lax.broadcasted_iota(jnp.int32, (C2, C2), 1)
    same = (ri >= C) == (ci >= C)
    strict = same & (ri > ci)
    incl = same & (ri >= ci)
    eye = (ri == ci).astype(F32)
    ti = lax.broadcasted_iota(jnp.int32, (C, C), 0)
    tj = lax.broadcasted_iota(jnp.int32, (C, C), 1)
    tri = (ti >= tj).astype(BF16)
    n_levels = max(C.bit_length() - 2, 0)
    merged = C2 % LANES == 0
    streams = range(Bb)

    def stack(x):
        return jnp.concatenate([x, x], axis=0) * mask2

    def each(f, *lists):
        return [f(*[l[i] for l in lists]) for i in streams]

    @pl.when(pl.program_id(2) == 0)
    def _():
        s_ref[...] = s0_ref[...]

    def chunk(c, states):
        sl = pl.ds(pl.multiple_of(c * C, C), C)
        S = list(states)
        load = lambda ref: [ref[i, sl, :] for i in streams]
        r, lw, k, v, kn, b = (load(ref) for ref in (r_ref, lw_ref, k_ref, v_ref, kn_ref, b_ref))
        cs = each(lambda x: _dot_exact_lhs3(tri, x), lw)
        gt = each(lambda x: x[C - 1:C, :], cs)
        e_neg = each(lambda x: jnp.exp(-x), cs)
        e_rem = each(lambda g, x: jnp.exp(g - x), gt, cs)
        KT = each(lambda a, x, l: stack(a * jnp.exp(x - l)), kn, cs, lw)
        BI = each(lambda a, e: stack(a * e), b, e_neg)
        KI = each(lambda a, e: stack(a * e), k, e_neg)
        RT = each(lambda a, x: stack(a * jnp.exp(x)), r, cs)
        V2 = each(stack, v)
        KG = each(lambda a, e: stack(a * e), k, e_rem)
        BG = each(lambda a, e: stack(a * e), b, e_rem)
        if merged:
            quad = each(lambda kt, rt, bi, ki: _dot_nt(jnp.concatenate([kt, rt], axis=0),
                                                        jnp.concatenate([bi, ki], axis=0)), KT, RT, BI, KI)
            a_kb = each(lambda q: q[:C2, :C2], quad)
            a_kv = each(lambda q: q[:C2, C2:], quad)
            a_rb = each(lambda q: q[C2:, :C2], quad)
            a_rk = each(lambda q: q[C2:, C2:], quad)
        else:
            a_kb = each(_dot_nt, KT, BI)
            a_kv = each(_dot_nt, KT, KI)
            a_rb = each(_dot_nt, RT, BI)
            a_rk = each(_dot_nt, RT, KI)
        a_rb = each(lambda x: jnp.where(incl, x, 0.0), a_rb)
        a_rk = each(lambda x: jnp.where(incl, x, 0.0), a_rk)
        pw = each(lambda x: -jnp.where(strict, x, 0.0), a_kb)
        tinv = each(lambda x: eye + x, pw)
        if n_levels:
            pw = each(_dot, pw, pw)
            for _ in range(n_levels - 1):
                both = each(lambda p, t: _dot(jnp.concatenate([p, t], axis=0), p), pw, tinv)
                pw = each(lambda x: x[:C2], both)
                tinv = each(lambda t, x: t + x[C2:], tinv, both)
            tinv = each(lambda t, p: t + _dot(t, p), tinv, pw)
        av = each(lambda x, vv: _dot(jnp.where(strict, x, 0.0), vv), a_kv, V2)
        kpw = each(lambda t, kt, a: _dot(t, jnp.concatenate([kt, a], axis=1)), tinv, KT, av)
        corr = each(_dot, a_rb, kpw)
        rp = each(lambda x, c_: x - c_[:, :LANES], RT, corr)
        y1 = each(lambda a, vv, c_: _dot(a, vv) - c_[:, LANES:], a_rk, V2, corr)
        low = each(_dot_tn, kpw, BG)
        mlow = each(lambda x: x[:LANES], low)
        nt = each(lambda vv, kg, x: _dot_tn(vv, kg) - x[LANES:], V2, KG, low)
        y2 = each(lambda p, s, y: _dot_nt(p, s) + y, rp, S, y1)
        for i in streams:
            y_ref[i, sl, :] = y2[i][:C] + y2[i][C:]
        new = each(lambda s, g, m, n: s * jnp.exp(g) - _dot3(s, *_split2(m)) + n, S, gt, mlow, nt)
        return tuple(new)

    final = lax.fori_loop(0, nc, chunk, tuple(s_ref[i, 0] for i in streams))
    for i in streams:
        s_ref[i, 0] = final[i]


def rwkv_chunk(r, lw, k, v, kn, b, s_bd, B, T):
    C = min(T, 64)
    Tt = _pick(T, (512, 256, 128, 64)) if T > C else T
    nc = Tt // C
    Bb = _pick(B, (4, 2, 1)) if T > C else _pick(B, (8, 4, 2, 1))
    HP = A_WIDTH // LANES
    seq = pl.BlockSpec((Bb, Tt, LANES), lambda i, h, t: (i, t, h))
    st = pl.BlockSpec((Bb, 1, LANES, LANES), lambda i, h, t: (i, h, 0, 0))
    r3 = lambda x: x.reshape(B, T, A_WIDTH)
    y, s = pl.pallas_call(
        functools.partial(_rwkv_chunk_kernel, C=C, nc=nc, Bb=Bb),
        grid=(B // Bb, HP, T // Tt),
        in_specs=[seq] * 6 + [st],
        out_specs=[seq, st],
        out_shape=[jax.ShapeDtypeStruct((B, T, A_WIDTH), F32),
                   jax.ShapeDtypeStruct((B, HP, LANES, LANES), F32)],
        compiler_params=_cparams("parallel", "parallel", "arbitrary"),
        name="rwkv_chunk",
    )(r3(r), r3(lw), r3(k), r3(v), r3(kn), r3(b), s_bd)
    return y.reshape(B * T, A_WIDTH), s


def _rwkv_post_kernel(y_ref, r_ref, k_ref, v_ref, g_ref, rk_ref, lnw_ref, lnb_ref, bd_ref, o_ref):
    y = y_ref[...]
    bd = bd_ref[...]
    inv_n = 1.0 / HEAD_DIM
    mean = _dot_exact_rhs(y, bd) * inv_n
    d = y - mean
    var = _dot_exact_rhs(d * d, bd) * inv_n
    yn = d * lax.rsqrt(var + RWKV_GN_EPS) * lnw_ref[...] + lnb_ref[...]
    bonus = _dot_exact_rhs(r_ref[...] * k_ref[...] * rk_ref[...], bd)
    o_ref[...] = _bf((yn + bonus * v_ref[...]) * g_ref[...])


def rwkv_post(y, r, k, v, g, prm):
    M, W = y.shape
    tm = _pick(M, (256, 128, 8))
    blk = pl.BlockSpec((tm, W), lambda i: (i, 0))
    row = pl.BlockSpec((1, W), lambda i: (0, 0))
    return pl.pallas_call(
        _rwkv_post_kernel,
        grid=(M // tm,),
        in_specs=[blk] * 5 + [row] * 3 + [pl.BlockSpec((W, W), lambda i: (0, 0))],
        out_specs=blk,
        out_shape=jax.ShapeDtypeStruct((M, W), BF16),
        compiler_params=_cparams("parallel"),
        name="rwkv_post",
    )(y, r, k, v, g, prm["r_k"], prm["ln_w"], prm["ln_b"], prm["bd"])


def _logf_kernel(f_ref, b_ref, o_ref):
    z = f_ref[0] + b_ref[...]
    o_ref[...] = jnp.minimum(z, 0.0) - jnp.log(1.0 + jnp.exp(-jnp.abs(z)))


def fox_logf(pab, bias_row, col_block):
    M = pab.shape[1]
    tm = _pick(M, (1024, 512, 256, 128, 8))
    return pl.pallas_call(
        _logf_kernel,
        grid=(M // tm,),
        in_specs=[pl.BlockSpec((1, tm, LANES), lambda i: (1, i, col_block)),
                  pl.BlockSpec((1, LANES), lambda i: (0, 0))],
        out_specs=pl.BlockSpec((tm, LANES), lambda i: (i, 0)),
        out_shape=jax.ShapeDtypeStruct((M, LANES), F32),
        compiler_params=_cparams("parallel"),
        name="fox_logf",
    )(pab, bias_row)


def _cumsum_kernel(x_ref, o_ref, carry_ref):
    @pl.when(pl.program_id(1) == 0)
    def _():
        carry_ref[...] = jnp.zeros_like(carry_ref)

    ti = lax.broadcasted_iota(jnp.int32, (LANES, LANES), 0)
    tj = lax.broadcasted_iota(jnp.int32, (LANES, LANES), 1)
    triu = (ti <= tj).astype(BF16)
    hi, mid, lo = _split3(x_ref[0])
    cs = (jnp.dot(hi, triu, preferred_element_type=F32)
          + (jnp.dot(mid, triu, preferred_element_type=F32)
             + jnp.dot(lo, triu, preferred_element_type=F32))) + carry_ref[...]
    o_ref[0] = cs
    carry_ref[...] = jnp.broadcast_to(cs[:, LANES - 1:LANES], cs.shape)


def cumsum_lanes(xT):
    B, H, L = xT.shape
    return pl.pallas_call(
        _cumsum_kernel,
        grid=(B, L // LANES),
        in_specs=[pl.BlockSpec((1, H, LANES), lambda b, p: (b, 0, p))],
        out_specs=pl.BlockSpec((1, H, LANES), lambda b, p: (b, 0, p)),
        out_shape=jax.ShapeDtypeStruct((B, H, L), F32),
        scratch_shapes=[pltpu.VMEM((H, LANES), F32)],
        compiler_params=_cparams("parallel", "arbitrary"),
        name="fox_cumsum",
    )(xT)


def _head_lane_mask(n_rows, width, head):
    lane = lax.broadcasted_iota(jnp.int32, (n_rows, width), 1)
    return (lane // HEAD_DIM) == head


def _two_pass_attend(qs, n_past, add_past, diag_start, add_diag, kb_ref, vb_ref, s_ref, acc_ref, l_ref, m_ref):
    blk = MOBA_BLOCK
    rep = blk // LANES
    kd = kb_ref[pl.ds(diag_start, blk), :]
    s_d = add_diag(lax.dot_general(qs, kd, NT_DIMS, preferred_element_type=F32))
    s_ref[s_ref.shape[0] - 1] = s_d
    l_ref[...] = s_d

    def pass1(n, carry):
        kb = kb_ref[pl.ds(pl.multiple_of(n * blk, blk), blk), :]
        s = add_past(n, lax.dot_general(qs, kb, NT_DIMS, preferred_element_type=F32))
        s_ref[n] = s
        l_ref[...] = jnp.maximum(l_ref[...], s)
        return carry

    lax.fori_loop(0, n_past, pass1, 0)
    m_ref[...] = jnp.broadcast_to(jnp.max(l_ref[...], axis=1, keepdims=True), m_ref.shape)

    def probs(n):
        return jnp.exp2(s_ref[n] - jnp.concatenate([m_ref[...]] * rep, axis=1))

    p_d = probs(s_ref.shape[0] - 1)
    l_ref[...] = p_d
    acc_ref[...] = jnp.dot(_bf(p_d), vb_ref[pl.ds(diag_start, blk), :], preferred_element_type=F32)

    def pass2(n, carry):
        p = probs(n)
        l_ref[...] += p
        acc_ref[...] += jnp.dot(_bf(p), vb_ref[pl.ds(pl.multiple_of(n * blk, blk), blk), :],
                                preferred_element_type=F32)
        return carry

    lax.fori_loop(0, n_past, pass2, 0)
    return acc_ref[...] / jnp.sum(l_ref[...], axis=1, keepdims=True)


def _attend_scratch(n_blocks, R):
    return [pltpu.VMEM((n_blocks + 1, R, MOBA_BLOCK), F32),
            pltpu.VMEM((R, GROUP_LANES), F32),
            pltpu.VMEM((R, MOBA_BLOCK), F32),
            pltpu.VMEM((R, LANES), F32)]


def _fox_prompt_kernel(q_ref, k_ref, v_ref, c_ref, o_ref, kb_ref, vb_ref, s_ref, acc_ref, l_ref, m_ref, *, tq, hpg):
    g = pl.program_id(1)
    qi = pl.program_id(2)

    @pl.when(qi == 0)
    def _():
        kb_ref[...] = _bf(k_ref[0])
        vb_ref[...] = _bf(v_ref[0])

    q = q_ref[0] * (HEAD_DIM ** -0.5 * LOG2E)
    qs = _bf(jnp.concatenate([jnp.where(_head_lane_mask(tq, GROUP_LANES, h), q, 0.0) for h in range(hpg)], axis=0))
    causal = lax.broadcasted_iota(jnp.int32, (tq, tq), 0) >= lax.broadcasted_iota(jnp.int32, (tq, tq), 1)

    def add_bias(n, s, mask):
        parts = []
        for h in range(hpg):
            c_row = c_ref[0, pl.ds(g * hpg + h, 1), pl.ds(pl.multiple_of(n * tq, tq), tq)] * LOG2E
            sh = s[h * tq:(h + 1) * tq, :] - c_row
            parts.append(jnp.where(causal, sh, NEG_INF) if mask else sh)
        return jnp.concatenate(parts, axis=0)

    o = _two_pass_attend(qs, qi, lambda n, s: add_bias(n, s, False), pl.multiple_of(qi * tq, tq),
                         lambda s: add_bias(qi, s, True), kb_ref, vb_ref, s_ref, acc_ref, l_ref, m_ref)
    out = jnp.zeros((tq, GROUP_LANES), F32)
    for h in range(hpg):
        out = out + jnp.where(_head_lane_mask(tq, GROUP_LANES, h), o[h * tq:(h + 1) * tq, :], 0.0)
    o_ref[...] = _bf(out)


def fox_prompt(pab, cT, B, T):
    tq = MOBA_BLOCK
    nq = T // tq
    hpg = GROUP_LANES // HEAD_DIM
    G = B_WIDTH // GROUP_LANES
    return pl.pallas_call(
        functools.partial(_fox_prompt_kernel, tq=tq, hpg=hpg),
        grid=(B, G, nq),
        in_specs=[pl.BlockSpec((1, tq, GROUP_LANES), lambda b, g, qi: (1, b * nq + qi, g)),
                  pl.BlockSpec((1, T, GROUP_LANES), lambda b, g, qi: (1, b, G + g)),
                  pl.BlockSpec((1, T, GROUP_LANES), lambda b, g, qi: (1, b, 2 * G + g)),
                  pl.BlockSpec((1, B_HEADS, T), lambda b, g, qi: (b, 0, 0))],
        out_specs=pl.BlockSpec((tq, GROUP_LANES), lambda b, g, qi: (b * nq + qi, g)),
        out_shape=jax.ShapeDtypeStruct((B * T, B_WIDTH), BF16),
        scratch_shapes=[pltpu.VMEM((T, GROUP_LANES), BF16),
                        pltpu.VMEM((T, GROUP_LANES), BF16)] + _attend_scratch(nq - 1, hpg * tq),
        compiler_params=_cparams("parallel", "parallel", "arbitrary"),
        name="fox_prompt",
    )(pab, pab, pab, cT)


def _moba_slopes(n_rows, rows_per_head, i, kv_base):
    c = lax.broadcasted_iota(jnp.int32, (n_rows, 1), 0) // rows_per_head
    head = C_REP * (kv_base + c) + i
    return jnp.exp2(-8.0 * (head + 1).astype(F32) / C_HEADS)


def _top_blocks(z, live, idxf, axis=1):
    sel = jnp.zeros(z.shape, F32)
    for _ in range(MOBA_TOPK):
        m = jnp.max(z, axis=axis, keepdims=True)
        idx = jnp.min(jnp.where((z == m) & live, idxf, float(LANES)), axis=axis, keepdims=True)
        pick = idxf == idx
        sel = jnp.where(pick, 1.0, sel)
        z = jnp.where(pick, NEG_INF, z)
    return sel


def _moba_prompt_kernel(q0_ref, q1_ref, q2_ref, q3_ref, k_ref, v_ref, o_ref,
                        kb_ref, vb_ref, km_ref, s_ref, acc_ref, l_ref, m_ref, *, tq, nb, cpg):
    g = pl.program_id(1)
    qi = pl.program_id(2)
    R = cpg * tq
    blk = MOBA_BLOCK

    @pl.when(qi == 0)
    def _():
        km_ref[...] = jnp.zeros_like(km_ref)
        for n in range(nb):
            kblk = k_ref[0, n * blk:(n + 1) * blk, :]
            kb_ref[n * blk:(n + 1) * blk, :] = _bf(kblk)
            vb_ref[n * blk:(n + 1) * blk, :] = _bf(v_ref[0, n * blk:(n + 1) * blk, :])
            km_ref[n:n + 1, :] = jnp.sum(kblk, axis=0, keepdims=True) * (1.0 / blk)

    own = (qi * tq) // blk
    row_tok = lax.broadcasted_iota(jnp.int32, (R, 1), 0) % tq
    qpos = (qi * tq + row_tok).astype(F32)
    koff = lax.broadcasted_iota(jnp.int32, (R, blk), 1)
    causal = (qi * tq - own * blk + row_tok) >= koff
    blockf = lax.broadcasted_iota(jnp.int32, (NB_PAD, R), 0).astype(F32)
    past = blockf < own
    km = km_ref[0:NB_PAD, :]
    spread_row = lax.broadcasted_iota(jnp.int32, (NB_PAD, LANES), 0)
    for i, q_ref in enumerate((q0_ref, q1_ref, q2_ref, q3_ref)):
        q = q_ref[0] * (HEAD_DIM ** -0.5 * LOG2E)
        qs = jnp.concatenate([jnp.where(_head_lane_mask(tq, GROUP_LANES, c), q, 0.0) for c in range(cpg)], axis=0)
        gate = _dot3_nt(km, *_split2(qs))
        sel_bf = _bf(_top_blocks(jnp.where(past, gate, NEG_INF), past, blockf, axis=0))
        slope = _moba_slopes(R, tq, i, g * cpg) * LOG2E
        b0 = slope * koff.astype(F32)
        slope_rep = jnp.broadcast_to(slope, (R, LANES))
        sq_rep = slope_rep * qpos
        own_term = b0 + slope * ((own * blk).astype(F32) - qpos)

        def add_past(n, s, b0=b0, sel_bf=sel_bf, slope_rep=slope_rep, sq_rep=sq_rep):
            chosen = _dot_tn(sel_bf, (spread_row == n).astype(BF16)) > 0.5
            rt = jnp.where(chosen, slope_rep * (n * blk).astype(F32) - sq_rep, NEG_INF)
            return s + b0 + jnp.concatenate([rt] * (blk // LANES), axis=1)

        def add_own(s, own_term=own_term):
            return jnp.where(causal, s + own_term, NEG_INF)

        o = _two_pass_attend(_bf(qs), own, add_past, pl.multiple_of(own * blk, blk), add_own,
                             kb_ref, vb_ref, s_ref, acc_ref, l_ref, m_ref)
        out = jnp.zeros((tq, GROUP_LANES), F32)
        for c in range(cpg):
            out = out + jnp.where(_head_lane_mask(tq, GROUP_LANES, c), o[c * tq:(c + 1) * tq, :], 0.0)
        o_ref[i] = _bf(out)


def moba_prompt(p1, B, T):
    tq = MOBA_BLOCK
    nq = T // tq
    nb = T // MOBA_BLOCK
    assert nb <= NB_PAD
    cpg = GROUP_LANES // HEAD_DIM
    G = C_KV_WIDTH // GROUP_LANES
    R = cpg * tq
    qspec = lambda i: pl.BlockSpec((1, tq, GROUP_LANES), lambda b, g, qi, i=i: (i, b * nq + qi, g))
    return pl.pallas_call(
        functools.partial(_moba_prompt_kernel, tq=tq, nb=nb, cpg=cpg),
        grid=(B, G, nq),
        in_specs=[qspec(0), qspec(1), qspec(2), qspec(3),
                  pl.BlockSpec((1, T, GROUP_LANES), lambda b, g, qi: (4, b, g)),
                  pl.BlockSpec((1, T, GROUP_LANES), lambda b, g, qi: (5, b, g))],
        out_specs=pl.BlockSpec((C_REP, tq, GROUP_LANES), lambda b, g, qi: (0, b * nq + qi, g)),
        out_shape=jax.ShapeDtypeStruct((C_REP, B * T, C_KV_WIDTH), BF16),
        scratch_shapes=[pltpu.VMEM((T, GROUP_LANES), BF16),
                        pltpu.VMEM((T, GROUP_LANES), BF16),
                        pltpu.VMEM((LANES, GROUP_LANES), F32)] + _attend_scratch(nb - 1, R),
        compiler_params=_cparams("parallel", "parallel", "arbitrary"),
        name="moba_prompt",
    )(p1, p1, p1, p1, p1, p1)


def _flat_online_update(s_list, v_list, m_ref, l_ref, acc_ref):
    m_old = m_ref[...]
    m_new = m_old
    for s in s_list:
        m_new = jnp.maximum(m_new, jnp.max(s, axis=1, keepdims=True))
    alpha = jnp.exp(m_old - m_new)
    l_new = alpha * l_ref[...]
    acc = alpha * acc_ref[...]
    for s, v in zip(s_list, v_list):
        p = jnp.exp(s - m_new)
        l_new = l_new + jnp.sum(p, axis=1, keepdims=True)
        acc = acc + jnp.dot(_bf(p), v, preferred_element_type=F32)
    m_ref[...] = m_new
    l_ref[...] = l_new
    acc_ref[...] = acc


def _flat_cumsum(x, carry, heads):
    rows = x.shape[0]
    lane = lax.broadcasted_iota(jnp.int32, x.shape, 1)
    d = heads
    while d < LANES:
        x = x + jnp.where(lane >= d, pltpu.roll(x, d, 1), 0.0)
        d *= 2
    tot = jnp.where(lane >= LANES - heads, x, 0.0)
    d = heads
    while d < LANES:
        tot = tot + pltpu.roll(tot, LANES - d, 1)
        d *= 2
    if rows == 1:
        return x + carry, carry + tot
    ri = lax.broadcasted_iota(jnp.int32, (rows, rows), 0)
    ci = lax.broadcasted_iota(jnp.int32, (rows, rows), 1)
    ex = _dot_exact_lhs3((ri > ci).astype(BF16), tot)
    c = x + ex + carry
    new_carry = carry + ex[rows - 1:rows, :] + tot[rows - 1:rows, :]
    return c, new_carry


def _rows_to_lanes(c):
    return jnp.concatenate([c[r:r + 1, :] for r in range(c.shape[0])], axis=1)


def _fox_sample_kernel(pt_ref, q_ref, kn_ref, vn_ref, lfn_ref, *rest, T, NP, n_steps):
    k_refs = rest[0:NP]
    v_refs = rest[NP:2 * NP]
    lf_refs = rest[2 * NP:3 * NP]
    o_ref, m_ref, l_ref, acc_ref, carry_ref = rest[3 * NP:]
    s_id = pl.program_id(1)
    H = B_HEADS
    R = H * T

    @pl.when(s_id == 0)
    def _():
        m_ref[...] = jnp.full_like(m_ref, NEG_INF)
        l_ref[...] = jnp.zeros_like(l_ref)
        acc_ref[...] = jnp.zeros_like(acc_ref)
        carry_ref[...] = jnp.zeros_like(carry_ref)

    qb = _bf(q_ref[0] * (HEAD_DIM ** -0.5))
    row_head = lax.broadcasted_iota(jnp.int32, (R, LANES), 0) // T
    lane = lax.broadcasted_iota(jnp.int32, (R, LANES), 1)
    same_head = (lane % H) == row_head
    n_rows = k_refs[0].shape[1] * H
    head_mask = jnp.concatenate([jnp.where(same_head, 0.0, NEG_INF)] * (n_rows // LANES), axis=1)

    carry = carry_ref[...]
    s_list, v_list = [], []
    for j in range(NP):
        c, carry = _flat_cumsum(lf_refs[j][0], carry, H)
        s = _dot_nt(qb, k_refs[j][0].reshape(n_rows, HEAD_DIM))
        s_list.append(s - _rows_to_lanes(c) + head_mask)
        v_list.append(_bf(v_refs[j][0].reshape(n_rows, HEAD_DIM)))
    carry_ref[...] = carry

    @pl.when(s_id < n_steps - 1)
    def _():
        _flat_online_update(s_list, v_list, m_ref, l_ref, acc_ref)

    @pl.when(s_id == n_steps - 1)
    def _():
        c_new, _ = _flat_cumsum(lfn_ref[0], carry, H)
        row_tok = lax.broadcasted_iota(jnp.int32, (R, LANES), 0) % T
        ok = same_head & ((lane // H) <= row_tok)
        s_new = jnp.where(ok, _dot_nt(qb, kn_ref[0]) - c_new, NEG_INF)
        _flat_online_update(s_list + [s_new], v_list + [_bf(vn_ref[0])], m_ref, l_ref, acc_ref)
        o_ref[0] = acc_ref[...] / l_ref[...]


def fox_sample(q_flat, kn_flat, vn_flat, lfn_flat, cache_k, cache_v, cache_lf, page_table):
    B, n_pages = page_table.shape
    R = q_flat.shape[1]
    T = R // B_HEADS
    page_shape = cache_k.shape[1:]
    NP = 4
    n_steps = n_pages // NP
    page = lambda j: (lambda b, s, pt: (pt[b, s * NP + j], 0, 0))
    page4 = lambda j: (lambda b, s, pt: (pt[b, s * NP + j], 0, 0, 0))
    per_b = lambda b, s, pt: (b, 0, 0)
    in_specs = [pl.BlockSpec((1, R, HEAD_DIM), per_b),
                pl.BlockSpec((1, T * B_HEADS, HEAD_DIM), per_b),
                pl.BlockSpec((1, T * B_HEADS, HEAD_DIM), per_b),
                pl.BlockSpec((1, 1, LANES), per_b)]
    in_specs += [pl.BlockSpec((1, *page_shape), page4(j)) for j in range(NP)] * 2
    in_specs += [pl.BlockSpec((1, *cache_lf.shape[1:]), page(j)) for j in range(NP)]
    gs = pltpu.PrefetchScalarGridSpec(
        num_scalar_prefetch=1, grid=(B, n_steps), in_specs=in_specs,
        out_specs=pl.BlockSpec((1, R, HEAD_DIM), per_b),
        scratch_shapes=[pltpu.VMEM((R, 1), F32), pltpu.VMEM((R, 1), F32),
                        pltpu.VMEM((R, HEAD_DIM), F32), pltpu.VMEM((1, LANES), F32)])
    return pl.pallas_call(
        functools.partial(_fox_sample_kernel, T=T, NP=NP, n_steps=n_steps),
        grid_spec=gs,
        out_shape=jax.ShapeDtypeStruct((B, R, HEAD_DIM), F32),
        compiler_params=_cparams("parallel", "arbitrary"),
        name="fox_sample",
    )(page_table, q_flat, kn_flat, vn_flat, lfn_flat, *([cache_k] * NP), *([cache_v] * NP), *([cache_lf] * NP))


def _moba_sample_kernel(pt_ref, q_ref, kn_ref, vn_ref, *rest, T, n_pages, q_start):
    k_refs = rest[0:n_pages]
    v_refs = rest[n_pages:2 * n_pages]
    o_ref, m_ref, l_ref, acc_ref = rest[2 * n_pages:]
    Hkv = C_KV_HEADS
    R = C_HEADS * T
    rows_per_kv = C_REP * T
    page = k_refs[0].shape[1]
    page_rows = page * Hkv
    ppb = MOBA_BLOCK // page
    nbp = n_pages // ppb
    own = q_start // MOBA_BLOCK
    blk_cols = MOBA_BLOCK * Hkv
    per_row = LANES // Hkv

    q = q_ref[0] * (HEAD_DIM ** -0.5)
    qb = _bf(q)
    rowi = lax.broadcasted_iota(jnp.int32, (R, 1), 0)
    row_kv = rowi // rows_per_kv
    row_sub = (rowi // T) % C_REP
    row_tok = rowi % T
    slope = jnp.exp2(-8.0 * (C_REP * row_kv + row_sub + 1).astype(F32) / C_HEADS)
    qpos = (q_start + row_tok).astype(F32)
    lane = lax.broadcasted_iota(jnp.int32, (R, LANES), 1)
    same_kv = (lane % Hkv) == row_kv

    sums = []
    for n in range(nbp):
        acc = jnp.zeros((Hkv, HEAD_DIM), F32)
        for j in range(ppb):
            acc = acc + jnp.sum(k_refs[n * ppb + j][0], axis=0)
        sums.append(acc * (1.0 / MOBA_BLOCK))
    km = jnp.concatenate(sums + [jnp.zeros((LANES - nbp * Hkv, HEAD_DIM), F32)], axis=0)
    gate = _dot3_nt(q, *_split2(km))
    live = same_kv & (lane < nbp * Hkv) & ((lane // Hkv) < own)
    sel = _top_blocks(jnp.where(live, gate, NEG_INF), live, lane.astype(F32))

    m_ref[...] = jnp.full_like(m_ref, NEG_INF)
    l_ref[...] = jnp.zeros_like(l_ref)
    acc_ref[...] = jnp.zeros_like(acc_ref)

    tok_new = lane // Hkv
    ok = same_kv & (tok_new <= row_tok) & (tok_new < T)
    dist = (row_tok - tok_new).astype(F32)
    s_new = jnp.where(ok, _dot_nt(qb, kn_ref[0]) - slope * dist, NEG_INF)
    _flat_online_update([s_new], [_bf(vn_ref[0])], m_ref, l_ref, acc_ref)

    col_tok = jnp.concatenate([(lane // Hkv + r * per_row) for r in range(blk_cols // LANES)], axis=1).astype(F32)
    col_term = jnp.concatenate([jnp.where(same_kv, 0.0, NEG_INF)] * (blk_cols // LANES), axis=1) + slope * col_tok
    for n in range(nbp):
        chosen = jnp.max(jnp.where((lane // Hkv) == n, sel, 0.0), axis=1, keepdims=True) > 0.5
        row_term = jnp.where(chosen, slope * (n * MOBA_BLOCK - qpos), NEG_INF)
        flat = lambda ref: _bf(ref[0].reshape(page_rows, HEAD_DIM))
        kb = jnp.concatenate([flat(k_refs[n * ppb + j]) for j in range(ppb)], axis=0)
        vb = jnp.concatenate([flat(v_refs[n * ppb + j]) for j in range(ppb)], axis=0)
        s = _dot_nt(qb, kb) + col_term + row_term
        _flat_online_update([s], [vb], m_ref, l_ref, acc_ref)
    o_ref[0] = acc_ref[...] / l_ref[...]


def moba_sample(q_flat, kn_flat, vn_flat, cache_k, cache_v, page_table, q_start):
    B, n_pages = page_table.shape
    R = q_flat.shape[1]
    T = R // C_HEADS
    page = lambda j: (lambda b, pt: (pt[b, j], 0, 0, 0))
    per_b = lambda b, pt: (b, 0, 0)
    in_specs = [pl.BlockSpec((1, R, HEAD_DIM), per_b),
                pl.BlockSpec((1, LANES, HEAD_DIM), per_b),
                pl.BlockSpec((1, LANES, HEAD_DIM), per_b)]
    in_specs += [pl.BlockSpec((1, *cache_k.shape[1:]), page(j)) for j in range(n_pages)] * 2
    gs = pltpu.PrefetchScalarGridSpec(
        num_scalar_prefetch=1, grid=(B,), in_specs=in_specs,
        out_specs=pl.BlockSpec((1, R, HEAD_DIM), per_b),
        scratch_shapes=[pltpu.VMEM((R, 1), F32), pltpu.VMEM((R, 1), F32), pltpu.VMEM((R, HEAD_DIM), F32)])
    return pl.pallas_call(
        functools.partial(_moba_sample_kernel, T=T, n_pages=n_pages, q_start=q_start),
        grid_spec=gs,
        out_shape=jax.ShapeDtypeStruct((B, R, HEAD_DIM), F32),
        compiler_params=_cparams("arbitrary"),
        name="moba_sample",
    )(page_table, q_flat, kn_flat, vn_flat, *([cache_k] * n_pages), *([cache_v] * n_pages))


def _block_diag_rows(x, n_heads):
    lane = lax.broadcasted_iota(jnp.int32, x.shape, 1) // HEAD_DIM
    return jnp.concatenate([jnp.where(lane == h, x, 0.0) for h in range(n_heads)], axis=0)


def _gather_heads(out, n_heads, T):
    lane = lax.broadcasted_iota(jnp.int32, (T, out.shape[1]), 1) // HEAD_DIM
    y = jnp.zeros((T, out.shape[1]), F32)
    for h in range(n_heads):
        y = y + jnp.where(lane == h, out[h * T:(h + 1) * T, :], 0.0)
    return y


def _pad_rows(x, n):
    return jnp.concatenate([x, jnp.zeros((n - x.shape[0], x.shape[1]), x.dtype)], axis=0)


def _softmax_pv(s_list, s_new, v_refs, v_new, width):
    m = jnp.max(s_new, axis=1, keepdims=True)
    for s in s_list:
        m = jnp.maximum(m, jnp.max(s, axis=1, keepdims=True))
    p = jnp.exp(s_new - m)
    l = jnp.sum(p, axis=1, keepdims=True)
    acc = jnp.dot(_bf(p), v_new, preferred_element_type=F32)
    for s, v_ref in zip(s_list, v_refs):
        p = jnp.exp(s - m)
        l = l + jnp.sum(p, axis=1, keepdims=True)
        acc = acc + _dot_nt(p, v_ref[0].reshape(width, v_ref.shape[-1]))
    return acc / l


def _fox_decode_kernel(pt_ref, q_ref, kn_ref, vn_ref, lfn_ref, *rest, T, NP):
    k_refs = rest[0:NP]
    v_refs = rest[NP:2 * NP]
    lf_refs = rest[2 * NP:3 * NP]
    o_ref = rest[3 * NP]
    H, W = B_HEADS, B_WIDTH
    R = H * T
    page = k_refs[0].shape[-1]
    qbd = _bf(_block_diag_rows(q_ref[0] * (HEAD_DIM ** -0.5), H))
    ti = lax.broadcasted_iota(jnp.int32, (page, page), 0)
    tj = lax.broadcasted_iota(jnp.int32, (page, page), 1)
    triu = (ti <= tj).astype(BF16)

    def cumsum(x, carry):
        hi, mid, lo = _split3(x)
        cs = (jnp.dot(hi, triu, preferred_element_type=F32)
              + (jnp.dot(mid, triu, preferred_element_type=F32)
                 + jnp.dot(lo, triu, preferred_element_type=F32))) + carry
        return cs, jnp.broadcast_to(cs[:, page - 1:page], cs.shape)

    def per_row(c):
        return jnp.broadcast_to(c[:, None, :], (H, T, page)).reshape(R, page)

    carry = jnp.zeros((H, page), F32)
    s_list = []
    for j in range(NP):
        cs, carry = cumsum(lf_refs[j][0], carry)
        s = jnp.dot(qbd, _bf(k_refs[j][0].reshape(W, page)), preferred_element_type=F32)
        s_list.append(s - per_row(cs))
    cs_new, _ = cumsum(lfn_ref[0], carry)
    row_tok = lax.broadcasted_iota(jnp.int32, (R, page), 0) % T
    key = lax.broadcasted_iota(jnp.int32, (R, page), 1)
    s_new = _dot_nt(qbd, _pad_rows(kn_ref[0], page)) - per_row(cs_new)
    s_new = jnp.where(key <= row_tok, s_new, NEG_INF)
    out = _softmax_pv(s_list, s_new, v_refs, _bf(_pad_rows(vn_ref[0], page)), W)
    o_ref[0] = _bf(_gather_heads(out, H, T))


def fox_decode(pab, lfn_t, cache_k, cache_v, cache_lf, page_table, T):
    B, NP = page_table.shape
    W = B_WIDTH
    page4 = lambda j: (lambda b, pt: (pt[b, j], 0, 0, 0))
    page3 = lambda j: (lambda b, pt: (pt[b, j], 0, 0))
    in_specs = [pl.BlockSpec((1, T, W), lambda b, pt: (1, b, 0)),
                pl.BlockSpec((1, T, W), lambda b, pt: (1, b, 1)),
                pl.BlockSpec((1, T, W), lambda b, pt: (1, b, 2)),
                pl.BlockSpec((1, *lfn_t.shape[1:]), lambda b, pt: (b, 0, 0))]
    in_specs += [pl.BlockSpec((1, *cache_k.shape[1:]), page4(j)) for j in range(NP)] * 2
    in_specs += [pl.BlockSpec((1, *cache_lf.shape[1:]), page3(j)) for j in range(NP)]
    gs = pltpu.PrefetchScalarGridSpec(
        num_scalar_prefetch=1, grid=(B,), in_specs=in_specs,
        out_specs=pl.BlockSpec((1, T, W), lambda b, pt: (b, 0, 0)))
    return pl.pallas_call(
        functools.partial(_fox_decode_kernel, T=T, NP=NP),
        grid_spec=gs,
        out_shape=jax.ShapeDtypeStruct((B, T, W), BF16),
        compiler_params=_cparams("arbitrary"),
        name="fox_decode",
    )(page_table, pab, pab, pab, lfn_t, *([cache_k] * NP), *([cache_v] * NP), *([cache_lf] * NP))


def _moba_decode_kernel(pt_ref, q_ref, kn_ref, vn_ref, *rest, T, NP, q_start):
    k_refs = rest[0:NP]
    v_refs = rest[NP:2 * NP]
    o_ref = rest[2 * NP]
    Hkv, W = C_KV_HEADS, C_KV_WIDTH
    page = k_refs[0].shape[-1]
    ppb = MOBA_BLOCK // page
    nbp = NP // ppb
    own = q_start // MOBA_BLOCK
    RG = Hkv * T
    R = C_REP * RG

    qbd = jnp.concatenate([_block_diag_rows(q_ref[i] * (HEAD_DIM ** -0.5), Hkv) for i in range(C_REP)], axis=0)
    qbd_bf = _bf(qbd)
    rowi = lax.broadcasted_iota(jnp.int32, (R, 1), 0)
    row_tok = rowi % T
    head = C_REP * ((rowi // T) % Hkv) + rowi // RG
    slope = jnp.exp2(-8.0 * (head + 1).astype(F32) / C_HEADS)
    qpos = (q_start + row_tok).astype(F32)
    lane = lax.broadcasted_iota(jnp.int32, (R, page), 1)
    lanef = lane.astype(F32)

    kps = [k_refs[j][0].reshape(W, page) for j in range(NP)]
    col = lax.broadcasted_iota(jnp.int32, (W, page), 1)
    km = jnp.zeros((W, page), F32)
    for n in range(nbp):
        tot = jnp.sum(kps[n * ppb], axis=1, keepdims=True)
        for j in range(1, ppb):
            tot = tot + jnp.sum(kps[n * ppb + j], axis=1, keepdims=True)
        km = jnp.where(col == n, tot * (1.0 / MOBA_BLOCK), km)
    live = lane < min(own, nbp)
    sel = _top_blocks(jnp.where(live, _dot3(qbd, *_split2(km)), NEG_INF), live, lanef)

    b0 = slope * lanef
    s_list = []
    for n in range(nbp):
        chosen = jnp.max(jnp.where(lane == n, sel, 0.0), axis=1, keepdims=True) > 0.5
        for j in range(n * ppb, (n + 1) * ppb):
            row_term = jnp.where(chosen, slope * (j * page - qpos), NEG_INF)
            s_list.append(jnp.dot(qbd_bf, _bf(kps[j]), preferred_element_type=F32) + b0 + row_term)
    s_new = _dot_nt(qbd_bf, _pad_rows(kn_ref[0], page)) - slope * (row_tok - lane).astype(F32)
    s_new = jnp.where(lane <= row_tok, s_new, NEG_INF)
    out = _softmax_pv(s_list, s_new, v_refs, _bf(_pad_rows(vn_ref[0], page)), W)
    for i in range(C_REP):
        o_ref[i] = _bf(_gather_heads(out[i * RG:(i + 1) * RG, :], Hkv, T))


def moba_decode(p1, cache_k, cache_v, page_table, T, q_start):
    B, NP = page_table.shape
    W = C_KV_WIDTH
    page4 = lambda j: (lambda b, pt: (pt[b, j], 0, 0, 0))
    in_specs = [pl.BlockSpec((C_REP, T, W), lambda b, pt: (0, b, 0)),
                pl.BlockSpec((1, T, W), lambda b, pt: (C_REP, b, 0)),
                pl.BlockSpec((1, T, W), lambda b, pt: (C_REP + 1, b, 0))]
    in_specs += [pl.BlockSpec((1, *cache_k.shape[1:]), page4(j)) for j in range(NP)] * 2
    gs = pltpu.PrefetchScalarGridSpec(
        num_scalar_prefetch=1, grid=(B,), in_specs=in_specs,
        out_specs=pl.BlockSpec((C_REP, T, W), lambda b, pt: (0, b, 0)))
    return pl.pallas_call(
        functools.partial(_moba_decode_kernel, T=T, NP=NP, q_start=q_start),
        grid_spec=gs,
        out_shape=jax.ShapeDtypeStruct((C_REP, B * T, W), BF16),
        compiler_params=_cparams("arbitrary"),
        name="moba_decode",
    )(page_table, p1, p1, p1, *([cache_k] * NP), *([cache_v] * NP))


def _row(v):
    return v.reshape(1, -1).astype(F32)


def _prep_params(norm0_mix_g, w_in0, fox_b_f, rwkv_mu, rwkv_w0, rwkv_w2, rwkv_a0, rwkv_a2, rwkv_g2, rwkv_k_k,
                 rwkv_k_a, rwkv_r_k, rwkv_ln_w, rwkv_ln_b, w_out0, norm0_ffn_g, ffn_w_gate, ffn_w_up,
                 ffn_w_down, norm1_mix_g, w_in1, w_out1, norm1_ffn_g, router_w, router_b, moe_w_gate,
                 moe_w_up, moe_w_down, norm_final_g):
    D = w_in0.shape[0]
    pad_b = A_PROJ - (3 * B_WIDTH + B_HEADS)
    w0 = jnp.concatenate([w_in0, jnp.zeros((D, pad_b), F32)], axis=1)
    wq = w_in1[:, :C_WIDTH].reshape(D, C_KV_HEADS, C_REP, HEAD_DIM).transpose(0, 2, 1, 3).reshape(D, C_WIDTH)
    w1 = jnp.concatenate([wq, w_in1[:, C_WIDTH:]], axis=1)
    wo1 = w_out1.reshape(C_KV_HEADS, C_REP, HEAD_DIM, D).transpose(1, 0, 2, 3).reshape(C_WIDTH, D)
    hd = lax.broadcasted_iota(jnp.int32, (A_WIDTH, A_WIDTH), 0) // HEAD_DIM
    hd2 = lax.broadcasted_iota(jnp.int32, (A_WIDTH, A_WIDTH), 1) // HEAD_DIM
    rw = jnp.concatenate([router_w.T, jnp.zeros((E_PAD - N_EXPERTS, D), F32)], axis=0)
    rb = jnp.concatenate([router_b, jnp.zeros((E_PAD - N_EXPERTS,), F32)])
    return dict(
        g0=norm0_mix_g, w0=_bf(w0),
        fox_b=jnp.concatenate([fox_b_f, jnp.zeros((LANES - B_HEADS,), F32)]).reshape(1, LANES),
        rwkv=dict(mu=_row(rwkv_mu), w0=_row(rwkv_w0), w2=_split2(rwkv_w2), a0=_row(rwkv_a0), a2=_split2(rwkv_a2),
                  g2=_split2(rwkv_g2), k_k=_row(rwkv_k_k), k_a=_row(rwkv_k_a), r_k=_row(rwkv_r_k),
                  ln_w=_row(rwkv_ln_w), ln_b=_row(rwkv_ln_b), bd=(hd == hd2).astype(BF16)),
        wo0=_bf(w_out0), g0f=norm0_ffn_g,
        ffn_g=_bf(ffn_w_gate)[None], ffn_u=_bf(ffn_w_up)[None], ffn_d=_bf(ffn_w_down),
        g1=norm1_mix_g, w1=_bf(w1), wo1=_bf(wo1), g1f=norm1_ffn_g,
        router=(*_split2(rw), rb.reshape(E_PAD, 1)),
        moe_g=_bf(moe_w_gate), moe_u=_bf(moe_w_up), moe_d=_bf(moe_w_down),
        gf=norm_final_g)


def _pair_states(S):
    B = S.shape[0]
    S = S.reshape(B, A_HEADS // 2, 2, HEAD_DIM, HEAD_DIM)
    z = jnp.zeros_like(S[:, :, 0])
    top = jnp.concatenate([S[:, :, 0], z], axis=-1)
    bot = jnp.concatenate([z, S[:, :, 1]], axis=-1)
    return jnp.concatenate([top, bot], axis=-2)


def _unpair_states(S):
    B = S.shape[0]
    a = S[:, :, :HEAD_DIM, :HEAD_DIM]
    b = S[:, :, HEAD_DIM:, HEAD_DIM:]
    return jnp.stack([a, b], axis=2).reshape(B, A_HEADS, HEAD_DIM, HEAD_DIM)


def _run(P, x, S0, shift0, caches, page_table):
    B, T, D = x.shape
    M = B * T
    xt = x.reshape(M, D)

    pab = norm_matmul(xt, P["g0"], P["w0"], A_PROJ, A_PROJ // 2)
    r, lw, k, v, kn, bb, gg = rwkv_prep(pab, shift0, P["rwkv"], T)
    s_in = jnp.zeros((B, A_HEADS // 2, LANES, LANES), F32) if S0 is None else _pair_states(S0)
    y, s_bd = rwkv_chunk(r, lw, k, v, kn, bb, s_in, B, T)
    ya = rwkv_post(y, r, k, v, gg, P["rwkv"])
    S_new = _unpair_states(s_bd)
    shift_new = pab[0].reshape(B, T, A_PROJ)[:, -1]

    pb = pab[1]
    fk = pb[:, B_WIDTH:2 * B_WIDTH].reshape(B, T, B_HEADS, HEAD_DIM)
    fv = pb[:, 2 * B_WIDTH:3 * B_WIDTH].reshape(B, T, B_HEADS, HEAD_DIM)
    logf = fox_logf(pab, P["fox_b"], 3 * B_WIDTH // LANES)[:, :B_HEADS].reshape(B, T, B_HEADS)
    if caches is None:
        yb = fox_prompt(pab, cumsum_lanes(jnp.swapaxes(logf, 1, 2)), B, T)
    else:
        page = caches["fox_k"].shape[1]
        by_head = lambda c: jnp.transpose(c, (0, 2, 3, 1))
        lfn_t = jnp.pad(jnp.swapaxes(logf, 1, 2), ((0, 0), (0, 0), (0, page - T)))
        yb = fox_decode(pab, lfn_t, by_head(caches["fox_k"]), by_head(caches["fox_v"]),
                        jnp.swapaxes(caches["fox_logf"], 1, 2), page_table, T).reshape(M, B_WIDTH)
    yab = jnp.concatenate([ya, yb], axis=1)[None]
    x1 = matmul_res(yab, P["wo0"], xt, tk=A_WIDTH + B_WIDTH)

    h = swiglu_up(x1, P["g0f"], P["ffn_g"], P["ffn_u"])
    x2 = matmul_res(h[None], P["ffn_d"], x1, tk=_pick(h.shape[1], (1408, 1024, 512, 256, 128)))

    p1 = norm_matmul(x2, P["g1"], P["w1"], C_KV_WIDTH, C_KV_WIDTH)
    mk = p1[4].reshape(B, T, C_KV_HEADS, HEAD_DIM)
    mv = p1[5].reshape(B, T, C_KV_HEADS, HEAD_DIM)
    if caches is None:
        y1 = moba_prompt(p1, B, T)
    else:
        page = caches["moba_k"].shape[1]
        by_head = lambda c: jnp.transpose(c, (0, 2, 3, 1))
        y1 = moba_decode(p1, by_head(caches["moba_k"]), by_head(caches["moba_v"]), page_table, T,
                         page_table.shape[1] * page)
    x3 = matmul_res(y1, P["wo1"], x2, tk=C_KV_WIDTH)

    xn, rows, cols, meta = moe_route(x3, P["g1f"], P["router"])
    off = meta[:, :N_EXPERTS, 0].astype(jnp.int32)
    run = meta[:, :N_EXPERTS, 1].astype(jnp.int32)
    hs = moe_up_sparse(xn, rows, off, run, P["moe_g"], P["moe_u"])
    out = rmsnorm(moe_down_sparse(hs, cols, rows, off, run, P["moe_d"], x3), P["gf"])
    return out.reshape(B, T, D), S_new, shift_new, fk, fv, logf, mk, mv


def kernel(x_prompt, x_sample, state_rwkv_S, state_rwkv_shift, cache_fox_k, cache_fox_v, cache_fox_logf,
           cache_moba_k, cache_moba_v, page_table, norm0_mix_g, w_in0, fox_b_f, rwkv_mu, rwkv_w0, rwkv_w2,
           rwkv_a0, rwkv_a2, rwkv_g2, rwkv_k_k, rwkv_k_a, rwkv_r_k, rwkv_ln_w, rwkv_ln_b, w_out0, norm0_ffn_g,
           ffn_w_gate, ffn_w_up, ffn_w_down, norm1_mix_g, w_in1, w_out1, norm1_ffn_g, router_w, router_b,
           moe_w_gate, moe_w_up, moe_w_down, norm_final_g):
    P = _prep_params(norm0_mix_g, w_in0, fox_b_f, rwkv_mu, rwkv_w0, rwkv_w2, rwkv_a0, rwkv_a2, rwkv_g2,
                     rwkv_k_k, rwkv_k_a, rwkv_r_k, rwkv_ln_w, rwkv_ln_b, w_out0, norm0_ffn_g, ffn_w_gate,
                     ffn_w_up, ffn_w_down, norm1_mix_g, w_in1, w_out1, norm1_ffn_g, router_w, router_b,
                     moe_w_gate, moe_w_up, moe_w_down, norm_final_g)
    n_prompt = x_prompt.shape[0]
    prompt = _run(P, x_prompt, None, jnp.zeros((n_prompt, A_PROJ), x_prompt.dtype), None, None)
    caches = dict(fox_k=cache_fox_k, fox_v=cache_fox_v, fox_logf=cache_fox_logf,
                  moba_k=cache_moba_k, moba_v=cache_moba_v)
    sample = _run(P, x_sample, state_rwkv_S, state_rwkv_shift, caches, page_table)
    return (prompt[0], sample[0], *prompt[1:], *sample[1:])
```

```python
import functools

import jax
import jax.numpy as jnp
from jax import lax
from jax.experimental import pallas as pl
from jax.experimental.pallas import tpu as pltpu

F32 = jnp.float32
BF16 = jnp.bfloat16

HEAD_DIM = 64
A_HEADS = 16
A_WIDTH = A_HEADS * HEAD_DIM
A_LORA_W = 64
A_LORA_A = 64
A_LORA_G = 128
A_PROJ = 3 * A_WIDTH + A_LORA_W + A_LORA_A + A_LORA_G
RWKV_GN_EPS = 64e-5
B_HEADS = 16
B_WIDTH = B_HEADS * HEAD_DIM
C_HEADS = 32
C_KV_HEADS = 8
C_REP = C_HEADS // C_KV_HEADS
C_WIDTH = C_HEADS * HEAD_DIM
C_KV_WIDTH = C_KV_HEADS * HEAD_DIM
MOBA_BLOCK = 256
MOBA_TOPK = 3
N_EXPERTS = 8
NORM_EPS = 1e-6

LANES = 128
GROUP_LANES = 256
VMEM_LIMIT = 56 * 1024 * 1024
NEG_INF = float("-inf")
LOG2E = 1.4426950408889634
NB_PAD = 16
NT_DIMS = (((1,), (1,)), ((), ()))


def _cparams(*sem):
    return pltpu.CompilerParams(dimension_semantics=sem, vmem_limit_bytes=VMEM_LIMIT)


def _bf(x):
    return x.astype(BF16)


def _dot(a, b):
    return jnp.dot(_bf(a), _bf(b), preferred_element_type=F32)


def _dot_nt(a, b):
    return lax.dot_general(_bf(a), _bf(b), NT_DIMS, preferred_element_type=F32)


def _dot_tn(a, b):
    return lax.dot_general(_bf(a), _bf(b), (((0,), (0,)), ((), ())), preferred_element_type=F32)


def _split2(x):
    hi = _bf(x)
    lo = _bf(x - hi.astype(F32))
    return hi, lo


def _split3(x):
    hi = _bf(x)
    r1 = x - hi.astype(F32)
    mid = _bf(r1)
    lo = _bf(r1 - mid.astype(F32))
    return hi, mid, lo


def _dot3(a, b_hi, b_lo):
    a_hi, a_lo = _split2(a)
    return (jnp.dot(a_hi, b_hi, preferred_element_type=F32)
            + (jnp.dot(a_hi, b_lo, preferred_element_type=F32)
               + jnp.dot(a_lo, b_hi, preferred_element_type=F32)))


def _dot3_nt(a, b_hi, b_lo):
    a_hi, a_lo = _split2(a)
    return (lax.dot_general(a_hi, b_hi, NT_DIMS, preferred_element_type=F32)
            + (lax.dot_general(a_hi, b_lo, NT_DIMS, preferred_element_type=F32)
               + lax.dot_general(a_lo, b_hi, NT_DIMS, preferred_element_type=F32)))


def _dot_exact_rhs(a, b_exact):
    a_hi, a_lo = _split2(a)
    return (jnp.dot(a_hi, b_exact, preferred_element_type=F32)
            + jnp.dot(a_lo, b_exact, preferred_element_type=F32))


def _dot_exact_lhs3(a_exact, x):
    hi, mid, lo = _split3(x)
    return (jnp.dot(a_exact, hi, preferred_element_type=F32)
            + (jnp.dot(a_exact, mid, preferred_element_type=F32)
               + jnp.dot(a_exact, lo, preferred_element_type=F32)))


def _sigmoid(x):
    return 1.0 / (1.0 + jnp.exp(-x))


def _pick(n, prefs):
    for p in prefs:
        if n % p == 0:
            return p
    return n


def _norm_mm_kernel(x_ref, g_ref, w_ref, o_ref, xn_ref):
    @pl.when(pl.program_id(1) == 0)
    def _():
        x = x_ref[...]
        ms = jnp.mean(x * x, axis=-1, keepdims=True)
        xn_ref[...] = _bf(x * lax.rsqrt(ms + NORM_EPS) * g_ref[...])

    o_ref[0] = jnp.dot(xn_ref[...], w_ref[...], preferred_element_type=F32)


def norm_matmul(x, g, w, cw, tn):
    M, K = x.shape
    N = w.shape[1]
    nc = N // cw
    tm = _pick(M, (1024, 512, 256, 128, 8))
    per = cw // tn
    return pl.pallas_call(
        _norm_mm_kernel,
        grid=(M // tm, N // tn),
        in_specs=[pl.BlockSpec((tm, K), lambda i, j: (i, 0)),
                  pl.BlockSpec((1, K), lambda i, j: (0, 0)),
                  pl.BlockSpec((K, tn), lambda i, j: (0, j))],
        out_specs=pl.BlockSpec((1, tm, tn), lambda i, j: (j // per, i, j % per)),
        out_shape=jax.ShapeDtypeStruct((nc, M, cw), F32),
        scratch_shapes=[pltpu.VMEM((tm, K), BF16)],
        compiler_params=_cparams("parallel", "arbitrary"),
        name="norm_matmul",
    )(x, g.reshape(1, K), w)


def _mm_res_kernel(a_ref, w_ref, r_ref, *rest, nk, final_norm):
    if final_norm:
        g_ref, o_ref, acc_ref = rest
    else:
        o_ref, acc_ref = rest
    k = pl.program_id(2)

    @pl.when(k == 0)
    def _():
        acc_ref[...] = r_ref[...]

    acc_ref[...] += jnp.dot(a_ref[0], w_ref[...], preferred_element_type=F32)

    @pl.when(k == nk - 1)
    def _():
        y = acc_ref[...]
        if final_norm:
            ms = jnp.mean(y * y, axis=-1, keepdims=True)
            y = y * lax.rsqrt(ms + NORM_EPS) * g_ref[...]
        o_ref[...] = y


def matmul_res(a, w, res, tk, final_g=None):
    ka, M, kw = a.shape
    N = w.shape[1]
    tm = _pick(M, (512, 256, 128, 8))
    tn = N if final_g is not None else _pick(N, (1024, 512, 256, 128))
    per = kw // tk
    nk = ka * per
    in_specs = [pl.BlockSpec((1, tm, tk), lambda i, j, k: (k // per, i, k % per)),
                pl.BlockSpec((tk, tn), lambda i, j, k: (k, j)),
                pl.BlockSpec((tm, tn), lambda i, j, k: (i, j))]
    args = [a, w, res]
    if final_g is not None:
        in_specs.append(pl.BlockSpec((1, tn), lambda i, j, k: (0, 0)))
        args.append(final_g.reshape(1, N))
    return pl.pallas_call(
        functools.partial(_mm_res_kernel, nk=nk, final_norm=final_g is not None),
        grid=(M // tm, N // tn, nk),
        in_specs=in_specs,
        out_specs=pl.BlockSpec((tm, tn), lambda i, j, k: (i, j)),
        out_shape=jax.ShapeDtypeStruct((M, N), F32),
        scratch_shapes=[pltpu.VMEM((tm, tn), F32)],
        compiler_params=_cparams("parallel", "parallel", "arbitrary"),
        name="matmul_res",
    )(*args)


def _swiglu_up_kernel(x_ref, g_ref, wg_ref, wu_ref, *rest, routed):
    if routed:
        rwh_ref, rwl_ref, rb_ref, h_ref, xn_ref, gate_ref = rest
    else:
        h_ref, xn_ref = rest
    e = pl.program_id(1)

    @pl.when((e == 0) & (pl.program_id(2) == 0))
    def _():
        x = x_ref[...]
        ms = jnp.mean(x * x, axis=-1, keepdims=True)
        xn = x * lax.rsqrt(ms + NORM_EPS) * g_ref[...]
        xn_ref[...] = _bf(xn)
        if routed:
            logits = _dot3(xn, rwh_ref[...], rwl_ref[...]) + rb_ref[...]
            lane = lax.broadcasted_iota(jnp.int32, logits.shape, 1).astype(F32)
            live = lane < N_EXPERTS
            z = jnp.where(live, logits, NEG_INF)
            m1 = jnp.max(z, axis=1, keepdims=True)
            i1 = jnp.min(jnp.where(z == m1, lane, float(LANES)), axis=1, keepdims=True)
            z2 = jnp.where(lane == i1, NEG_INF, z)
            m2 = jnp.max(z2, axis=1, keepdims=True)
            i2 = jnp.min(jnp.where(z2 == m2, lane, float(LANES)), axis=1, keepdims=True)
            e2 = jnp.exp(m2 - m1)
            den = 1.0 + e2
            gate_ref[...] = jnp.where(lane == i1, 1.0 / den, jnp.where(lane == i2, e2 / den, 0.0))

    xn = xn_ref[...]
    a = jnp.dot(xn, wg_ref[0], preferred_element_type=F32)
    b = jnp.dot(xn, wu_ref[0], preferred_element_type=F32)
    h = a * _sigmoid(a) * b
    if routed:
        gates = gate_ref[...]
        lane = lax.broadcasted_iota(jnp.int32, gates.shape, 1)
        h = h * jnp.sum(jnp.where(lane == e, gates, 0.0), axis=1, keepdims=True)
    h_ref[...] = _bf(h)


def swiglu_up(x, g, wg, wu, router=None):
    M, K = x.shape
    E, _, F = wg.shape
    tm = _pick(M, (1024, 512, 256, 128, 8))
    tn = _pick(F, (512, 256, 128))
    per = F // tn
    routed = router is not None
    in_specs = [pl.BlockSpec((tm, K), lambda i, e, j: (i, 0)),
                pl.BlockSpec((1, K), lambda i, e, j: (0, 0)),
                pl.BlockSpec((1, K, tn), lambda i, e, j: (e, 0, j)),
                pl.BlockSpec((1, K, tn), lambda i, e, j: (e, 0, j))]
    args = [x, g.reshape(1, K), wg, wu]
    scratch = [pltpu.VMEM((tm, K), BF16)]
    if routed:
        in_specs += [pl.BlockSpec((K, LANES), lambda i, e, j: (0, 0)),
                     pl.BlockSpec((K, LANES), lambda i, e, j: (0, 0)),
                     pl.BlockSpec((1, LANES), lambda i, e, j: (0, 0))]
        args += list(router)
        scratch.append(pltpu.VMEM((tm, LANES), F32))
    return pl.pallas_call(
        functools.partial(_swiglu_up_kernel, routed=routed),
        grid=(M // tm, E, per),
        in_specs=in_specs,
        out_specs=pl.BlockSpec((tm, tn), lambda i, e, j: (i, e * per + j)),
        out_shape=jax.ShapeDtypeStruct((M, E * F), BF16),
        scratch_shapes=scratch,
        compiler_params=_cparams("parallel", "arbitrary", "arbitrary"),
        name="moe_up" if routed else "ffn_up",
    )(*args)


MOE_RUN = 128
E_PAD = 16
TOP_K = 2


def _moe_cap(TM):
    return TOP_K * TM + N_EXPERTS * MOE_RUN


def _run_sizes(TM):
    top = -(-TM // MOE_RUN) * MOE_RUN
    sizes, s = [], MOE_RUN
    while s <= top:
        sizes.append(s)
        s *= 2
    return tuple(reversed(sizes))


def _moe_route_kernel(x_ref, g_ref, rwh_ref, rwl_ref, rb_ref, xn_ref, rows_ref, cols_ref, meta_ref):
    x = x_ref[...]
    TM = x.shape[0]
    ms = jnp.mean(x * x, axis=-1, keepdims=True)
    xn = x * lax.rsqrt(ms + NORM_EPS) * g_ref[...]
    xn_ref[...] = _bf(xn)
    hi, lo = _split2(xn)
    rwh = rwh_ref[...]
    logits = (lax.dot_general(rwh, hi, NT_DIMS, preferred_element_type=F32)
              + (lax.dot_general(rwh, lo, NT_DIMS, preferred_element_type=F32)
                 + lax.dot_general(rwl_ref[...], hi, NT_DIMS, preferred_element_type=F32))) + rb_ref[...]
    ef = lax.broadcasted_iota(jnp.int32, (E_PAD, TM), 0).astype(F32)
    z = jnp.where(ef < N_EXPERTS, logits, NEG_INF)
    m1 = jnp.max(z, axis=0, keepdims=True)
    i1 = jnp.min(jnp.where(z == m1, ef, float(E_PAD)), axis=0, keepdims=True)
    z2 = jnp.where(ef == i1, NEG_INF, z)
    m2 = jnp.max(z2, axis=0, keepdims=True)
    i2 = jnp.min(jnp.where(z2 == m2, ef, float(E_PAD)), axis=0, keepdims=True)
    e2 = jnp.exp(m2 - m1)
    den = 1.0 + e2
    member = jnp.where((ef == i1) | (ef == i2), 1.0, 0.0)
    before = (lax.broadcasted_iota(jnp.int32, (TM, TM), 0) < lax.broadcasted_iota(jnp.int32, (TM, TM), 1))
    rank = jnp.dot(_bf(member), before.astype(BF16), preferred_element_type=F32)
    cnt = jnp.sum(member, axis=1, keepdims=True)
    run = jnp.floor((cnt + (MOE_RUN - 1)) * (1.0 / MOE_RUN)) * MOE_RUN
    run_b = jnp.broadcast_to(run, (E_PAD, LANES))
    lower = (lax.broadcasted_iota(jnp.int32, (E_PAD, E_PAD), 0)
             > lax.broadcasted_iota(jnp.int32, (E_PAD, E_PAD), 1)).astype(BF16)
    off = jnp.dot(lower, _bf(run_b), preferred_element_type=F32)
    slot = off[:, 0:1] + rank
    slot1 = jnp.sum(jnp.where(ef == i1, slot, 0.0), axis=0, keepdims=True)
    slot2 = jnp.sum(jnp.where(ef == i2, slot, 0.0), axis=0, keepdims=True)
    info = jnp.concatenate([slot1, slot2, 1.0 / den, e2 / den, jnp.zeros((E_PAD - 4, TM), F32)], axis=0)
    rows_ref[0] = info
    ident = (lax.broadcasted_iota(jnp.int32, (E_PAD, LANES), 0)
             == lax.broadcasted_iota(jnp.int32, (E_PAD, LANES), 1)).astype(BF16)
    tn_dims = (((0,), (0,)), ((), ()))
    h3, m3, l3 = _split3(info)
    cols_ref[0] = (lax.dot_general(h3, ident, tn_dims, preferred_element_type=F32)
                   + (lax.dot_general(m3, ident, tn_dims, preferred_element_type=F32)
                      + lax.dot_general(l3, ident, tn_dims, preferred_element_type=F32)))
    lane = lax.broadcasted_iota(jnp.int32, (E_PAD, LANES), 1)
    meta_ref[0] = jnp.where(lane == 0, off, jnp.where(lane == 1, run_b, 0.0))


def moe_route(x, g, router):
    M, D = x.shape
    TM = _pick(M, (1024, 512, 256, 128, 64))
    NT = M // TM
    rwh, rwl, rb = router
    full = lambda a: pl.BlockSpec(a.shape, lambda i: (0,) * a.ndim)
    return pl.pallas_call(
        _moe_route_kernel,
        grid=(NT,),
        in_specs=[pl.BlockSpec((TM, D), lambda i: (i, 0)), pl.BlockSpec((1, D), lambda i: (0, 0)),
                  full(rwh), full(rwl), full(rb)],
        out_specs=[pl.BlockSpec((TM, D), lambda i: (i, 0)),
                   pl.BlockSpec((1, E_PAD, TM), lambda i: (i, 0, 0)),
                   pl.BlockSpec((1, TM, LANES), lambda i: (i, 0, 0)),
                   pl.BlockSpec((1, E_PAD, LANES), lambda i: (i, 0, 0))],
        out_shape=[jax.ShapeDtypeStruct((M, D), BF16),
                   jax.ShapeDtypeStruct((NT, E_PAD, TM), F32),
                   jax.ShapeDtypeStruct((NT, TM, LANES), F32),
                   jax.ShapeDtypeStruct((NT, E_PAD, LANES), F32)],
        compiler_params=_cparams("parallel"),
        name="moe_route",
    )(x, g.reshape(1, D), rwh, rwl, rb)


def _for_each_run_chunk(off_ref, run_ref, e, sizes, body):
    i = pl.program_id(0)
    base = off_ref[i, e]
    n = run_ref[i, e]
    for size in sizes:
        @pl.when((n & size) != 0)
        def _(size=size):
            body(pl.multiple_of(base + (n & (-2 * size)), MOE_RUN), size)


def _moe_up_kernel(off_ref, run_ref, xn_ref, rows_ref, wg_ref, wu_ref, hs_ref, xs_ref, *, sizes, GC):
    j = pl.program_id(1)
    e = pl.program_id(2)
    TM = xn_ref.shape[0]
    CAP = xs_ref.shape[0]

    @pl.when((j == 0) & (e == 0))
    def _():
        s1 = rows_ref[0, 0:1, :]
        s2 = rows_ref[0, 1:2, :]
        xnb = xn_ref[...]
        for c in range(CAP // GC):
            srow = (c * GC + lax.broadcasted_iota(jnp.int32, (GC, TM), 0)).astype(F32)
            onehot = jnp.where((srow == s1) | (srow == s2), 1.0, 0.0)
            xs_ref[c * GC:(c + 1) * GC, :] = _bf(jnp.dot(_bf(onehot), xnb, preferred_element_type=F32))

    @pl.when(e == 0)
    def _():
        hs_ref[...] = jnp.zeros_like(hs_ref)

    def body(start, size):
        xr = xs_ref[pl.ds(start, size), :]
        a = jnp.dot(xr, wg_ref[0], preferred_element_type=F32)
        b = jnp.dot(xr, wu_ref[0], preferred_element_type=F32)
        hs_ref[0, pl.ds(start, size), :] = _bf(a * _sigmoid(a) * b)

    _for_each_run_chunk(off_ref, run_ref, e, sizes, body)


def moe_up_sparse(xn, rows, off, run, wg, wu):
    M, D = xn.shape
    NT, _, TM = rows.shape
    E, _, F = wg.shape
    CAP = _moe_cap(TM)
    tn = _pick(F, (256, 128))
    gs = pltpu.PrefetchScalarGridSpec(
        num_scalar_prefetch=2, grid=(NT, F // tn, E),
        in_specs=[pl.BlockSpec((TM, D), lambda i, j, e, o, r: (i, 0)),
                  pl.BlockSpec((1, E_PAD, TM), lambda i, j, e, o, r: (i, 0, 0)),
                  pl.BlockSpec((1, D, tn), lambda i, j, e, o, r: (e, 0, j)),
                  pl.BlockSpec((1, D, tn), lambda i, j, e, o, r: (e, 0, j))],
        out_specs=pl.BlockSpec((1, CAP, tn), lambda i, j, e, o, r: (i, 0, j)),
        scratch_shapes=[pltpu.VMEM((CAP, D), BF16)])
    return pl.pallas_call(
        functools.partial(_moe_up_kernel, sizes=_run_sizes(TM), GC=_pick(CAP, (512, 256, 128))),
        grid_spec=gs,
        out_shape=jax.ShapeDtypeStruct((NT, CAP, F), BF16),
        compiler_params=_cparams("arbitrary", "arbitrary", "arbitrary"),
        name="moe_up",
    )(off, run, xn, rows, wg, wu)


def _moe_down_kernel(off_ref, run_ref, hs_ref, wd_ref, x_ref, cols_ref, rows_ref, o_ref, ys_ref, *,
                     sizes, GC, n_e, n_kf):
    kf = pl.program_id(2)
    e = pl.program_id(3)
    TM = x_ref.shape[0]
    CAP = ys_ref.shape[0]

    @pl.when((e == 0) & (kf == 0))
    def _():
        ys_ref[...] = jnp.zeros_like(ys_ref)

    def body(start, size):
        ys_ref[pl.ds(start, size), :] += jnp.dot(hs_ref[0, pl.ds(start, size), :], wd_ref[0],
                                                 preferred_element_type=F32)

    _for_each_run_chunk(off_ref, run_ref, e, sizes, body)

    @pl.when((e == n_e - 1) & (kf == n_kf - 1))
    def _():
        info = cols_ref[0]
        s1, s2 = info[:, 0:1], info[:, 1:2]
        r1, r2, p1, p2 = (rows_ref[0, k:k + 1, :] for k in range(4))
        acc = x_ref[...]
        for c in range(CAP // GC):
            srow = (c * GC + lax.broadcasted_iota(jnp.int32, (GC, TM), 0)).astype(F32)
            gate = jnp.sum(jnp.where(srow == r1, p1, 0.0) + jnp.where(srow == r2, p2, 0.0), axis=1, keepdims=True)
            scol = (c * GC + lax.broadcasted_iota(jnp.int32, (TM, GC), 1)).astype(F32)
            onehot = jnp.where((scol == s1) | (scol == s2), 1.0, 0.0)
            acc = acc + jnp.dot(_bf(onehot), _bf(ys_ref[c * GC:(c + 1) * GC, :] * gate), preferred_element_type=F32)
        o_ref[...] = acc


def moe_down_sparse(hs, cols, rows, off, run, wd, x):
    NT, CAP, F = hs.shape
    TM = cols.shape[1]
    E, _, D = wd.shape
    tn = _pick(D, (512, 256, 128))
    tk = _pick(F, (1408, 1024, 512, 256, 128))
    n_kf = F // tk
    gs = pltpu.PrefetchScalarGridSpec(
        num_scalar_prefetch=2, grid=(NT, D // tn, n_kf, E),
        in_specs=[pl.BlockSpec((1, CAP, tk), lambda i, n, k, e, o, r: (i, 0, k)),
                  pl.BlockSpec((1, tk, tn), lambda i, n, k, e, o, r: (e, k, n)),
                  pl.BlockSpec((TM, tn), lambda i, n, k, e, o, r: (i, n)),
                  pl.BlockSpec((1, TM, LANES), lambda i, n, k, e, o, r: (i, 0, 0)),
                  pl.BlockSpec((1, E_PAD, TM), lambda i, n, k, e, o, r: (i, 0, 0))],
        out_specs=pl.BlockSpec((TM, tn), lambda i, n, k, e, o, r: (i, n)),
        scratch_shapes=[pltpu.VMEM((CAP, tn), F32)])
    return pl.pallas_call(
        functools.partial(_moe_down_kernel, sizes=_run_sizes(TM), GC=_pick(CAP, (512, 256, 128)), n_e=E, n_kf=n_kf),
        grid_spec=gs,
        out_shape=jax.ShapeDtypeStruct(x.shape, F32),
        compiler_params=_cparams("arbitrary", "arbitrary", "arbitrary", "arbitrary"),
        name="moe_down",
    )(off, run, hs, wd, x, cols, rows)


def _rmsnorm_kernel(x_ref, g_ref, o_ref):
    x = x_ref[...]
    ms = jnp.mean(x * x, axis=-1, keepdims=True)
    o_ref[...] = x * lax.rsqrt(ms + NORM_EPS) * g_ref[...]


def rmsnorm(x, g):
    M, D = x.shape
    tm = _pick(M, (512, 256, 128, 8))
    return pl.pallas_call(
        _rmsnorm_kernel,
        grid=(M // tm,),
        in_specs=[pl.BlockSpec((tm, D), lambda i: (i, 0)), pl.BlockSpec((1, D), lambda i: (0, 0))],
        out_specs=pl.BlockSpec((tm, D), lambda i: (i, 0)),
        out_shape=jax.ShapeDtypeStruct((M, D), F32),
        compiler_params=_cparams("parallel"),
        name="final_norm",
    )(x, g.reshape(1, D))


def _rwkv_prep_kernel(p_ref, prev_ref, s0_ref, mu_ref, w0_ref, w2h_ref, w2l_ref, a0_ref, a2h_ref, a2l_ref,
                      g2h_ref, g2l_ref, kk_ref, ka_ref, bd_ref,
                      r_ref, lw_ref, k_ref, v_ref, kn_ref, b_ref, g_ref, *, T):
    p = p_ref[0]
    tm = p.shape[0]
    row = lax.broadcasted_iota(jnp.int32, (tm, 1), 0)
    shifted = jnp.where(row == 0, prev_ref[0, 7:8, :], pltpu.roll(p, 1, 0))
    nseq = s0_ref.shape[0]
    s0_rows = jnp.broadcast_to(s0_ref[...], (nseq, tm // nseq, p.shape[1])).reshape(tm, p.shape[1])
    first = ((pl.program_id(0) * tm + row) % T) == 0
    ps = p + (jnp.where(first, s0_rows, shifted) - p) * mu_ref[...]
    W = A_WIDTH
    r = ps[:, 0:W]
    k = ps[:, W:2 * W]
    v = ps[:, 2 * W:3 * W]
    o = 3 * W
    wd = ps[:, o:o + A_LORA_W]
    ad = ps[:, o + A_LORA_W:o + A_LORA_W + A_LORA_A]
    gd = ps[:, o + A_LORA_W + A_LORA_A:]
    z = -(w0_ref[...] + _dot3(jnp.tanh(wd), w2h_ref[...], w2l_ref[...]))
    softplus = jnp.maximum(z, 0.0) + jnp.log(1.0 + jnp.exp(-jnp.abs(z)))
    lw_ref[...] = -jnp.exp(-softplus - 0.5)
    a = _sigmoid(a0_ref[...] + _dot3(ad, a2h_ref[...], a2l_ref[...]))
    g_ref[...] = _dot3(_sigmoid(gd), g2h_ref[...], g2l_ref[...])
    kk = k * kk_ref[...]
    ss = _dot_exact_rhs(kk * kk, bd_ref[...])
    kn = kk / jnp.maximum(jnp.sqrt(ss), 1e-12)
    r_ref[...] = r
    v_ref[...] = v
    kn_ref[...] = kn
    b_ref[...] = kn * a
    k_ref[...] = k * (1.0 + (a - 1.0) * ka_ref[...])


def rwkv_prep(pab, shift0, prm, T):
    M = pab.shape[1]
    tm = _pick(M, (256, 128, 8))
    assert T % tm == 0 or tm % T == 0
    nseq = max(tm // T, 1)
    s0_map = (lambda i: (i, 0, 0)) if tm >= T else (lambda i: ((i * tm) // T, 0, 0))
    W = A_WIDTH
    row = lambda n: pl.BlockSpec((1, n), lambda i: (0, 0))
    mat = lambda a, b: pl.BlockSpec((a, b), lambda i: (0, 0))
    out = pl.BlockSpec((tm, W), lambda i: (i, 0))
    return pl.pallas_call(
        functools.partial(_rwkv_prep_kernel, T=T),
        grid=(M // tm,),
        in_specs=[pl.BlockSpec((1, tm, A_PROJ), lambda i: (0, i, 0)),
                  pl.BlockSpec((1, 8, A_PROJ), lambda i: (0, jnp.maximum(i * (tm // 8) - 1, 0), 0)),
                  pl.BlockSpec((nseq, 1, A_PROJ), s0_map),
                  row(A_PROJ), row(W), mat(A_LORA_W, W), mat(A_LORA_W, W),
                  row(W), mat(A_LORA_A, W), mat(A_LORA_A, W),
                  mat(A_LORA_G, W), mat(A_LORA_G, W), row(W), row(W), mat(W, W)],
        out_specs=[out] * 7,
        out_shape=[jax.ShapeDtypeStruct((M, W), F32)] * 7,
        compiler_params=_cparams("parallel"),
        name="rwkv_prep",
    )(pab, pab, shift0[:, None, :], prm["mu"], prm["w0"], *prm["w2"], prm["a0"], *prm["a2"], *prm["g2"],
      prm["k_k"], prm["k_a"], prm["bd"])


def _rwkv_chunk_kernel(r_ref, lw_ref, k_ref, v_ref, kn_ref, b_ref, s0_ref, y_ref, s_ref, *, C, nc, Bb):
    C2 = 2 * C
    lane = lax.broadcasted_iota(jnp.int32, (C2, LANES), 1)
    rowi = lax.broadcasted_iota(jnp.int32, (C2, LANES), 0)
    mask2 = ((rowi >= C) == (lane >= HEAD_DIM)).astype(F32)
    ri = lax.broadcasted_iota(jnp.int32, (C2, C2), 0)
    ci = lax.broadcasted_iota(jnp.int32, (C2, C2), 1)
    same = (ri >= C) == (ci >= C)
    strict = same & (ri > ci)
    incl = same & (ri >= ci)
    eye = (ri == ci).astype(F32)
    ti = lax.broadcasted_iota(jnp.int32, (C, C), 0)
    tj = lax.broadcasted_iota(jnp.int32, (C, C), 1)
    tri = (ti >= tj).astype(BF16)
    n_levels = max(C.bit_length() - 2, 0)
    merged = C2 % LANES == 0
    streams = range(Bb)

    def stack(x):
        return jnp.concatenate([x, x], axis=0) * mask2

    def each(f, *lists):
        return [f(*[l[i] for l in lists]) for i in streams]

    @pl.when(pl.program_id(2) == 0)
    def _():
        s_ref[...] = s0_ref[...]

    def chunk(c, states):
        sl = pl.ds(pl.multiple_of(c * C, C), C)
        S = list(states)
        load = lambda ref: [ref[i, sl, :] for i in streams]
        r, lw, k, v, kn, b = (load(ref) for ref in (r_ref, lw_ref, k_ref, v_ref, kn_ref, b_ref))
        cs = each(lambda x: _dot_exact_lhs3(tri, x), lw)
        gt = each(lambda x: x[C - 1:C, :], cs)
        e_neg = each(lambda x: jnp.exp(-x), cs)
        e_rem = each(lambda g, x: jnp.exp(g - x), gt, cs)
        KT = each(lambda a, x, l: stack(a * jnp.exp(x - l)), kn, cs, lw)
        BI = each(lambda a, e: stack(a * e), b, e_neg)
        KI = each(lambda a, e: stack(a * e), k, e_neg)
        RT = each(lambda a, x: stack(a * jnp.exp(x)), r, cs)
        V2 = each(stack, v)
        KG = each(lambda a, e: stack(a * e), k, e_rem)
        BG = each(lambda a, e: stack(a * e), b, e_rem)
        if merged:
            quad = each(lambda kt, rt, bi, ki: _dot_nt(jnp.concatenate([kt, rt], axis=0),
                                                        jnp.concatenate([bi, ki], axis=0)), KT, RT, BI, KI)
            a_kb = each(lambda q: q[:C2, :C2], quad)
            a_kv = each(lambda q: q[:C2, C2:], quad)
            a_rb = each(lambda q: q[C2:, :C2], quad)
            a_rk = each(lambda q: q[C2:, C2:], quad)
        else:
            a_kb = each(_dot_nt, KT, BI)
            a_kv = each(_dot_nt, KT, KI)
            a_rb = each(_dot_nt, RT, BI)
            a_rk = each(_dot_nt, RT, KI)
        a_rb = each(lambda x: jnp.where(incl, x, 0.0), a_rb)
        a_rk = each(lambda x: jnp.where(incl, x, 0.0), a_rk)
        pw = each(lambda x: -jnp.where(strict, x, 0.0), a_kb)
        tinv = each(lambda x: eye + x, pw)
        if n_levels:
            pw = each(_dot, pw, pw)
            for _ in range(n_levels - 1):
                both = each(lambda p, t: _dot(jnp.concatenate([p, t], axis=0), p), pw, tinv)
                pw = each(lambda x: x[:C2], both)
                tinv = each(lambda t, x: t + x[C2:], tinv, both)
            tinv = each(lambda t, p: t + _dot(t, p), tinv, pw)
        av = each(lambda x, vv: _dot(jnp.where(strict, x, 0.0), vv), a_kv, V2)
        kpw = each(lambda t, kt, a: _dot(t, jnp.concatenate([kt, a], axis=1)), tinv, KT, av)
        corr = each(_dot, a_rb, kpw)
        rp = each(lambda x, c_: x - c_[:, :LANES], RT, corr)
        y1 = each(lambda a, vv, c_: _dot(a, vv) - c_[:, LANES:], a_rk, V2, corr)
        low = each(_dot_tn, kpw, BG)
        mlow = each(lambda x: x[:LANES], low)
        nt = each(lambda vv, kg, x: _dot_tn(vv, kg) - x[LANES:], V2, KG, low)
        y2 = each(lambda p, s, y: _dot_nt(p, s) + y, rp, S, y1)
        for i in streams:
            y_ref[i, sl, :] = y2[i][:C] + y2[i][C:]
        new = each(lambda s, g, m, n: s * jnp.exp(g) - _dot3(s, *_split2(m)) + n, S, gt, mlow, nt)
        return tuple(new)

    final = lax.fori_loop(0, nc, chunk, tuple(s_ref[i, 0] for i in streams))
    for i in streams:
        s_ref[i, 0] = final[i]


def rwkv_chunk(r, lw, k, v, kn, b, s_bd, B, T):
    C = min(T, 64)
    Tt = _pick(T, (512, 256, 128, 64)) if T > C else T
    nc = Tt // C
    Bb = _pick(B, (4, 2, 1)) if T > C else _pick(B, (8, 4, 2, 1))
    HP = A_WIDTH // LANES
    seq = pl.BlockSpec((Bb, Tt, LANES), lambda i, h, t: (i, t, h))
    st = pl.BlockSpec((Bb, 1, LANES, LANES), lambda i, h, t: (i, h, 0, 0))
    r3 = lambda x: x.reshape(B, T, A_WIDTH)
    y, s = pl.pallas_call(
        functools.partial(_rwkv_chunk_kernel, C=C, nc=nc, Bb=Bb),
        grid=(B // Bb, HP, T // Tt),
        in_specs=[seq] * 6 + [st],
        out_specs=[seq, st],
        out_shape=[jax.ShapeDtypeStruct((B, T, A_WIDTH), F32),
                   jax.ShapeDtypeStruct((B, HP, LANES, LANES), F32)],
        compiler_params=_cparams("parallel", "parallel", "arbitrary"),
        name="rwkv_chunk",
    )(r3(r), r3(lw), r3(k), r3(v), r3(kn), r3(b), s_bd)
    return y.reshape(B * T, A_WIDTH), s


def _rwkv_post_kernel(y_ref, r_ref, k_ref, v_ref, g_ref, rk_ref, lnw_ref, lnb_ref, bd_ref, o_ref):
    y = y_ref[...]
    bd = bd_ref[...]
    inv_n = 1.0 / HEAD_DIM
    mean = _dot_exact_rhs(y, bd) * inv_n
    d = y - mean
    var = _dot_exact_rhs(d * d, bd) * inv_n
    yn = d * lax.rsqrt(var + RWKV_GN_EPS) * lnw_ref[...] + lnb_ref[...]
    bonus = _dot_exact_rhs(r_ref[...] * k_ref[...] * rk_ref[...], bd)
    o_ref[...] = _bf((yn + bonus * v_ref[...]) * g_ref[...])


def rwkv_post(y, r, k, v, g, prm):
    M, W = y.shape
    tm = _pick(M, (256, 128, 8))
    blk = pl.BlockSpec((tm, W), lambda i: (i, 0))
    row = pl.BlockSpec((1, W), lambda i: (0, 0))
    return pl.pallas_call(
        _rwkv_post_kernel,
        grid=(M // tm,),
        in_specs=[blk] * 5 + [row] * 3 + [pl.BlockSpec((W, W), lambda i: (0, 0))],
        out_specs=blk,
        out_shape=jax.ShapeDtypeStruct((M, W), BF16),
        compiler_params=_cparams("parallel"),
        name="rwkv_post",
    )(y, r, k, v, g, prm["r_k"], prm["ln_w"], prm["ln_b"], prm["bd"])


def _logf_kernel(f_ref, b_ref, o_ref):
    z = f_ref[0] + b_ref[...]
    o_ref[...] = jnp.minimum(z, 0.0) - jnp.log(1.0 + jnp.exp(-jnp.abs(z)))


def fox_logf(pab, bias_row, col_block):
    M = pab.shape[1]
    tm = _pick(M, (1024, 512, 256, 128, 8))
    return pl.pallas_call(
        _logf_kernel,
        grid=(M // tm,),
        in_specs=[pl.BlockSpec((1, tm, LANES), lambda i: (1, i, col_block)),
                  pl.BlockSpec((1, LANES), lambda i: (0, 0))],
        out_specs=pl.BlockSpec((tm, LANES), lambda i: (i, 0)),
        out_shape=jax.ShapeDtypeStruct((M, LANES), F32),
        compiler_params=_cparams("parallel"),
        name="fox_logf",
    )(pab, bias_row)


def _cumsum_kernel(x_ref, o_ref, carry_ref):
    @pl.when(pl.program_id(1) == 0)
    def _():
        carry_ref[...] = jnp.zeros_like(carry_ref)

    ti = lax.broadcasted_iota(jnp.int32, (LANES, LANES), 0)
    tj = lax.broadcasted_iota(jnp.int32, (LANES, LANES), 1)
    triu = (ti <= tj).astype(BF16)
    hi, mid, lo = _split3(x_ref[0])
    cs = (jnp.dot(hi, triu, preferred_element_type=F32)
          + (jnp.dot(mid, triu, preferred_element_type=F32)
             + jnp.dot(lo, triu, preferred_element_type=F32))) + carry_ref[...]
    o_ref[0] = cs
    carry_ref[...] = jnp.broadcast_to(cs[:, LANES - 1:LANES], cs.shape)


def cumsum_lanes(xT):
    B, H, L = xT.shape
    return pl.pallas_call(
        _cumsum_kernel,
        grid=(B, L // LANES),
        in_specs=[pl.BlockSpec((1, H, LANES), lambda b, p: (b, 0, p))],
        out_specs=pl.BlockSpec((1, H, LANES), lambda b, p: (b, 0, p)),
        out_shape=jax.ShapeDtypeStruct((B, H, L), F32),
        scratch_shapes=[pltpu.VMEM((H, LANES), F32)],
        compiler_params=_cparams("parallel", "arbitrary"),
        name="fox_cumsum",
    )(xT)


def _head_lane_mask(n_rows, width, head):
    lane = lax.broadcasted_iota(jnp.int32, (n_rows, width), 1)
    return (lane // HEAD_DIM) == head


def _two_pass_attend(qs, n_past, add_past, diag_start, add_diag, kb_ref, vb_ref, s_ref, acc_ref, l_ref, m_ref):
    blk = MOBA_BLOCK
    rep = blk // LANES
    kd = kb_ref[pl.ds(diag_start, blk), :]
    s_d = add_diag(lax.dot_general(qs, kd, NT_DIMS, preferred_element_type=F32))
    s_ref[s_ref.shape[0] - 1] = s_d
    l_ref[...] = s_d

    def pass1(n, carry):
        kb = kb_ref[pl.ds(pl.multiple_of(n * blk, blk), blk), :]
        s = add_past(n, lax.dot_general(qs, kb, NT_DIMS, preferred_element_type=F32))
        s_ref[n] = s
        l_ref[...] = jnp.maximum(l_ref[...], s)
        return carry

    lax.fori_loop(0, n_past, pass1, 0)
    m_ref[...] = jnp.broadcast_to(jnp.max(l_ref[...], axis=1, keepdims=True), m_ref.shape)

    def probs(n):
        return jnp.exp2(s_ref[n] - jnp.concatenate([m_ref[...]] * rep, axis=1))

    p_d = probs(s_ref.shape[0] - 1)
    l_ref[...] = p_d
    acc_ref[...] = jnp.dot(_bf(p_d), vb_ref[pl.ds(diag_start, blk), :], preferred_element_type=F32)

    def pass2(n, carry):
        p = probs(n)
        l_ref[...] += p
        acc_ref[...] += jnp.dot(_bf(p), vb_ref[pl.ds(pl.multiple_of(n * blk, blk), blk), :],
                                preferred_element_type=F32)
        return carry

    lax.fori_loop(0, n_past, pass2, 0)
    return acc_ref[...] / jnp.sum(l_ref[...], axis=1, keepdims=True)


def _attend_scratch(n_blocks, R):
    return [pltpu.VMEM((n_blocks + 1, R, MOBA_BLOCK), F32),
            pltpu.VMEM((R, GROUP_LANES), F32),
            pltpu.VMEM((R, MOBA_BLOCK), F32),
            pltpu.VMEM((R, LANES), F32)]


def _fox_prompt_kernel(q_ref, k_ref, v_ref, c_ref, o_ref, kb_ref, vb_ref, s_ref, acc_ref, l_ref, m_ref, *, tq, hpg):
    g = pl.program_id(1)
    qi = pl.program_id(2)

    @pl.when(qi == 0)
    def _():
        kb_ref[...] = _bf(k_ref[0])
        vb_ref[...] = _bf(v_ref[0])

    q = q_ref[0] * (HEAD_DIM ** -0.5 * LOG2E)
    qs = _bf(jnp.concatenate([jnp.where(_head_lane_mask(tq, GROUP_LANES, h), q, 0.0) for h in range(hpg)], axis=0))
    causal = lax.broadcasted_iota(jnp.int32, (tq, tq), 0) >= lax.broadcasted_iota(jnp.int32, (tq, tq), 1)

    def add_bias(n, s, mask):
        parts = []
        for h in range(hpg):
            c_row = c_ref[0, pl.ds(g * hpg + h, 1), pl.ds(pl.multiple_of(n * tq, tq), tq)] * LOG2E
            sh = s[h * tq:(h + 1) * tq, :] - c_row
            parts.append(jnp.where(causal, sh, NEG_INF) if mask else sh)
        return jnp.concatenate(parts, axis=0)

    o = _two_pass_attend(qs, qi, lambda n, s: add_bias(n, s, False), pl.multiple_of(qi * tq, tq),
                         lambda s: add_bias(qi, s, True), kb_ref, vb_ref, s_ref, acc_ref, l_ref, m_ref)
    out = jnp.zeros((tq, GROUP_LANES), F32)
    for h in range(hpg):
        out = out + jnp.where(_head_lane_mask(tq, GROUP_LANES, h), o[h * tq:(h + 1) * tq, :], 0.0)
    o_ref[...] = _bf(out)


def fox_prompt(pab, cT, B, T):
    tq = MOBA_BLOCK
    nq = T // tq
    hpg = GROUP_LANES // HEAD_DIM
    G = B_WIDTH // GROUP_LANES
    return pl.pallas_call(
        functools.partial(_fox_prompt_kernel, tq=tq, hpg=hpg),
        grid=(B, G, nq),
        in_specs=[pl.BlockSpec((1, tq, GROUP_LANES), lambda b, g, qi: (1, b * nq + qi, g)),
                  pl.BlockSpec((1, T, GROUP_LANES), lambda b, g, qi: (1, b, G + g)),
                  pl.BlockSpec((1, T, GROUP_LANES), lambda b, g, qi: (1, b, 2 * G + g)),
                  pl.BlockSpec((1, B_HEADS, T), lambda b, g, qi: (b, 0, 0))],
        out_specs=pl.BlockSpec((tq, GROUP_LANES), lambda b, g, qi: (b * nq + qi, g)),
        out_shape=jax.ShapeDtypeStruct((B * T, B_WIDTH), BF16),
        scratch_shapes=[pltpu.VMEM((T, GROUP_LANES), BF16),
                        pltpu.VMEM((T, GROUP_LANES), BF16)] + _attend_scratch(nq - 1, hpg * tq),
        compiler_params=_cparams("parallel", "parallel", "arbitrary"),
        name="fox_prompt",
    )(pab, pab, pab, cT)


def _moba_slopes(n_rows, rows_per_head, i, kv_base):
    c = lax.broadcasted_iota(jnp.int32, (n_rows, 1), 0) // rows_per_head
    head = C_REP * (kv_base + c) + i
    return jnp.exp2(-8.0 * (head + 1).astype(F32) / C_HEADS)


def _top_blocks(z, live, idxf, axis=1):
    sel = jnp.zeros(z.shape, F32)
    for _ in range(MOBA_TOPK):
        m = jnp.max(z, axis=axis, keepdims=True)
        idx = jnp.min(jnp.where((z == m) & live, idxf, float(LANES)), axis=axis, keepdims=True)
        pick = idxf == idx
        sel = jnp.where(pick, 1.0, sel)
        z = jnp.where(pick, NEG_INF, z)
    return sel


def _moba_prompt_kernel(q0_ref, q1_ref, q2_ref, q3_ref, k_ref, v_ref, o_ref,
                        kb_ref, vb_ref, km_ref, s_ref, acc_ref, l_ref, m_ref, *, tq, nb, cpg):
    g = pl.program_id(1)
    qi = pl.program_id(2)
    R = cpg * tq
    blk = MOBA_BLOCK

    @pl.when(qi == 0)
    def _():
        km_ref[...] = jnp.zeros_like(km_ref)
        for n in range(nb):
            kblk = k_ref[0, n * blk:(n + 1) * blk, :]
            kb_ref[n * blk:(n + 1) * blk, :] = _bf(kblk)
            vb_ref[n * blk:(n + 1) * blk, :] = _bf(v_ref[0, n * blk:(n + 1) * blk, :])
            km_ref[n:n + 1, :] = jnp.sum(kblk, axis=0, keepdims=True) * (1.0 / blk)

    own = (qi * tq) // blk
    row_tok = lax.broadcasted_iota(jnp.int32, (R, 1), 0) % tq
    qpos = (qi * tq + row_tok).astype(F32)
    koff = lax.broadcasted_iota(jnp.int32, (R, blk), 1)
    causal = (qi * tq - own * blk + row_tok) >= koff
    blockf = lax.broadcasted_iota(jnp.int32, (NB_PAD, R), 0).astype(F32)
    past = blockf < own
    km = km_ref[0:NB_PAD, :]
    spread_row = lax.broadcasted_iota(jnp.int32, (NB_PAD, LANES), 0)
    for i, q_ref in enumerate((q0_ref, q1_ref, q2_ref, q3_ref)):
        q = q_ref[0] * (HEAD_DIM ** -0.5 * LOG2E)
        qs = jnp.concatenate([jnp.where(_head_lane_mask(tq, GROUP_LANES, c), q, 0.0) for c in range(cpg)], axis=0)
        gate = _dot3_nt(km, *_split2(qs))
        sel_bf = _bf(_top_blocks(jnp.where(past, gate, NEG_INF), past, blockf, axis=0))
        slope = _moba_slopes(R, tq, i, g * cpg) * LOG2E
        b0 = slope * koff.astype(F32)
        slope_rep = jnp.broadcast_to(slope, (R, LANES))
        sq_rep = slope_rep * qpos
        own_term = b0 + slope * ((own * blk).astype(F32) - qpos)

        def add_past(n, s, b0=b0, sel_bf=sel_bf, slope_rep=slope_rep, sq_rep=sq_rep):
            chosen = _dot_tn(sel_bf, (spread_row == n).astype(BF16)) > 0.5
            rt = jnp.where(chosen, slope_rep * (n * blk).astype(F32) - sq_rep, NEG_INF)
            return s + b0 + jnp.concatenate([rt] * (blk // LANES), axis=1)

        def add_own(s, own_term=own_term):
            return jnp.where(causal, s + own_term, NEG_INF)

        o = _two_pass_attend(_bf(qs), own, add_past, pl.multiple_of(own * blk, blk), add_own,
                             kb_ref, vb_ref, s_ref, acc_ref, l_ref, m_ref)
        out = jnp.zeros((tq, GROUP_LANES), F32)
        for c in range(cpg):
            out = out + jnp.where(_head_lane_mask(tq, GROUP_LANES, c), o[c * tq:(c + 1) * tq, :], 0.0)
        o_ref[i] = _bf(out)


def moba_prompt(p1, B, T):
    tq = MOBA_BLOCK
    nq = T // tq
    nb = T // MOBA_BLOCK
    assert nb <= NB_PAD
    cpg = GROUP_LANES // HEAD_DIM
    G = C_KV_WIDTH // GROUP_LANES
    R = cpg * tq
    qspec = lambda i: pl.BlockSpec((1, tq, GROUP_LANES), lambda b, g, qi, i=i: (i, b * nq + qi, g))
    return pl.pallas_call(
        functools.partial(_moba_prompt_kernel, tq=tq, nb=nb, cpg=cpg),
        grid=(B, G, nq),
        in_specs=[qspec(0), qspec(1), qspec(2), qspec(3),
                  pl.BlockSpec((1, T, GROUP_LANES), lambda b, g, qi: (4, b, g)),
                  pl.BlockSpec((1, T, GROUP_LANES), lambda b, g, qi: (5, b, g))],
        out_specs=pl.BlockSpec((C_REP, tq, GROUP_LANES), lambda b, g, qi: (0, b * nq + qi, g)),
        out_shape=jax.ShapeDtypeStruct((C_REP, B * T, C_KV_WIDTH), BF16),
        scratch_shapes=[pltpu.VMEM((T, GROUP_LANES), BF16),
                        pltpu.VMEM((T, GROUP_LANES), BF16),
                        pltpu.VMEM((LANES, GROUP_LANES), F32)] + _attend_scratch(nb - 1, R),
        compiler_params=_cparams("parallel", "parallel", "arbitrary"),
        name="moba_prompt",
    )(p1, p1, p1, p1, p1, p1)


def _flat_online_update(s_list, v_list, m_ref, l_ref, acc_ref):
    m_old = m_ref[...]
    m_new = m_old
    for s in s_list:
        m_new = jnp.maximum(m_new, jnp.max(s, axis=1, keepdims=True))
    alpha = jnp.exp(m_old - m_new)
    l_new = alpha * l_ref[...]
    acc = alpha * acc_ref[...]
    for s, v in zip(s_list, v_list):
        p = jnp.exp(s - m_new)
        l_new = l_new + jnp.sum(p, axis=1, keepdims=True)
        acc = acc + jnp.dot(_bf(p), v, preferred_element_type=F32)
    m_ref[...] = m_new
    l_ref[...] = l_new
    acc_ref[...] = acc


def _flat_cumsum(x, carry, heads):
    rows = x.shape[0]
    lane = lax.broadcasted_iota(jnp.int32, x.shape, 1)
    d = heads
    while d < LANES:
        x = x + jnp.where(lane >= d, pltpu.roll(x, d, 1), 0.0)
        d *= 2
    tot = jnp.where(lane >= LANES - heads, x, 0.0)
    d = heads
    while d < LANES:
        tot = tot + pltpu.roll(tot, LANES - d, 1)
        d *= 2
    if rows == 1:
        return x + carry, carry + tot
    ri = lax.broadcasted_iota(jnp.int32, (rows, rows), 0)
    ci = lax.broadcasted_iota(jnp.int32, (rows, rows), 1)
    ex = _dot_exact_lhs3((ri > ci).astype(BF16), tot)
    c = x + ex + carry
    new_carry = carry + ex[rows - 1:rows, :] + tot[rows - 1:rows, :]
    return c, new_carry


def _rows_to_lanes(c):
    return jnp.concatenate([c[r:r + 1, :] for r in range(c.shape[0])], axis=1)


def _fox_sample_kernel(pt_ref, q_ref, kn_ref, vn_ref, lfn_ref, *rest, T, NP, n_steps):
    k_refs = rest[0:NP]
    v_refs = rest[NP:2 * NP]
    lf_refs = rest[2 * NP:3 * NP]
    o_ref, m_ref, l_ref, acc_ref, carry_ref = rest[3 * NP:]
    s_id = pl.program_id(1)
    H = B_HEADS
    R = H * T

    @pl.when(s_id == 0)
    def _():
        m_ref[...] = jnp.full_like(m_ref, NEG_INF)
        l_ref[...] = jnp.zeros_like(l_ref)
        acc_ref[...] = jnp.zeros_like(acc_ref)
        carry_ref[...] = jnp.zeros_like(carry_ref)

    qb = _bf(q_ref[0] * (HEAD_DIM ** -0.5))
    row_head = lax.broadcasted_iota(jnp.int32, (R, LANES), 0) // T
    lane = lax.broadcasted_iota(jnp.int32, (R, LANES), 1)
    same_head = (lane % H) == row_head
    n_rows = k_refs[0].shape[1] * H
    head_mask = jnp.concatenate([jnp.where(same_head, 0.0, NEG_INF)] * (n_rows // LANES), axis=1)

    carry = carry_ref[...]
    s_list, v_list = [], []
    for j in range(NP):
        c, carry = _flat_cumsum(lf_refs[j][0], carry, H)
        s = _dot_nt(qb, k_refs[j][0].reshape(n_rows, HEAD_DIM))
        s_list.append(s - _rows_to_lanes(c) + head_mask)
        v_list.append(_bf(v_refs[j][0].reshape(n_rows, HEAD_DIM)))
    carry_ref[...] = carry

    @pl.when(s_id < n_steps - 1)
    def _():
        _flat_online_update(s_list, v_list, m_ref, l_ref, acc_ref)

    @pl.when(s_id == n_steps - 1)
    def _():
        c_new, _ = _flat_cumsum(lfn_ref[0], carry, H)
        row_tok = lax.broadcasted_iota(jnp.int32, (R, LANES), 0) % T
        ok = same_head & ((lane // H) <= row_tok)
        s_new = jnp.where(ok, _dot_nt(qb, kn_ref[0]) - c_new, NEG_INF)
        _flat_online_update(s_list + [s_new], v_list + [_bf(vn_ref[0])], m_ref, l_ref, acc_ref)
        o_ref[0] = acc_ref[...] / l_ref[...]


def fox_sample(q_flat, kn_flat, vn_flat, lfn_flat, cache_k, cache_v, cache_lf, page_table):
    B, n_pages = page_table.shape
    R = q_flat.shape[1]
    T = R // B_HEADS
    page_shape = cache_k.shape[1:]
    NP = 4
    n_steps = n_pages // NP
    page = lambda j: (lambda b, s, pt: (pt[b, s * NP + j], 0, 0))
    page4 = lambda j: (lambda b, s, pt: (pt[b, s * NP + j], 0, 0, 0))
    per_b = lambda b, s, pt: (b, 0, 0)
    in_specs = [pl.BlockSpec((1, R, HEAD_DIM), per_b),
                pl.BlockSpec((1, T * B_HEADS, HEAD_DIM), per_b),
                pl.BlockSpec((1, T * B_HEADS, HEAD_DIM), per_b),
                pl.BlockSpec((1, 1, LANES), per_b)]
    in_specs += [pl.BlockSpec((1, *page_shape), page4(j)) for j in range(NP)] * 2
    in_specs += [pl.BlockSpec((1, *cache_lf.shape[1:]), page(j)) for j in range(NP)]
    gs = pltpu.PrefetchScalarGridSpec(
        num_scalar_prefetch=1, grid=(B, n_steps), in_specs=in_specs,
        out_specs=pl.BlockSpec((1, R, HEAD_DIM), per_b),
        scratch_shapes=[pltpu.VMEM((R, 1), F32), pltpu.VMEM((R, 1), F32),
                        pltpu.VMEM((R, HEAD_DIM), F32), pltpu.VMEM((1, LANES), F32)])
    return pl.pallas_call(
        functools.partial(_fox_sample_kernel, T=T, NP=NP, n_steps=n_steps),
        grid_spec=gs,
        out_shape=jax.ShapeDtypeStruct((B, R, HEAD_DIM), F32),
        compiler_params=_cparams("parallel", "arbitrary"),
        name="fox_sample",
    )(page_table, q_flat, kn_flat, vn_flat, lfn_flat, *([cache_k] * NP), *([cache_v] * NP), *([cache_lf] * NP))


def _moba_sample_kernel(pt_ref, q_ref, kn_ref, vn_ref, *rest, T, n_pages, q_start):
    k_refs = rest[0:n_pages]
    v_refs = rest[n_pages:2 * n_pages]
    o_ref, m_ref, l_ref, acc_ref = rest[2 * n_pages:]
    Hkv = C_KV_HEADS
    R = C_HEADS * T
    rows_per_kv = C_REP * T
    page = k_refs[0].shape[1]
    page_rows = page * Hkv
    ppb = MOBA_BLOCK // page
    nbp = n_pages // ppb
    own = q_start // MOBA_BLOCK
    blk_cols = MOBA_BLOCK * Hkv
    per_row = LANES // Hkv

    q = q_ref[0] * (HEAD_DIM ** -0.5)
    qb = _bf(q)
    rowi = lax.broadcasted_iota(jnp.int32, (R, 1), 0)
    row_kv = rowi // rows_per_kv
    row_sub = (rowi // T) % C_REP
    row_tok = rowi % T
    slope = jnp.exp2(-8.0 * (C_REP * row_kv + row_sub + 1).astype(F32) / C_HEADS)
    qpos = (q_start + row_tok).astype(F32)
    lane = lax.broadcasted_iota(jnp.int32, (R, LANES), 1)
    same_kv = (lane % Hkv) == row_kv

    sums = []
    for n in range(nbp):
        acc = jnp.zeros((Hkv, HEAD_DIM), F32)
        for j in range(ppb):
            acc = acc + jnp.sum(k_refs[n * ppb + j][0], axis=0)
        sums.append(acc * (1.0 / MOBA_BLOCK))
    km = jnp.concatenate(sums + [jnp.zeros((LANES - nbp * Hkv, HEAD_DIM), F32)], axis=0)
    gate = _dot3_nt(q, *_split2(km))
    live = same_kv & (lane < nbp * Hkv) & ((lane // Hkv) < own)
    sel = _top_blocks(jnp.where(live, gate, NEG_INF), live, lane.astype(F32))

    m_ref[...] = jnp.full_like(m_ref, NEG_INF)
    l_ref[...] = jnp.zeros_like(l_ref)
    acc_ref[...] = jnp.zeros_like(acc_ref)

    tok_new = lane // Hkv
    ok = same_kv & (tok_new <= row_tok) & (tok_new < T)
    dist = (row_tok - tok_new).astype(F32)
    s_new = jnp.where(ok, _dot_nt(qb, kn_ref[0]) - slope * dist, NEG_INF)
    _flat_online_update([s_new], [_bf(vn_ref[0])], m_ref, l_ref, acc_ref)

    col_tok = jnp.concatenate([(lane // Hkv + r * per_row) for r in range(blk_cols // LANES)], axis=1).astype(F32)
    col_term = jnp.concatenate([jnp.where(same_kv, 0.0, NEG_INF)] * (blk_cols // LANES), axis=1) + slope * col_tok
    for n in range(nbp):
        chosen = jnp.max(jnp.where((lane // Hkv) == n, sel, 0.0), axis=1, keepdims=True) > 0.5
        row_term = jnp.where(chosen, slope * (n * MOBA_BLOCK - qpos), NEG_INF)
        flat = lambda ref: _bf(ref[0].reshape(page_rows, HEAD_DIM))
        kb = jnp.concatenate([flat(k_refs[n * ppb + j]) for j in range(ppb)], axis=0)
        vb = jnp.concatenate([flat(v_refs[n * ppb + j]) for j in range(ppb)], axis=0)
        s = _dot_nt(qb, kb) + col_term + row_term
        _flat_online_update([s], [vb], m_ref, l_ref, acc_ref)
    o_ref[0] = acc_ref[...] / l_ref[...]


def moba_sample(q_flat, kn_flat, vn_flat, cache_k, cache_v, page_table, q_start):
    B, n_pages = page_table.shape
    R = q_flat.shape[1]
    T = R // C_HEADS
    page = lambda j: (lambda b, pt: (pt[b, j], 0, 0, 0))
    per_b = lambda b, pt: (b, 0, 0)
    in_specs = [pl.BlockSpec((1, R, HEAD_DIM), per_b),
                pl.BlockSpec((1, LANES, HEAD_DIM), per_b),
                pl.BlockSpec((1, LANES, HEAD_DIM), per_b)]
    in_specs += [pl.BlockSpec((1, *cache_k.shape[1:]), page(j)) for j in range(n_pages)] * 2
    gs = pltpu.PrefetchScalarGridSpec(
        num_scalar_prefetch=1, grid=(B,), in_specs=in_specs,
        out_specs=pl.BlockSpec((1, R, HEAD_DIM), per_b),
        scratch_shapes=[pltpu.VMEM((R, 1), F32), pltpu.VMEM((R, 1), F32), pltpu.VMEM((R, HEAD_DIM), F32)])
    return pl.pallas_call(
        functools.partial(_moba_sample_kernel, T=T, n_pages=n_pages, q_start=q_start),
        grid_spec=gs,
        out_shape=jax.ShapeDtypeStruct((B, R, HEAD_DIM), F32),
        compiler_params=_cparams("arbitrary"),
        name="moba_sample",
    )(page_table, q_flat, kn_flat, vn_flat, *([cache_k] * n_pages), *([cache_v] * n_pages))


def _block_diag_rows(x, n_heads):
    lane = lax.broadcasted_iota(jnp.int32, x.shape, 1) // HEAD_DIM
    return jnp.concatenate([jnp.where(lane == h, x, 0.0) for h in range(n_heads)], axis=0)


def _gather_heads(out, n_heads, T):
    lane = lax.broadcasted_iota(jnp.int32, (T, out.shape[1]), 1) // HEAD_DIM
    y = jnp.zeros((T, out.shape[1]), F32)
    for h in range(n_heads):
        y = y + jnp.where(lane == h, out[h * T:(h + 1) * T, :], 0.0)
    return y


def _pad_rows(x, n):
    return jnp.concatenate([x, jnp.zeros((n - x.shape[0], x.shape[1]), x.dtype)], axis=0)


def _softmax_pv(s_list, s_new, v_refs, v_new, width):
    m = jnp.max(s_new, axis=1, keepdims=True)
    for s in s_list:
        m = jnp.maximum(m, jnp.max(s, axis=1, keepdims=True))
    p = jnp.exp(s_new - m)
    l = jnp.sum(p, axis=1, keepdims=True)
    acc = jnp.dot(_bf(p), v_new, preferred_element_type=F32)
    for s, v_ref in zip(s_list, v_refs):
        p = jnp.exp(s - m)
        l = l + jnp.sum(p, axis=1, keepdims=True)
        acc = acc + _dot_nt(p, v_ref[0].reshape(width, v_ref.shape[-1]))
    return acc / l


def _fox_decode_kernel(pt_ref, q_ref, kn_ref, vn_ref, lfn_ref, *rest, T, NP):
    k_refs = rest[0:NP]
    v_refs = rest[NP:2 * NP]
    lf_refs = rest[2 * NP:3 * NP]
    o_ref = rest[3 * NP]
    H, W = B_HEADS, B_WIDTH
    R = H * T
    page = k_refs[0].shape[-1]
    qbd = _bf(_block_diag_rows(q_ref[0] * (HEAD_DIM ** -0.5), H))
    ti = lax.broadcasted_iota(jnp.int32, (page, page), 0)
    tj = lax.broadcasted_iota(jnp.int32, (page, page), 1)
    triu = (ti <= tj).astype(BF16)

    def cumsum(x, carry):
        hi, mid, lo = _split3(x)
        cs = (jnp.dot(hi, triu, preferred_element_type=F32)
              + (jnp.dot(mid, triu, preferred_element_type=F32)
                 + jnp.dot(lo, triu, preferred_element_type=F32))) + carry
        return cs, jnp.broadcast_to(cs[:, page - 1:page], cs.shape)

    def per_row(c):
        return jnp.broadcast_to(c[:, None, :], (H, T, page)).reshape(R, page)

    carry = jnp.zeros((H, page), F32)
    s_list = []
    for j in range(NP):
        cs, carry = cumsum(lf_refs[j][0], carry)
        s = jnp.dot(qbd, _bf(k_refs[j][0].reshape(W, page)), preferred_element_type=F32)
        s_list.append(s - per_row(cs))
    cs_new, _ = cumsum(lfn_ref[0], carry)
    row_tok = lax.broadcasted_iota(jnp.int32, (R, page), 0) % T
    key = lax.broadcasted_iota(jnp.int32, (R, page), 1)
    s_new = _dot_nt(qbd, _pad_rows(kn_ref[0], page)) - per_row(cs_new)
    s_new = jnp.where(key <= row_tok, s_new, NEG_INF)
    out = _softmax_pv(s_list, s_new, v_refs, _bf(_pad_rows(vn_ref[0], page)), W)
    o_ref[0] = _bf(_gather_heads(out, H, T))


def fox_decode(pab, lfn_t, cache_k, cache_v, cache_lf, page_table, T):
    B, NP = page_table.shape
    W = B_WIDTH
    page4 = lambda j: (lambda b, pt: (pt[b, j], 0, 0, 0))
    page3 = lambda j: (lambda b, pt: (pt[b, j], 0, 0))
    in_specs = [pl.BlockSpec((1, T, W), lambda b, pt: (1, b, 0)),
                pl.BlockSpec((1, T, W), lambda b, pt: (1, b, 1)),
                pl.BlockSpec((1, T, W), lambda b, pt: (1, b, 2)),
                pl.BlockSpec((1, *lfn_t.shape[1:]), lambda b, pt: (b, 0, 0))]
    in_specs += [pl.BlockSpec((1, *cache_k.shape[1:]), page4(j)) for j in range(NP)] * 2
    in_specs += [pl.BlockSpec((1, *cache_lf.shape[1:]), page3(j)) for j in range(NP)]
    gs = pltpu.PrefetchScalarGridSpec(
        num_scalar_prefetch=1, grid=(B,), in_specs=in_specs,
        out_specs=pl.BlockSpec((1, T, W), lambda b, pt: (b, 0, 0)))
    return pl.pallas_call(
        functools.partial(_fox_decode_kernel, T=T, NP=NP),
        grid_spec=gs,
        out_shape=jax.ShapeDtypeStruct((B, T, W), BF16),
        compiler_params=_cparams("arbitrary"),
        name="fox_decode",
    )(page_table, pab, pab, pab, lfn_t, *([cache_k] * NP), *([cache_v] * NP), *([cache_lf] * NP))


def _moba_decode_kernel(pt_ref, q_ref, kn_ref, vn_ref, *rest, T, NP, q_start):
    k_refs = rest[0:NP]
    v_refs = rest[NP:2 * NP]
    o_ref = rest[2 * NP]
    Hkv, W = C_KV_HEADS, C_KV_WIDTH
    page = k_refs[0].shape[-1]
    ppb = MOBA_BLOCK // page
    nbp = NP // ppb
    own = q_start // MOBA_BLOCK
    RG = Hkv * T
    R = C_REP * RG

    qbd = jnp.concatenate([_block_diag_rows(q_ref[i] * (HEAD_DIM ** -0.5), Hkv) for i in range(C_REP)], axis=0)
    qbd_bf = _bf(qbd)
    rowi = lax.broadcasted_iota(jnp.int32, (R, 1), 0)
    row_tok = rowi % T
    head = C_REP * ((rowi // T) % Hkv) + rowi // RG
    slope = jnp.exp2(-8.0 * (head + 1).astype(F32) / C_HEADS)
    qpos = (q_start + row_tok).astype(F32)
    lane = lax.broadcasted_iota(jnp.int32, (R, page), 1)
    lanef = lane.astype(F32)

    kps = [k_refs[j][0].reshape(W, page) for j in range(NP)]
    col = lax.broadcasted_iota(jnp.int32, (W, page), 1)
    km = jnp.zeros((W, page), F32)
    for n in range(nbp):
        tot = jnp.sum(kps[n * ppb], axis=1, keepdims=True)
        for j in range(1, ppb):
            tot = tot + jnp.sum(kps[n * ppb + j], axis=1, keepdims=True)
        km = jnp.where(col == n, tot * (1.0 / MOBA_BLOCK), km)
    live = lane < min(own, nbp)
    sel = _top_blocks(jnp.where(live, _dot3(qbd, *_split2(km)), NEG_INF), live, lanef)

    b0 = slope * lanef
    s_list = []
    for n in range(nbp):
        chosen = jnp.max(jnp.where(lane == n, sel, 0.0), axis=1, keepdims=True) > 0.5
        for j in range(n * ppb, (n + 1) * ppb):
            row_term = jnp.where(chosen, slope * (j * page - qpos), NEG_INF)
            s_list.append(jnp.dot(qbd_bf, _bf(kps[j]), preferred_element_type=F32) + b0 + row_term)
    s_new = _dot_nt(qbd_bf, _pad_rows(kn_ref[0], page)) - slope * (row_tok - lane).astype(F32)
    s_new = jnp.where(lane <= row_tok, s_new, NEG_INF)
    out = _softmax_pv(s_list, s_new, v_refs, _bf(_pad_rows(vn_ref[0], page)), W)
    for i in range(C_REP):
        o_ref[i] = _bf(_gather_heads(out[i * RG:(i + 1) * RG, :], Hkv, T))


def moba_decode(p1, cache_k, cache_v, page_table, T, q_start):
    B, NP = page_table.shape
    W = C_KV_WIDTH
    page4 = lambda j: (lambda b, pt: (pt[b, j], 0, 0, 0))
    in_specs = [pl.BlockSpec((C_REP, T, W), lambda b, pt: (0, b, 0)),
                pl.BlockSpec((1, T, W), lambda b, pt: (C_REP, b, 0)),
                pl.BlockSpec((1, T, W), lambda b, pt: (C_REP + 1, b, 0))]
    in_specs += [pl.BlockSpec((1, *cache_k.shape[1:]), page4(j)) for j in range(NP)] * 2
    gs = pltpu.PrefetchScalarGridSpec(
        num_scalar_prefetch=1, grid=(B,), in_specs=in_specs,
        out_specs=pl.BlockSpec((C_REP, T, W), lambda b, pt: (0, b, 0)))
    return pl.pallas_call(
        functools.partial(_moba_decode_kernel, T=T, NP=NP, q_start=q_start),
        grid_spec=gs,
        out_shape=jax.ShapeDtypeStruct((C_REP, B * T, W), BF16),
        compiler_params=_cparams("arbitrary"),
        name="moba_decode",
    )(page_table, p1, p1, p1, *([cache_k] * NP), *([cache_v] * NP))


def _row(v):
    return v.reshape(1, -1).astype(F32)


def _prep_params(norm0_mix_g, w_in0, fox_b_f, rwkv_mu, rwkv_w0, rwkv_w2, rwkv_a0, rwkv_a2, rwkv_g2, rwkv_k_k,
                 rwkv_k_a, rwkv_r_k, rwkv_ln_w, rwkv_ln_b, w_out0, norm0_ffn_g, ffn_w_gate, ffn_w_up,
                 ffn_w_down, norm1_mix_g, w_in1, w_out1, norm1_ffn_g, router_w, router_b, moe_w_gate,
                 moe_w_up, moe_w_down, norm_final_g):
    D = w_in0.shape[0]
    pad_b = A_PROJ - (3 * B_WIDTH + B_HEADS)
    w0 = jnp.concatenate([w_in0, jnp.zeros((D, pad_b), F32)], axis=1)
    wq = w_in1[:, :C_WIDTH].reshape(D, C_KV_HEADS, C_REP, HEAD_DIM).transpose(0, 2, 1, 3).reshape(D, C_WIDTH)
    w1 = jnp.concatenate([wq, w_in1[:, C_WIDTH:]], axis=1)
    wo1 = w_out1.reshape(C_KV_HEADS, C_REP, HEAD_DIM, D).transpose(1, 0, 2, 3).reshape(C_WIDTH, D)
    hd = lax.broadcasted_iota(jnp.int32, (A_WIDTH, A_WIDTH), 0) // HEAD_DIM
    hd2 = lax.broadcasted_iota(jnp.int32, (A_WIDTH, A_WIDTH), 1) // HEAD_DIM
    rw = jnp.concatenate([router_w.T, jnp.zeros((E_PAD - N_EXPERTS, D), F32)], axis=0)
    rb = jnp.concatenate([router_b, jnp.zeros((E_PAD - N_EXPERTS,), F32)])
    return dict(
        g0=norm0_mix_g, w0=_bf(w0),
        fox_b=jnp.concatenate([fox_b_f, jnp.zeros((LANES - B_HEADS,), F32)]).reshape(1, LANES),
        rwkv=dict(mu=_row(rwkv_mu), w0=_row(rwkv_w0), w2=_split2(rwkv_w2), a0=_row(rwkv_a0), a2=_split2(rwkv_a2),
                  g2=_split2(rwkv_g2), k_k=_row(rwkv_k_k), k_a=_row(rwkv_k_a), r_k=_row(rwkv_r_k),
                  ln_w=_row(rwkv_ln_w), ln_b=_row(rwkv_ln_b), bd=(hd == hd2).astype(BF16)),
        wo0=_bf(w_out0), g0f=norm0_ffn_g,
        ffn_g=_bf(ffn_w_gate)[None], ffn_u=_bf(ffn_w_up)[None], ffn_d=_bf(ffn_w_down),
        g1=norm1_mix_g, w1=_bf(w1), wo1=_bf(wo1), g1f=norm1_ffn_g,
        router=(*_split2(rw), rb.reshape(E_PAD, 1)),
        moe_g=_bf(moe_w_gate), moe_u=_bf(moe_w_up), moe_d=_bf(moe_w_down),
        gf=norm_final_g)


def _pair_states(S):
    B = S.shape[0]
    S = S.reshape(B, A_HEADS // 2, 2, HEAD_DIM, HEAD_DIM)
    z = jnp.zeros_like(S[:, :, 0])
    top = jnp.concatenate([S[:, :, 0], z], axis=-1)
    bot = jnp.concatenate([z, S[:, :, 1]], axis=-1)
    return jnp.concatenate([top, bot], axis=-2)


def _unpair_states(S):
    B = S.shape[0]
    a = S[:, :, :HEAD_DIM, :HEAD_DIM]
    b = S[:, :, HEAD_DIM:, HEAD_DIM:]
    return jnp.stack([a, b], axis=2).reshape(B, A_HEADS, HEAD_DIM, HEAD_DIM)


def _run(P, x, S0, shift0, caches, page_table):
    B, T, D = x.shape
    M = B * T
    xt = x.reshape(M, D)

    pab = norm_matmul(xt, P["g0"], P["w0"], A_PROJ, A_PROJ // 2)
    r, lw, k, v, kn, bb, gg = rwkv_prep(pab, shift0, P["rwkv"], T)
    s_in = jnp.zeros((B, A_HEADS // 2, LANES, LANES), F32) if S0 is None else _pair_states(S0)
    y, s_bd = rwkv_chunk(r, lw, k, v, kn, bb, s_in, B, T)
    ya = rwkv_post(y, r, k, v, gg, P["rwkv"])
    S_new = _unpair_states(s_bd)
    shift_new = lax.slice(pab, (0, T - 1, 0), (1, M, A_PROJ), (1, T, 1)).reshape(B, A_PROJ)

    cols = lambda c: lax.slice(pab, (1, 0, c * B_WIDTH), (2, M, (c + 1) * B_WIDTH))
    fk = cols(1).reshape(B, T, B_HEADS, HEAD_DIM)
    fv = cols(2).reshape(B, T, B_HEADS, HEAD_DIM)
    logf = fox_logf(pab, P["fox_b"], 3 * B_WIDTH // LANES)[:, :B_HEADS].reshape(B, T, B_HEADS)
    if caches is None:
        yb = fox_prompt(pab, cumsum_lanes(jnp.swapaxes(logf, 1, 2)), B, T)
    else:
        page = caches["fox_k"].shape[1]
        by_head = lambda c: jnp.transpose(c, (0, 2, 3, 1))
        lfn_t = jnp.pad(jnp.swapaxes(logf, 1, 2), ((0, 0), (0, 0), (0, page - T)))
        yb = fox_decode(pab, lfn_t, by_head(caches["fox_k"]), by_head(caches["fox_v"]),
                        jnp.swapaxes(caches["fox_logf"], 1, 2), page_table, T).reshape(M, B_WIDTH)
    yab = jnp.concatenate([ya, yb], axis=1)[None]
    x1 = matmul_res(yab, P["wo0"], xt, tk=A_WIDTH + B_WIDTH)

    h = swiglu_up(x1, P["g0f"], P["ffn_g"], P["ffn_u"])
    x2 = matmul_res(h[None], P["ffn_d"], x1, tk=_pick(h.shape[1], (1408, 1024, 512, 256, 128)))

    p1 = norm_matmul(x2, P["g1"], P["w1"], C_KV_WIDTH, C_KV_WIDTH)
    mk = p1[4].reshape(B, T, C_KV_HEADS, HEAD_DIM)
    mv = p1[5].reshape(B, T, C_KV_HEADS, HEAD_DIM)
    if caches is None:
        y1 = moba_prompt(p1, B, T)
    else:
        page = caches["moba_k"].shape[1]
        by_head = lambda c: jnp.transpose(c, (0, 2, 3, 1))
        y1 = moba_decode(p1, by_head(caches["moba_k"]), by_head(caches["moba_v"]), page_table, T,
                         page_table.shape[1] * page)
    x3 = matmul_res(y1, P["wo1"], x2, tk=C_KV_WIDTH)

    xn, rows, cols, meta = moe_route(x3, P["g1f"], P["router"])
    off = meta[:, :N_EXPERTS, 0].astype(jnp.int32)
    run = meta[:, :N_EXPERTS, 1].astype(jnp.int32)
    hs = moe_up_sparse(xn, rows, off, run, P["moe_g"], P["moe_u"])
    out = rmsnorm(moe_down_sparse(hs, cols, rows, off, run, P["moe_d"], x3), P["gf"])
    return out.reshape(B, T, D), S_new, shift_new, fk, fv, logf, mk, mv


def kernel(x_prompt, x_sample, state_rwkv_S, state_rwkv_shift, cache_fox_k, cache_fox_v, cache_fox_logf,
           cache_moba_k, cache_moba_v, page_table, norm0_mix_g, w_in0, fox_b_f, rwkv_mu, rwkv_w0, rwkv_w2,
           rwkv_a0, rwkv_a2, rwkv_g2, rwkv_k_k, rwkv_k_a, rwkv_r_k, rwkv_ln_w, rwkv_ln_b, w_out0, norm0_ffn_g,
           ffn_w_gate, ffn_w_up, ffn_w_down, norm1_mix_g, w_in1, w_out1, norm1_ffn_g, router_w, router_b,
           moe_w_gate, moe_w_up, moe_w_down, norm_final_g):
    P = _prep_params(norm0_mix_g, w_in0, fox_b_f, rwkv_mu, rwkv_w0, rwkv_w2, rwkv_a0, rwkv_a2, rwkv_g2,
                     rwkv_k_k, rwkv_k_a, rwkv_r_k, rwkv_ln_w, rwkv_ln_b, w_out0, norm0_ffn_g, ffn_w_gate,
                     ffn_w_up, ffn_w_down, norm1_mix_g, w_in1, w_out1, norm1_ffn_g, router_w, router_b,
                     moe_w_gate, moe_w_up, moe_w_down, norm_final_g)
    n_prompt = x_prompt.shape[0]
    prompt = _run(P, x_prompt, None, jnp.zeros((n_prompt, A_PROJ), x_prompt.dtype), None, None)
    caches = dict(fox_k=cache_fox_k, fox_v=cache_fox_v, fox_logf=cache_fox_logf,
                  moba_k=cache_moba_k, moba_v=cache_moba_v)
    sample = _run(P, x_sample, state_rwkv_S, state_rwkv_shift, caches, page_table)
    return (prompt[0], sample[0], *prompt[1:], *sample[1:])
```

```python
import functools

import jax
import jax.numpy as jnp
from jax import lax
from jax.experimental import pallas as pl
from jax.experimental.pallas import tpu as pltpu

F32 = jnp.float32
BF16 = jnp.bfloat16

HEAD_DIM = 64
A_HEADS = 16
A_WIDTH = A_HEADS * HEAD_DIM
A_LORA_W = 64
A_LORA_A = 64
A_LORA_G = 128
A_PROJ = 3 * A_WIDTH + A_LORA_W + A_LORA_A + A_LORA_G
RWKV_GN_EPS = 64e-5
B_HEADS = 16
B_WIDTH = B_HEADS * HEAD_DIM
C_HEADS = 32
C_KV_HEADS = 8
C_REP = C_HEADS // C_KV_HEADS
C_WIDTH = C_HEADS * HEAD_DIM
C_KV_WIDTH = C_KV_HEADS * HEAD_DIM
MOBA_BLOCK = 256
MOBA_TOPK = 3
N_EXPERTS = 8
NORM_EPS = 1e-6

LANES = 128
GROUP_LANES = 256
VMEM_LIMIT = 56 * 1024 * 1024
NEG_INF = float("-inf")
LOG2E = 1.4426950408889634
NB_PAD = 16
NT_DIMS = (((1,), (1,)), ((), ()))


def _cparams(*sem):
    return pltpu.CompilerParams(dimension_semantics=sem, vmem_limit_bytes=VMEM_LIMIT)


def _bf(x):
    return x.astype(BF16)


def _dot(a, b):
    return jnp.dot(_bf(a), _bf(b), preferred_element_type=F32)


def _dot_nt(a, b):
    return lax.dot_general(_bf(a), _bf(b), NT_DIMS, preferred_element_type=F32)


def _dot_tn(a, b):
    return lax.dot_general(_bf(a), _bf(b), (((0,), (0,)), ((), ())), preferred_element_type=F32)


def _split2(x):
    hi = _bf(x)
    lo = _bf(x - hi.astype(F32))
    return hi, lo


def _split3(x):
    hi = _bf(x)
    r1 = x - hi.astype(F32)
    mid = _bf(r1)
    lo = _bf(r1 - mid.astype(F32))
    return hi, mid, lo


def _dot3(a, b_hi, b_lo):
    a_hi, a_lo = _split2(a)
    return (jnp.dot(a_hi, b_hi, preferred_element_type=F32)
            + (jnp.dot(a_hi, b_lo, preferred_element_type=F32)
               + jnp.dot(a_lo, b_hi, preferred_element_type=F32)))


def _dot3_nt(a, b_hi, b_lo):
    a_hi, a_lo = _split2(a)
    return (lax.dot_general(a_hi, b_hi, NT_DIMS, preferred_element_type=F32)
            + (lax.dot_general(a_hi, b_lo, NT_DIMS, preferred_element_type=F32)
               + lax.dot_general(a_lo, b_hi, NT_DIMS, preferred_element_type=F32)))


def _head_sums(x, pair_ones):
    hi, lo = _split2(x)
    parts = []
    for c in range(x.shape[1] // LANES):
        sl = slice(c * LANES, (c + 1) * LANES)
        parts.append(jnp.dot(hi[:, sl], pair_ones, preferred_element_type=F32)
                     + jnp.dot(lo[:, sl], pair_ones, preferred_element_type=F32))
    return jnp.concatenate(parts, axis=1)


def _dot_exact_lhs3(a_exact, x):
    hi, mid, lo = _split3(x)
    return (jnp.dot(a_exact, hi, preferred_element_type=F32)
            + (jnp.dot(a_exact, mid, preferred_element_type=F32)
               + jnp.dot(a_exact, lo, preferred_element_type=F32)))


def _sigmoid(x):
    return 1.0 / (1.0 + jnp.exp(-x))


def _pick(n, prefs):
    for p in prefs:
        if n % p == 0:
            return p
    return n


def _norm_mm_kernel(x_ref, g_ref, w_ref, o_ref, xn_ref):
    @pl.when(pl.program_id(1) == 0)
    def _():
        x = x_ref[...]
        ms = jnp.mean(x * x, axis=-1, keepdims=True)
        xn_ref[...] = _bf(x * lax.rsqrt(ms + NORM_EPS) * g_ref[...])

    o_ref[0] = jnp.dot(xn_ref[...], w_ref[...], preferred_element_type=F32)


def norm_matmul(x, g, w, cw, tn):
    M, K = x.shape
    N = w.shape[1]
    nc = N // cw
    tm = _pick(M, (1024, 512, 256, 128, 8))
    per = cw // tn
    return pl.pallas_call(
        _norm_mm_kernel,
        grid=(M // tm, N // tn),
        in_specs=[pl.BlockSpec((tm, K), lambda i, j: (i, 0)),
                  pl.BlockSpec((1, K), lambda i, j: (0, 0)),
                  pl.BlockSpec((K, tn), lambda i, j: (0, j))],
        out_specs=pl.BlockSpec((1, tm, tn), lambda i, j: (j // per, i, j % per)),
        out_shape=jax.ShapeDtypeStruct((nc, M, cw), F32),
        scratch_shapes=[pltpu.VMEM((tm, K), BF16)],
        compiler_params=_cparams("parallel", "arbitrary"),
        name="norm_matmul",
    )(x, g.reshape(1, K), w)


def _mm_res_kernel(a_ref, w_ref, r_ref, o_ref, acc_ref, *, nk):
    k = pl.program_id(2)

    @pl.when(k == 0)
    def _():
        acc_ref[...] = r_ref[...]

    acc_ref[...] += jnp.dot(a_ref[0], w_ref[...], preferred_element_type=F32)

    @pl.when(k == nk - 1)
    def _():
        o_ref[...] = acc_ref[...]


def _mm_res_full_kernel(a_ref, w_ref, r_ref, o_ref):
    ka, _, kw = a_ref.shape
    acc = r_ref[...]
    for c in range(ka):
        acc = acc + jnp.dot(a_ref[c], w_ref[c * kw:(c + 1) * kw, :], preferred_element_type=F32)
    o_ref[...] = acc


def matmul_res(a, w, res, tk):
    ka, M, kw = a.shape
    N = w.shape[1]
    tm = _pick(M, (512, 256, 128, 8))
    tn = _pick(N, (1024, 512, 256, 128))
    per = kw // tk
    nk = ka * per
    if per == 1:
        return pl.pallas_call(
            _mm_res_full_kernel,
            grid=(M // tm, N // tn),
            in_specs=[pl.BlockSpec((ka, tm, kw), lambda i, j: (0, i, 0)),
                      pl.BlockSpec((ka * kw, tn), lambda i, j: (0, j)),
                      pl.BlockSpec((tm, tn), lambda i, j: (i, j))],
            out_specs=pl.BlockSpec((tm, tn), lambda i, j: (i, j)),
            out_shape=jax.ShapeDtypeStruct((M, N), F32),
            compiler_params=_cparams("parallel", "parallel"),
            name="matmul_res_full",
        )(a, w, res)
    return pl.pallas_call(
        functools.partial(_mm_res_kernel, nk=nk),
        grid=(M // tm, N // tn, nk),
        in_specs=[pl.BlockSpec((1, tm, tk), lambda i, j, k: (k // per, i, k % per)),
                  pl.BlockSpec((tk, tn), lambda i, j, k: (k, j)),
                  pl.BlockSpec((tm, tn), lambda i, j, k: (i, j))],
        out_specs=pl.BlockSpec((tm, tn), lambda i, j, k: (i, j)),
        out_shape=jax.ShapeDtypeStruct((M, N), F32),
        scratch_shapes=[pltpu.VMEM((tm, tn), F32)],
        compiler_params=_cparams("parallel", "parallel", "arbitrary"),
        name="matmul_res",
    )(a, w, res)


def _swiglu_up_kernel(x_ref, g_ref, wg_ref, wu_ref, h_ref, xn_ref):
    @pl.when(pl.program_id(1) == 0)
    def _():
        x = x_ref[...]
        ms = jnp.mean(x * x, axis=-1, keepdims=True)
        xn_ref[...] = _bf(x * lax.rsqrt(ms + NORM_EPS) * g_ref[...])

    xn = xn_ref[...]
    a = jnp.dot(xn, wg_ref[...], preferred_element_type=F32)
    b = jnp.dot(xn, wu_ref[...], preferred_element_type=F32)
    h_ref[...] = _bf(a * _sigmoid(a) * b)


def swiglu_up(x, g, wg, wu):
    M, K = x.shape
    F = wg.shape[1]
    tm = _pick(M, (1024, 512, 256, 128, 8))
    tn = _pick(F, (512, 256, 128))
    return pl.pallas_call(
        _swiglu_up_kernel,
        grid=(M // tm, F // tn),
        in_specs=[pl.BlockSpec((tm, K), lambda i, j: (i, 0)),
                  pl.BlockSpec((1, K), lambda i, j: (0, 0)),
                  pl.BlockSpec((K, tn), lambda i, j: (0, j)),
                  pl.BlockSpec((K, tn), lambda i, j: (0, j))],
        out_specs=pl.BlockSpec((tm, tn), lambda i, j: (i, j)),
        out_shape=jax.ShapeDtypeStruct((M, F), BF16),
        scratch_shapes=[pltpu.VMEM((tm, K), BF16)],
        compiler_params=_cparams("parallel", "arbitrary"),
        name="ffn_up",
    )(x, g.reshape(1, K), wg, wu)


MOE_RUN = 128
E_PAD = 16
TOP_K = 2


def _moe_cap(TM):
    return TOP_K * TM + N_EXPERTS * MOE_RUN


def _run_sizes(TM):
    top = -(-TM // MOE_RUN) * MOE_RUN
    sizes, s = [], MOE_RUN
    while s <= top:
        sizes.append(s)
        s *= 2
    return tuple(reversed(sizes))


def _moe_route_kernel(x_ref, g_ref, rwh_ref, rwl_ref, rb_ref, xn_ref, rows_ref, cols_ref, meta_ref):
    x = x_ref[...]
    TM = x.shape[0]
    ms = jnp.mean(x * x, axis=-1, keepdims=True)
    xn = x * lax.rsqrt(ms + NORM_EPS) * g_ref[...]
    xn_ref[...] = _bf(xn)
    hi, lo = _split2(xn)
    rwh = rwh_ref[...]
    logits = (lax.dot_general(rwh, hi, NT_DIMS, preferred_element_type=F32)
              + (lax.dot_general(rwh, lo, NT_DIMS, preferred_element_type=F32)
                 + lax.dot_general(rwl_ref[...], hi, NT_DIMS, preferred_element_type=F32))) + rb_ref[...]
    ef = lax.broadcasted_iota(jnp.int32, (E_PAD, TM), 0).astype(F32)
    z = jnp.where(ef < N_EXPERTS, logits, NEG_INF)
    m1 = jnp.max(z, axis=0, keepdims=True)
    i1 = jnp.min(jnp.where(z == m1, ef, float(E_PAD)), axis=0, keepdims=True)
    z2 = jnp.where(ef == i1, NEG_INF, z)
    m2 = jnp.max(z2, axis=0, keepdims=True)
    i2 = jnp.min(jnp.where(z2 == m2, ef, float(E_PAD)), axis=0, keepdims=True)
    e2 = jnp.exp(m2 - m1)
    den = 1.0 + e2
    member = jnp.where((ef == i1) | (ef == i2), 1.0, 0.0)
    before = (lax.broadcasted_iota(jnp.int32, (TM, TM), 0) < lax.broadcasted_iota(jnp.int32, (TM, TM), 1))
    rank = jnp.dot(_bf(member), before.astype(BF16), preferred_element_type=F32)
    cnt = jnp.sum(member, axis=1, keepdims=True)
    run = jnp.floor((cnt + (MOE_RUN - 1)) * (1.0 / MOE_RUN)) * MOE_RUN
    run_b = jnp.broadcast_to(run, (E_PAD, LANES))
    lower = (lax.broadcasted_iota(jnp.int32, (E_PAD, E_PAD), 0)
             > lax.broadcasted_iota(jnp.int32, (E_PAD, E_PAD), 1)).astype(BF16)
    off = jnp.dot(lower, _bf(run_b), preferred_element_type=F32)
    slot = off[:, 0:1] + rank
    slot1 = jnp.sum(jnp.where(ef == i1, slot, 0.0), axis=0, keepdims=True)
    slot2 = jnp.sum(jnp.where(ef == i2, slot, 0.0), axis=0, keepdims=True)
    info = jnp.concatenate([slot1, slot2, 1.0 / den, e2 / den, jnp.zeros((E_PAD - 4, TM), F32)], axis=0)
    rows_ref[0] = info
    ident = (lax.broadcasted_iota(jnp.int32, (E_PAD, LANES), 0)
             == lax.broadcasted_iota(jnp.int32, (E_PAD, LANES), 1)).astype(BF16)
    tn_dims = (((0,), (0,)), ((), ()))
    h3, m3, l3 = _split3(info)
    cols_ref[0] = (lax.dot_general(h3, ident, tn_dims, preferred_element_type=F32)
                   + (lax.dot_general(m3, ident, tn_dims, preferred_element_type=F32)
                      + lax.dot_general(l3, ident, tn_dims, preferred_element_type=F32)))
    lane = lax.broadcasted_iota(jnp.int32, (E_PAD, LANES), 1)
    meta_ref[0] = jnp.where(lane == 0, off, jnp.where(lane == 1, run_b, 0.0))


def moe_route(x, g, router):
    M, D = x.shape
    TM = _pick(M, (1024, 512, 256, 128, 64))
    NT = M // TM
    rwh, rwl, rb = router
    full = lambda a: pl.BlockSpec(a.shape, lambda i: (0,) * a.ndim)
    return pl.pallas_call(
        _moe_route_kernel,
        grid=(NT,),
        in_specs=[pl.BlockSpec((TM, D), lambda i: (i, 0)), pl.BlockSpec((1, D), lambda i: (0, 0)),
                  full(rwh), full(rwl), full(rb)],
        out_specs=[pl.BlockSpec((TM, D), lambda i: (i, 0)),
                   pl.BlockSpec((1, E_PAD, TM), lambda i: (i, 0, 0)),
                   pl.BlockSpec((1, TM, LANES), lambda i: (i, 0, 0)),
                   pl.BlockSpec((1, E_PAD, LANES), lambda i: (i, 0, 0))],
        out_shape=[jax.ShapeDtypeStruct((M, D), BF16),
                   jax.ShapeDtypeStruct((NT, E_PAD, TM), F32),
                   jax.ShapeDtypeStruct((NT, TM, LANES), F32),
                   jax.ShapeDtypeStruct((NT, E_PAD, LANES), F32)],
        compiler_params=_cparams("parallel"),
        name="moe_route",
    )(x, g.reshape(1, D), rwh, rwl, rb)


def _for_each_run_chunk(off_ref, run_ref, e, sizes, body):
    i = pl.program_id(0)
    base = off_ref[i, e]
    n = run_ref[i, e]
    for size in sizes:
        @pl.when((n & size) != 0)
        def _(size=size):
            body(pl.multiple_of(base + (n & (-2 * size)), MOE_RUN), size)


def _moe_up_kernel(off_ref, run_ref, xn_ref, rows_ref, wg_ref, wu_ref, hs_ref, xs_ref, *, sizes, GC):
    j = pl.program_id(1)
    e = pl.program_id(2)
    TM = xn_ref.shape[0]
    CAP = xs_ref.shape[0]

    @pl.when((j == 0) & (e == 0))
    def _():
        s1 = rows_ref[0, 0:1, :]
        s2 = rows_ref[0, 1:2, :]
        xnb = xn_ref[...]
        for c in range(CAP // GC):
            srow = (c * GC + lax.broadcasted_iota(jnp.int32, (GC, TM), 0)).astype(F32)
            onehot = jnp.where((srow == s1) | (srow == s2), 1.0, 0.0)
            xs_ref[c * GC:(c + 1) * GC, :] = _bf(jnp.dot(_bf(onehot), xnb, preferred_element_type=F32))

    @pl.when(e == 0)
    def _():
        hs_ref[...] = jnp.zeros_like(hs_ref)

    def body(start, size):
        xr = xs_ref[pl.ds(start, size), :]
        a = jnp.dot(xr, wg_ref[0], preferred_element_type=F32)
        b = jnp.dot(xr, wu_ref[0], preferred_element_type=F32)
        hs_ref[0, pl.ds(start, size), :] = _bf(a * _sigmoid(a) * b)

    _for_each_run_chunk(off_ref, run_ref, e, sizes, body)


def moe_up_sparse(xn, rows, off, run, wg, wu):
    M, D = xn.shape
    NT, _, TM = rows.shape
    E, _, F = wg.shape
    CAP = _moe_cap(TM)
    tn = _pick(F, (256, 128))
    gs = pltpu.PrefetchScalarGridSpec(
        num_scalar_prefetch=2, grid=(NT, F // tn, E),
        in_specs=[pl.BlockSpec((TM, D), lambda i, j, e, o, r: (i, 0)),
                  pl.BlockSpec((1, E_PAD, TM), lambda i, j, e, o, r: (i, 0, 0)),
                  pl.BlockSpec((1, D, tn), lambda i, j, e, o, r: (e, 0, j)),
                  pl.BlockSpec((1, D, tn), lambda i, j, e, o, r: (e, 0, j))],
        out_specs=pl.BlockSpec((1, CAP, tn), lambda i, j, e, o, r: (i, 0, j)),
        scratch_shapes=[pltpu.VMEM((CAP, D), BF16)])
    return pl.pallas_call(
        functools.partial(_moe_up_kernel, sizes=_run_sizes(TM), GC=_pick(CAP, (512, 256, 128))),
        grid_spec=gs,
        out_shape=jax.ShapeDtypeStruct((NT, CAP, F), BF16),
        compiler_params=_cparams("arbitrary", "arbitrary", "arbitrary"),
        name="moe_up",
    )(off, run, xn, rows, wg, wu)


def _moe_down_kernel(off_ref, run_ref, hs_ref, wd_ref, x_ref, cols_ref, rows_ref, o_ref, ys_ref, *,
                     sizes, GC, n_e, n_kf):
    kf = pl.program_id(2)
    e = pl.program_id(3)
    TM = x_ref.shape[0]
    CAP = ys_ref.shape[0]

    @pl.when((e == 0) & (kf == 0))
    def _():
        ys_ref[...] = jnp.zeros_like(ys_ref)

    def body(start, size):
        ys_ref[pl.ds(start, size), :] += jnp.dot(hs_ref[0, pl.ds(start, size), :], wd_ref[0],
                                                 preferred_element_type=F32)

    _for_each_run_chunk(off_ref, run_ref, e, sizes, body)

    @pl.when((e == n_e - 1) & (kf == n_kf - 1))
    def _():
        info = cols_ref[0]
        s1, s2 = info[:, 0:1], info[:, 1:2]
        r1, r2, p1, p2 = (rows_ref[0, k:k + 1, :] for k in range(4))
        acc = x_ref[...]
        for c in range(CAP // GC):
            srow = (c * GC + lax.broadcasted_iota(jnp.int32, (GC, TM), 0)).astype(F32)
            gate = jnp.sum(jnp.where(srow == r1, p1, 0.0) + jnp.where(srow == r2, p2, 0.0), axis=1, keepdims=True)
            scol = (c * GC + lax.broadcasted_iota(jnp.int32, (TM, GC), 1)).astype(F32)
            onehot = jnp.where((scol == s1) | (scol == s2), 1.0, 0.0)
            acc = acc + jnp.dot(_bf(onehot), _bf(ys_ref[c * GC:(c + 1) * GC, :] * gate), preferred_element_type=F32)
        o_ref[...] = acc


def moe_down_sparse(hs, cols, rows, off, run, wd, x):
    NT, CAP, F = hs.shape
    TM = cols.shape[1]
    E, _, D = wd.shape
    tn = _pick(D, (512, 256, 128))
    tk = _pick(F, (1408, 1024, 512, 256, 128))
    n_kf = F // tk
    gs = pltpu.PrefetchScalarGridSpec(
        num_scalar_prefetch=2, grid=(NT, D // tn, n_kf, E),
        in_specs=[pl.BlockSpec((1, CAP, tk), lambda i, n, k, e, o, r: (i, 0, k)),
                  pl.BlockSpec((1, tk, tn), lambda i, n, k, e, o, r: (e, k, n)),
                  pl.BlockSpec((TM, tn), lambda i, n, k, e, o, r: (i, n)),
                  pl.BlockSpec((1, TM, LANES), lambda i, n, k, e, o, r: (i, 0, 0)),
                  pl.BlockSpec((1, E_PAD, TM), lambda i, n, k, e, o, r: (i, 0, 0))],
        out_specs=pl.BlockSpec((TM, tn), lambda i, n, k, e, o, r: (i, n)),
        scratch_shapes=[pltpu.VMEM((CAP, tn), F32)])
    return pl.pallas_call(
        functools.partial(_moe_down_kernel, sizes=_run_sizes(TM), GC=_pick(CAP, (512, 256, 128)), n_e=E, n_kf=n_kf),
        grid_spec=gs,
        out_shape=jax.ShapeDtypeStruct(x.shape, F32),
        compiler_params=_cparams("arbitrary", "arbitrary", "arbitrary", "arbitrary"),
        name="moe_down",
    )(off, run, hs, wd, x, cols, rows)


def _rmsnorm_kernel(x_ref, g_ref, o_ref):
    x = x_ref[...]
    ms = jnp.mean(x * x, axis=-1, keepdims=True)
    o_ref[...] = x * lax.rsqrt(ms + NORM_EPS) * g_ref[...]


def rmsnorm(x, g):
    M, D = x.shape
    tm = _pick(M, (512, 256, 128, 8))
    return pl.pallas_call(
        _rmsnorm_kernel,
        grid=(M // tm,),
        in_specs=[pl.BlockSpec((tm, D), lambda i: (i, 0)), pl.BlockSpec((1, D), lambda i: (0, 0))],
        out_specs=pl.BlockSpec((tm, D), lambda i: (i, 0)),
        out_shape=jax.ShapeDtypeStruct((M, D), F32),
        compiler_params=_cparams("parallel"),
        name="final_norm",
    )(x, g.reshape(1, D))


def _rwkv_prep_kernel(p_ref, prev_ref, s0_ref, mu_ref, w0_ref, w2h_ref, w2l_ref, a0_ref, a2h_ref, a2l_ref,
                      g2h_ref, g2l_ref, kk_ref, ka_ref, bd_ref,
                      r_ref, lw_ref, k_ref, v_ref, kn_ref, b_ref, g_ref, *, T):
    p = p_ref[0]
    tm = p.shape[0]
    row = lax.broadcasted_iota(jnp.int32, (tm, 1), 0)
    shifted = jnp.where(row == 0, prev_ref[0, 7:8, :], pltpu.roll(p, 1, 0))
    nseq = s0_ref.shape[0]
    s0_rows = jnp.broadcast_to(s0_ref[...], (nseq, tm // nseq, p.shape[1])).reshape(tm, p.shape[1])
    first = ((pl.program_id(0) * tm + row) % T) == 0
    ps = p + (jnp.where(first, s0_rows, shifted) - p) * mu_ref[...]
    W = A_WIDTH
    r = ps[:, 0:W]
    k = ps[:, W:2 * W]
    v = ps[:, 2 * W:3 * W]
    o = 3 * W
    wd = ps[:, o:o + A_LORA_W]
    ad = ps[:, o + A_LORA_W:o + A_LORA_W + A_LORA_A]
    gd = ps[:, o + A_LORA_W + A_LORA_A:]
    z = -(w0_ref[...] + _dot3(jnp.tanh(wd), w2h_ref[...], w2l_ref[...]))
    softplus = jnp.maximum(z, 0.0) + jnp.log(1.0 + jnp.exp(-jnp.abs(z)))
    lw_ref[...] = -jnp.exp(-softplus - 0.5)
    a = _sigmoid(a0_ref[...] + _dot3(ad, a2h_ref[...], a2l_ref[...]))
    g_ref[...] = _dot3(_sigmoid(gd), g2h_ref[...], g2l_ref[...])
    kk = k * kk_ref[...]
    ss = _head_sums(kk * kk, bd_ref[...])
    kn = kk / jnp.maximum(jnp.sqrt(ss), 1e-12)
    r_ref[...] = r
    v_ref[...] = v
    kn_ref[...] = kn
    b_ref[...] = kn * a
    k_ref[...] = k * (1.0 + (a - 1.0) * ka_ref[...])


def rwkv_prep(pab, shift0, prm, T):
    M = pab.shape[1]
    tm = _pick(M, (256, 128, 8))
    assert T % tm == 0 or tm % T == 0
    nseq = max(tm // T, 1)
    s0_map = (lambda i: (i, 0, 0)) if tm >= T else (lambda i: ((i * tm) // T, 0, 0))
    W = A_WIDTH
    row = lambda n: pl.BlockSpec((1, n), lambda i: (0, 0))
    mat = lambda a, b: pl.BlockSpec((a, b), lambda i: (0, 0))
    out = pl.BlockSpec((tm, W), lambda i: (i, 0))
    return pl.pallas_call(
        functools.partial(_rwkv_prep_kernel, T=T),
        grid=(M // tm,),
        in_specs=[pl.BlockSpec((1, tm, A_PROJ), lambda i: (0, i, 0)),
                  pl.BlockSpec((1, 8, A_PROJ), lambda i: (0, jnp.maximum(i * (tm // 8) - 1, 0), 0)),
                  pl.BlockSpec((nseq, 1, A_PROJ), s0_map),
                  row(A_PROJ), row(W), mat(A_LORA_W, W), mat(A_LORA_W, W),
                  row(W), mat(A_LORA_A, W), mat(A_LORA_A, W),
                  mat(A_LORA_G, W), mat(A_LORA_G, W), row(W), row(W), mat(LANES, LANES)],
        out_specs=[out] * 7,
        out_shape=[jax.ShapeDtypeStruct((M, W), F32)] * 7,
        compiler_params=_cparams("parallel"),
        name="rwkv_prep",
    )(pab, pab, shift0[:, None, :], prm["mu"], prm["w0"], *prm["w2"], prm["a0"], *prm["a2"], *prm["g2"],
      prm["k_k"], prm["k_a"], prm["bd"])


def _rwkv_chunk_kernel(r_ref, lw_ref, k_ref, v_ref, kn_ref, b_ref, s0_ref, y_ref, s_ref, *, C, nc, Bb):
    C2 = 2 * C
    lane = lax.broadcasted_iota(jnp.int32, (C2, LANES), 1)
    rowi = lax.broadcasted_iota(jnp.int32, (C2, LANES), 0)
    mask2 = ((rowi >= C) == (lane >= HEAD_DIM)).astype(F32)
    ri = lax.broadcasted_iota(jnp.int32, (C2, C2), 0)
    ci = lax.broadcasted_iota(jnp.int32, (C2, C2), 1)
    same = (ri >= C) == (ci >= C)
    strict = same & (ri > ci)
    incl = same & (ri >= ci)
    eye = (ri == ci).astype(F32)
    ti = lax.broadcasted_iota(jnp.int32, (C, C), 0)
    tj = lax.broadcasted_iota(jnp.int32, (C, C), 1)
    tri = (ti >= tj).astype(BF16)
    n_levels = max(C.bit_length() - 2, 0)
    merged = C2 % LANES == 0
    streams = range(Bb)

    def stack(x):
        return jnp.concatenate([x, x], axis=0) * mask2

    def each(f, *lists):
        return [f(*[l[i] for l in lists]) for i in streams]

    @pl.when(pl.program_id(2) == 0)
    def _():
        s_ref[...] = s0_ref[...]

    def chunk(c, states):
        sl = pl.ds(pl.multiple_of(c * C, C), C)
        S = list(states)
        load = lambda ref: [ref[i, sl, :] for i in streams]
        r, lw, k, v, kn, b = (load(ref) for ref in (r_ref, lw_ref, k_ref, v_ref, kn_ref, b_ref))
        cs = each(lambda x: _dot_exact_lhs3(tri, x), lw)
        gt = each(lambda x: x[C - 1:C, :], cs)
        e_neg = each(lambda x: jnp.exp(-x), cs)
        e_rem = each(lambda g, x: jnp.exp(g - x), gt, cs)
        KT = each(lambda a, x, l: stack(a * jnp.exp(x - l)), kn, cs, lw)
        BI = each(lambda a, e: stack(a * e), b, e_neg)
        KI = each(lambda a, e: stack(a * e), k, e_neg)
        RT = each(lambda a, x: stack(a * jnp.exp(x)), r, cs)
        V2 = each(stack, v)
        KG = each(lambda a, e: stack(a * e), k, e_rem)
        BG = each(lambda a, e: stack(a * e), b, e_rem)
        if merged:
            quad = each(lambda kt, rt, bi, ki: _dot_nt(jnp.concatenate([kt, rt], axis=0),
                                                        jnp.concatenate([bi, ki], axis=0)), KT, RT, BI, KI)
            a_kb = each(lambda q: q[:C2, :C2], quad)
            a_kv = each(lambda q: q[:C2, C2:], quad)
            a_rb = each(lambda q: q[C2:, :C2], quad)
            a_rk = each(lambda q: q[C2:, C2:], quad)
        else:
            a_kb = each(_dot_nt, KT, BI)
            a_kv = each(_dot_nt, KT, KI)
            a_rb = each(_dot_nt, RT, BI)
            a_rk = each(_dot_nt, RT, KI)
        a_rb = each(lambda x: jnp.where(incl, x, 0.0), a_rb)
        a_rk = each(lambda x: jnp.where(incl, x, 0.0), a_rk)
        pw = each(lambda x: -jnp.where(strict, x, 0.0), a_kb)
        tinv = each(lambda x: eye + x, pw)
        if n_levels:
            pw = each(_dot, pw, pw)
            for _ in range(n_levels - 1):
                both = each(lambda p, t: _dot(jnp.concatenate([p, t], axis=0), p), pw, tinv)
                pw = each(lambda x: x[:C2], both)
                tinv = each(lambda t, x: t + x[C2:], tinv, both)
            tinv = each(lambda t, p: t + _dot(t, p), tinv, pw)
        av = each(lambda x, vv: _dot(jnp.where(strict, x, 0.0), vv), a_kv, V2)
        kpw = each(lambda t, kt, a: _dot(t, jnp.concatenate([kt, a], axis=1)), tinv, KT, av)
        corr = each(_dot, a_rb, kpw)
        rp = each(lambda x, c_: x - c_[:, :LANES], RT, corr)
        y1 = each(lambda a, vv, c_: _dot(a, vv) - c_[:, LANES:], a_rk, V2, corr)
        low = each(_dot_tn, kpw, BG)
        mlow = each(lambda x: x[:LANES], low)
        nt = each(lambda vv, kg, x: _dot_tn(vv, kg) - x[LANES:], V2, KG, low)
        y2 = each(lambda p, s, y: _dot_nt(p, s) + y, rp, S, y1)
        for i in streams:
            y_ref[i, sl, :] = y2[i][:C] + y2[i][C:]
        new = each(lambda s, g, m, n: s * jnp.exp(g) - _dot3(s, *_split2(m)) + n, S, gt, mlow, nt)
        return tuple(new)

    final = lax.fori_loop(0, nc, chunk, tuple(s_ref[i, 0] for i in streams))
    for i in streams:
        s_ref[i, 0] = final[i]


def rwkv_chunk(r, lw, k, v, kn, b, s_bd, B, T):
    C = min(T, 64)
    Tt = _pick(T, (512, 256, 128, 64)) if T > C else T
    nc = Tt // C
    Bb = _pick(B, (4, 2, 1)) if T > C else _pick(B, (8, 4, 2, 1))
    HP = A_WIDTH // LANES
    seq = pl.BlockSpec((Bb, Tt, LANES), lambda i, h, t: (i, t, h))
    st = pl.BlockSpec((Bb, 1, LANES, LANES), lambda i, h, t: (i, h, 0, 0))
    r3 = lambda x: x.reshape(B, T, A_WIDTH)
    y, s = pl.pallas_call(
        functools.partial(_rwkv_chunk_kernel, C=C, nc=nc, Bb=Bb),
        grid=(B // Bb, HP, T // Tt),
        in_specs=[seq] * 6 + [st],
        out_specs=[seq, st],
        out_shape=[jax.ShapeDtypeStruct((B, T, A_WIDTH), F32),
                   jax.ShapeDtypeStruct((B, HP, LANES, LANES), F32)],
        compiler_params=_cparams("parallel", "parallel", "arbitrary"),
        name="rwkv_chunk",
    )(r3(r), r3(lw), r3(k), r3(v), r3(kn), r3(b), s_bd)
    return y.reshape(B * T, A_WIDTH), s


def _rwkv_post_kernel(y_ref, r_ref, k_ref, v_ref, g_ref, rk_ref, lnw_ref, lnb_ref, bd_ref, o_ref):
    y = y_ref[...]
    bd = bd_ref[...]
    inv_n = 1.0 / HEAD_DIM
    mean = _head_sums(y, bd) * inv_n
    d = y - mean
    var = _head_sums(d * d, bd) * inv_n
    yn = d * lax.rsqrt(var + RWKV_GN_EPS) * lnw_ref[...] + lnb_ref[...]
    bonus = _head_sums(r_ref[...] * k_ref[...] * rk_ref[...], bd)
    o_ref[...] = _bf((yn + bonus * v_ref[...]) * g_ref[...])


def rwkv_post(y, r, k, v, g, prm):
    M, W = y.shape
    tm = _pick(M, (256, 128, 8))
    blk = pl.BlockSpec((tm, W), lambda i: (i, 0))
    row = pl.BlockSpec((1, W), lambda i: (0, 0))
    return pl.pallas_call(
        _rwkv_post_kernel,
        grid=(M // tm,),
        in_specs=[blk] * 5 + [row] * 3 + [pl.BlockSpec((LANES, LANES), lambda i: (0, 0))],
        out_specs=blk,
        out_shape=jax.ShapeDtypeStruct((M, W), BF16),
        compiler_params=_cparams("parallel"),
        name="rwkv_post",
    )(y, r, k, v, g, prm["r_k"], prm["ln_w"], prm["ln_b"], prm["bd"])


def _logf_kernel(f_ref, b_ref, o_ref):
    z = f_ref[0] + b_ref[...]
    o_ref[...] = jnp.minimum(z, 0.0) - jnp.log(1.0 + jnp.exp(-jnp.abs(z)))


def fox_logf(pab, bias_row, col_block):
    M = pab.shape[1]
    tm = _pick(M, (1024, 512, 256, 128, 8))
    return pl.pallas_call(
        _logf_kernel,
        grid=(M // tm,),
        in_specs=[pl.BlockSpec((1, tm, LANES), lambda i: (1, i, col_block)),
                  pl.BlockSpec((1, LANES), lambda i: (0, 0))],
        out_specs=pl.BlockSpec((tm, LANES), lambda i: (i, 0)),
        out_shape=jax.ShapeDtypeStruct((M, LANES), F32),
        compiler_params=_cparams("parallel"),
        name="fox_logf",
    )(pab, bias_row)


def _cumsum_kernel(x_ref, o_ref, carry_ref):
    @pl.when(pl.program_id(1) == 0)
    def _():
        carry_ref[...] = jnp.zeros_like(carry_ref)

    ti = lax.broadcasted_iota(jnp.int32, (LANES, LANES), 0)
    tj = lax.broadcasted_iota(jnp.int32, (LANES, LANES), 1)
    triu = (ti <= tj).astype(BF16)
    hi, mid, lo = _split3(x_ref[0])
    cs = (jnp.dot(hi, triu, preferred_element_type=F32)
          + (jnp.dot(mid, triu, preferred_element_type=F32)
             + jnp.dot(lo, triu, preferred_element_type=F32))) + carry_ref[...]
    o_ref[0] = cs
    carry_ref[...] = jnp.broadcast_to(cs[:, LANES - 1:LANES], cs.shape)


def cumsum_lanes(xT):
    B, H, L = xT.shape
    return pl.pallas_call(
        _cumsum_kernel,
        grid=(B, L // LANES),
        in_specs=[pl.BlockSpec((1, H, LANES), lambda b, p: (b, 0, p))],
        out_specs=pl.BlockSpec((1, H, LANES), lambda b, p: (b, 0, p)),
        out_shape=jax.ShapeDtypeStruct((B, H, L), F32),
        scratch_shapes=[pltpu.VMEM((H, LANES), F32)],
        compiler_params=_cparams("parallel", "arbitrary"),
        name="fox_cumsum",
    )(xT)


def _head_lane_mask(n_rows, width, head):
    lane = lax.broadcasted_iota(jnp.int32, (n_rows, width), 1)
    return (lane // HEAD_DIM) == head


def _two_pass_attend(qs, n_past, add_past, diag_start, add_diag, kb_ref, vb_ref, s_ref, acc_ref, l_ref, m_ref):
    blk = MOBA_BLOCK
    rep = blk // LANES

    def scores(kb):
        return lax.dot_general(qs, kb, NT_DIMS, preferred_element_type=F32)

    s_d = add_diag(scores(kb_ref[pl.ds(diag_start, blk), :]))
    s_ref[s_ref.shape[0] - 1] = s_d
    l_ref[...] = s_d

    def pass1(n, carry):
        s = add_past(n, scores(kb_ref[pl.ds(pl.multiple_of(n * blk, blk), blk), :]))
        s_ref[n] = s
        l_ref[...] = jnp.maximum(l_ref[...], s)
        return carry

    lax.fori_loop(0, n_past, pass1, 0)
    m_ref[...] = jnp.broadcast_to(jnp.max(l_ref[...], axis=1, keepdims=True), m_ref.shape)

    def probs(n):
        return jnp.exp2(s_ref[n] - jnp.concatenate([m_ref[...]] * rep, axis=1))

    p_d = probs(s_ref.shape[0] - 1)
    l_ref[...] = p_d
    acc_ref[...] = jnp.dot(_bf(p_d), vb_ref[pl.ds(diag_start, blk), :], preferred_element_type=F32)

    def pass2(n, carry):
        p = probs(n)
        l_ref[...] += p
        acc_ref[...] += jnp.dot(_bf(p), vb_ref[pl.ds(pl.multiple_of(n * blk, blk), blk), :],
                                preferred_element_type=F32)
        return carry

    lax.fori_loop(0, n_past, pass2, 0)
    return acc_ref[...] / jnp.sum(l_ref[...], axis=1, keepdims=True)


def _attend_scratch(n_blocks, R):
    return [pltpu.VMEM((n_blocks + 1, R, MOBA_BLOCK), F32),
            pltpu.VMEM((R, GROUP_LANES), F32),
            pltpu.VMEM((R, MOBA_BLOCK), F32),
            pltpu.VMEM((R, LANES), F32)]


def _fox_prompt_kernel(q_ref, k_ref, v_ref, c_ref, o_ref, kb_ref, vb_ref, s_ref, acc_ref, l_ref, m_ref, *, tq, hpg):
    g = pl.program_id(1)
    qi = pl.program_id(2)

    @pl.when(qi == 0)
    def _():
        kb_ref[...] = _bf(k_ref[0])
        vb_ref[...] = _bf(v_ref[0])

    q = q_ref[0] * (HEAD_DIM ** -0.5 * LOG2E)
    qs = _bf(jnp.concatenate([jnp.where(_head_lane_mask(tq, GROUP_LANES, h), q, 0.0) for h in range(hpg)], axis=0))
    causal = lax.broadcasted_iota(jnp.int32, (tq, tq), 0) >= lax.broadcasted_iota(jnp.int32, (tq, tq), 1)

    def add_bias(n, s, mask):
        parts = []
        for h in range(hpg):
            c_row = c_ref[0, pl.ds(g * hpg + h, 1), pl.ds(pl.multiple_of(n * tq, tq), tq)] * LOG2E
            sh = s[h * tq:(h + 1) * tq, :] - c_row
            parts.append(jnp.where(causal, sh, NEG_INF) if mask else sh)
        return jnp.concatenate(parts, axis=0)

    o = _two_pass_attend(qs, qi, lambda n, s: add_bias(n, s, False), pl.multiple_of(qi * tq, tq),
                         lambda s: add_bias(qi, s, True), kb_ref, vb_ref, s_ref, acc_ref, l_ref, m_ref)
    out = jnp.zeros((tq, GROUP_LANES), F32)
    for h in range(hpg):
        out = out + jnp.where(_head_lane_mask(tq, GROUP_LANES, h), o[h * tq:(h + 1) * tq, :], 0.0)
    o_ref[...] = _bf(out)


def fox_prompt(pab, cT, B, T):
    tq = MOBA_BLOCK
    nq = T // tq
    hpg = GROUP_LANES // HEAD_DIM
    G = B_WIDTH // GROUP_LANES
    return pl.pallas_call(
        functools.partial(_fox_prompt_kernel, tq=tq, hpg=hpg),
        grid=(B, G, nq),
        in_specs=[pl.BlockSpec((1, tq, GROUP_LANES), lambda b, g, qi: (1, b * nq + qi, g)),
                  pl.BlockSpec((1, T, GROUP_LANES), lambda b, g, qi: (1, b, G + g)),
                  pl.BlockSpec((1, T, GROUP_LANES), lambda b, g, qi: (1, b, 2 * G + g)),
                  pl.BlockSpec((1, B_HEADS, T), lambda b, g, qi: (b, 0, 0))],
        out_specs=pl.BlockSpec((tq, GROUP_LANES), lambda b, g, qi: (b * nq + qi, g)),
        out_shape=jax.ShapeDtypeStruct((B * T, B_WIDTH), BF16),
        scratch_shapes=[pltpu.VMEM((T, GROUP_LANES), BF16),
                        pltpu.VMEM((T, GROUP_LANES), BF16)] + _attend_scratch(nq - 1, hpg * tq),
        compiler_params=_cparams("parallel", "parallel", "arbitrary"),
        name="fox_prompt",
    )(pab, pab, pab, cT)


def _moba_slopes(n_rows, rows_per_head, i, kv_base):
    c = lax.broadcasted_iota(jnp.int32, (n_rows, 1), 0) // rows_per_head
    head = C_REP * (kv_base + c) + i
    return jnp.exp2(-8.0 * (head + 1).astype(F32) / C_HEADS)


def _top_blocks(z, live, idxf, axis=1):
    sel = jnp.zeros(z.shape, F32)
    for _ in range(MOBA_TOPK):
        m = jnp.max(z, axis=axis, keepdims=True)
        idx = jnp.min(jnp.where((z == m) & live, idxf, float(LANES)), axis=axis, keepdims=True)
        pick = idxf == idx
        sel = jnp.where(pick, 1.0, sel)
        z = jnp.where(pick, NEG_INF, z)
    return sel


def _moba_prompt_kernel(q0_ref, q1_ref, q2_ref, q3_ref, k_ref, v_ref, o_ref,
                        kb_ref, vb_ref, km_ref, s_ref, acc_ref, l_ref, m_ref, *, tq, nb, cpg):
    g = pl.program_id(1)
    qi = pl.program_id(2)
    R = cpg * tq
    blk = MOBA_BLOCK

    @pl.when(qi == 0)
    def _():
        km_ref[...] = jnp.zeros_like(km_ref)
        for n in range(nb):
            kblk = k_ref[0, n * blk:(n + 1) * blk, :]
            kb_ref[n * blk:(n + 1) * blk, :] = _bf(kblk)
            vb_ref[n * blk:(n + 1) * blk, :] = _bf(v_ref[0, n * blk:(n + 1) * blk, :])
            km_ref[n:n + 1, :] = jnp.sum(kblk, axis=0, keepdims=True) * (1.0 / blk)

    own = (qi * tq) // blk
    row_tok = lax.broadcasted_iota(jnp.int32, (R, 1), 0) % tq
    qpos = (qi * tq + row_tok).astype(F32)
    koff = lax.broadcasted_iota(jnp.int32, (R, blk), 1)
    causal = (qi * tq - own * blk + row_tok) >= koff
    blockf = lax.broadcasted_iota(jnp.int32, (NB_PAD, R), 0).astype(F32)
    past = blockf < own
    km = km_ref[0:NB_PAD, :]
    spread_row = lax.broadcasted_iota(jnp.int32, (NB_PAD, LANES), 0)
    for i, q_ref in enumerate((q0_ref, q1_ref, q2_ref, q3_ref)):
        q = q_ref[0] * (HEAD_DIM ** -0.5 * LOG2E)
        qs = jnp.concatenate([jnp.where(_head_lane_mask(tq, GROUP_LANES, c), q, 0.0) for c in range(cpg)], axis=0)
        gate = _dot3_nt(km, *_split2(qs))
        sel_bf = _bf(_top_blocks(jnp.where(past, gate, NEG_INF), past, blockf, axis=0))
        slope = _moba_slopes(R, tq, i, g * cpg) * LOG2E
        b0 = slope * koff.astype(F32)
        slope_rep = jnp.broadcast_to(slope, (R, LANES))
        sq_rep = slope_rep * qpos
        own_term = b0 + slope * ((own * blk).astype(F32) - qpos)

        def add_past(n, s, b0=b0, sel_bf=sel_bf, slope_rep=slope_rep, sq_rep=sq_rep):
            chosen = _dot_tn(sel_bf, (spread_row == n).astype(BF16)) > 0.5
            rt = jnp.where(chosen, slope_rep * (n * blk).astype(F32) - sq_rep, NEG_INF)
            return s + b0 + jnp.concatenate([rt] * (blk // LANES), axis=1)

        def add_own(s, own_term=own_term):
            return jnp.where(causal, s + own_term, NEG_INF)

        o = _two_pass_attend(_bf(qs), own, add_past, pl.multiple_of(own * blk, blk), add_own,
                             kb_ref, vb_ref, s_ref, acc_ref, l_ref, m_ref)
        out = jnp.zeros((tq, GROUP_LANES), F32)
        for c in range(cpg):
            out = out + jnp.where(_head_lane_mask(tq, GROUP_LANES, c), o[c * tq:(c + 1) * tq, :], 0.0)
        o_ref[i] = _bf(out)


def moba_prompt(p1, B, T):
    tq = MOBA_BLOCK
    nq = T // tq
    nb = T // MOBA_BLOCK
    assert nb <= NB_PAD
    cpg = GROUP_LANES // HEAD_DIM
    G = C_KV_WIDTH // GROUP_LANES
    R = cpg * tq
    qspec = lambda i: pl.BlockSpec((1, tq, GROUP_LANES), lambda b, g, qi, i=i: (i, b * nq + qi, g))
    return pl.pallas_call(
        functools.partial(_moba_prompt_kernel, tq=tq, nb=nb, cpg=cpg),
        grid=(B, G, nq),
        in_specs=[qspec(0), qspec(1), qspec(2), qspec(3),
                  pl.BlockSpec((1, T, GROUP_LANES), lambda b, g, qi: (4, b, g)),
                  pl.BlockSpec((1, T, GROUP_LANES), lambda b, g, qi: (5, b, g))],
        out_specs=pl.BlockSpec((C_REP, tq, GROUP_LANES), lambda b, g, qi: (0, b * nq + qi, g)),
        out_shape=jax.ShapeDtypeStruct((C_REP, B * T, C_KV_WIDTH), BF16),
        scratch_shapes=[pltpu.VMEM((T, GROUP_LANES), BF16),
                        pltpu.VMEM((T, GROUP_LANES), BF16),
                        pltpu.VMEM((LANES, GROUP_LANES), F32)] + _attend_scratch(nb - 1, R),
        compiler_params=_cparams("parallel", "parallel", "arbitrary"),
        name="moba_prompt",
    )(p1, p1, p1, p1, p1, p1)


def _block_diag_rows(x, n_heads):
    lane = lax.broadcasted_iota(jnp.int32, x.shape, 1) // HEAD_DIM
    return jnp.concatenate([jnp.where(lane == h, x, 0.0) for h in range(n_heads)], axis=0)


def _gather_heads(out, n_heads, T):
    lane = lax.broadcasted_iota(jnp.int32, (T, out.shape[1]), 1) // HEAD_DIM
    y = jnp.zeros((T, out.shape[1]), F32)
    for h in range(n_heads):
        y = y + jnp.where(lane == h, out[h * T:(h + 1) * T, :], 0.0)
    return y


def _pad_rows(x, n):
    return jnp.concatenate([x, jnp.zeros((n - x.shape[0], x.shape[1]), x.dtype)], axis=0)


def _softmax_pv(s_list, s_new, v_refs, v_new, width):
    m = jnp.max(s_new, axis=1, keepdims=True)
    for s in s_list:
        m = jnp.maximum(m, jnp.max(s, axis=1, keepdims=True))
    p = jnp.exp(s_new - m)
    l = jnp.sum(p, axis=1, keepdims=True)
    acc = jnp.dot(_bf(p), v_new, preferred_element_type=F32)
    for s, v_ref in zip(s_list, v_refs):
        p = jnp.exp(s - m)
        l = l + jnp.sum(p, axis=1, keepdims=True)
        acc = acc + _dot_nt(p, v_ref[0].reshape(width, v_ref.shape[-1]))
    return acc / l


def _fox_decode_kernel(pt_ref, q_ref, kn_ref, vn_ref, lfn_ref, *rest, T, NP):
    k_refs = rest[0:NP]
    v_refs = rest[NP:2 * NP]
    lf_refs = rest[2 * NP:3 * NP]
    o_ref = rest[3 * NP]
    H, W = B_HEADS, B_WIDTH
    R = H * T
    page = k_refs[0].shape[-1]
    qbd = _bf(_block_diag_rows(q_ref[0] * (HEAD_DIM ** -0.5), H))
    ti = lax.broadcasted_iota(jnp.int32, (page, page), 0)
    tj = lax.broadcasted_iota(jnp.int32, (page, page), 1)
    triu = (ti <= tj).astype(BF16)

    def cumsum(x, carry):
        hi, mid, lo = _split3(x)
        cs = (jnp.dot(hi, triu, preferred_element_type=F32)
              + (jnp.dot(mid, triu, preferred_element_type=F32)
                 + jnp.dot(lo, triu, preferred_element_type=F32))) + carry
        return cs, jnp.broadcast_to(cs[:, page - 1:page], cs.shape)

    def per_row(c):
        return jnp.broadcast_to(c[:, None, :], (H, T, page)).reshape(R, page)

    carry = jnp.zeros((H, page), F32)
    s_list = []
    for j in range(NP):
        cs, carry = cumsum(lf_refs[j][0], carry)
        s = jnp.dot(qbd, _bf(k_refs[j][0].reshape(W, page)), preferred_element_type=F32)
        s_list.append(s - per_row(cs))
    cs_new, _ = cumsum(lfn_ref[0], carry)
    row_tok = lax.broadcasted_iota(jnp.int32, (R, page), 0) % T
    key = lax.broadcasted_iota(jnp.int32, (R, page), 1)
    s_new = _dot_nt(qbd, _pad_rows(kn_ref[0], page)) - per_row(cs_new)
    s_new = jnp.where(key <= row_tok, s_new, NEG_INF)
    out = _softmax_pv(s_list, s_new, v_refs, _bf(_pad_rows(vn_ref[0], page)), W)
    o_ref[0] = _bf(_gather_heads(out, H, T))


def fox_decode(pab, lfn_t, cache_k, cache_v, cache_lf, page_table, T):
    B, NP = page_table.shape
    W = B_WIDTH
    page4 = lambda j: (lambda b, pt: (pt[b, j], 0, 0, 0))
    page3 = lambda j: (lambda b, pt: (pt[b, j], 0, 0))
    in_specs = [pl.BlockSpec((1, T, W), lambda b, pt: (1, b, 0)),
                pl.BlockSpec((1, T, W), lambda b, pt: (1, b, 1)),
                pl.BlockSpec((1, T, W), lambda b, pt: (1, b, 2)),
                pl.BlockSpec((1, *lfn_t.shape[1:]), lambda b, pt: (b, 0, 0))]
    in_specs += [pl.BlockSpec((1, *cache_k.shape[1:]), page4(j)) for j in range(NP)] * 2
    in_specs += [pl.BlockSpec((1, *cache_lf.shape[1:]), page3(j)) for j in range(NP)]
    gs = pltpu.PrefetchScalarGridSpec(
        num_scalar_prefetch=1, grid=(B,), in_specs=in_specs,
        out_specs=pl.BlockSpec((1, T, W), lambda b, pt: (b, 0, 0)))
    return pl.pallas_call(
        functools.partial(_fox_decode_kernel, T=T, NP=NP),
        grid_spec=gs,
        out_shape=jax.ShapeDtypeStruct((B, T, W), BF16),
        compiler_params=_cparams("arbitrary"),
        name="fox_decode",
    )(page_table, pab, pab, pab, lfn_t, *([cache_k] * NP), *([cache_v] * NP), *([cache_lf] * NP))


def _moba_decode_kernel(pt_ref, q_ref, kn_ref, vn_ref, *rest, T, NP, q_start):
    k_refs = rest[0:NP]
    v_refs = rest[NP:2 * NP]
    o_ref = rest[2 * NP]
    Hkv, W = C_KV_HEADS, C_KV_WIDTH
    page = k_refs[0].shape[-1]
    ppb = MOBA_BLOCK // page
    nbp = NP // ppb
    own = q_start // MOBA_BLOCK
    RG = Hkv * T
    R = C_REP * RG

    qbd = jnp.concatenate([_block_diag_rows(q_ref[i] * (HEAD_DIM ** -0.5), Hkv) for i in range(C_REP)], axis=0)
    qbd_bf = _bf(qbd)
    rowi = lax.broadcasted_iota(jnp.int32, (R, 1), 0)
    row_tok = rowi % T
    head = C_REP * ((rowi // T) % Hkv) + rowi // RG
    slope = jnp.exp2(-8.0 * (head + 1).astype(F32) / C_HEADS)
    qpos = (q_start + row_tok).astype(F32)
    lane = lax.broadcasted_iota(jnp.int32, (R, page), 1)
    lanef = lane.astype(F32)

    kps = [k_refs[j][0].reshape(W, page) for j in range(NP)]
    col = lax.broadcasted_iota(jnp.int32, (W, page), 1)
    km = jnp.zeros((W, page), F32)
    for n in range(nbp):
        tot = jnp.sum(kps[n * ppb], axis=1, keepdims=True)
        for j in range(1, ppb):
            tot = tot + jnp.sum(kps[n * ppb + j], axis=1, keepdims=True)
        km = jnp.where(col == n, tot * (1.0 / MOBA_BLOCK), km)
    live = lane < min(own, nbp)
    sel = _top_blocks(jnp.where(live, _dot3(qbd, *_split2(km)), NEG_INF), live, lanef)

    b0 = slope * lanef
    s_list = []
    for n in range(nbp):
        chosen = jnp.max(jnp.where(lane == n, sel, 0.0), axis=1, keepdims=True) > 0.5
        for j in range(n * ppb, (n + 1) * ppb):
            row_term = jnp.where(chosen, slope * (j * page - qpos), NEG_INF)
            s_list.append(jnp.dot(qbd_bf, _bf(kps[j]), preferred_element_type=F32) + b0 + row_term)
    s_new = _dot_nt(qbd_bf, _pad_rows(kn_ref[0], page)) - slope * (row_tok - lane).astype(F32)
    s_new = jnp.where(lane <= row_tok, s_new, NEG_INF)
    out = _softmax_pv(s_list, s_new, v_refs, _bf(_pad_rows(vn_ref[0], page)), W)
    for i in range(C_REP):
        o_ref[i] = _bf(_gather_heads(out[i * RG:(i + 1) * RG, :], Hkv, T))


def moba_decode(p1, cache_k, cache_v, page_table, T, q_start):
    B, NP = page_table.shape
    W = C_KV_WIDTH
    page4 = lambda j: (lambda b, pt: (pt[b, j], 0, 0, 0))
    in_specs = [pl.BlockSpec((C_REP, T, W), lambda b, pt: (0, b, 0)),
                pl.BlockSpec((1, T, W), lambda b, pt: (C_REP, b, 0)),
                pl.BlockSpec((1, T, W), lambda b, pt: (C_REP + 1, b, 0))]
    in_specs += [pl.BlockSpec((1, *cache_k.shape[1:]), page4(j)) for j in range(NP)] * 2
    gs = pltpu.PrefetchScalarGridSpec(
        num_scalar_prefetch=1, grid=(B,), in_specs=in_specs,
        out_specs=pl.BlockSpec((C_REP, T, W), lambda b, pt: (0, b, 0)))
    return pl.pallas_call(
        functools.partial(_moba_decode_kernel, T=T, NP=NP, q_start=q_start),
        grid_spec=gs,
        out_shape=jax.ShapeDtypeStruct((C_REP, B * T, W), BF16),
        compiler_params=_cparams("arbitrary"),
        name="moba_decode",
    )(page_table, p1, p1, p1, *([cache_k] * NP), *([cache_v] * NP))


def _row(v):
    return v.reshape(1, -1).astype(F32)


def _prep_params(norm0_mix_g, w_in0, fox_b_f, rwkv_mu, rwkv_w0, rwkv_w2, rwkv_a0, rwkv_a2, rwkv_g2, rwkv_k_k,
                 rwkv_k_a, rwkv_r_k, rwkv_ln_w, rwkv_ln_b, w_out0, norm0_ffn_g, ffn_w_gate, ffn_w_up,
                 ffn_w_down, norm1_mix_g, w_in1, w_out1, norm1_ffn_g, router_w, router_b, moe_w_gate,
                 moe_w_up, moe_w_down, norm_final_g):
    D = w_in0.shape[0]
    pad_b = A_PROJ - (3 * B_WIDTH + B_HEADS)
    w0 = jnp.concatenate([w_in0, jnp.zeros((D, pad_b), F32)], axis=1)
    wq = w_in1[:, :C_WIDTH].reshape(D, C_KV_HEADS, C_REP, HEAD_DIM).transpose(0, 2, 1, 3).reshape(D, C_WIDTH)
    w1 = jnp.concatenate([wq, w_in1[:, C_WIDTH:]], axis=1)
    wo1 = w_out1.reshape(C_KV_HEADS, C_REP, HEAD_DIM, D).transpose(1, 0, 2, 3).reshape(C_WIDTH, D)
    hd = lax.broadcasted_iota(jnp.int32, (LANES, LANES), 0) // HEAD_DIM
    hd2 = lax.broadcasted_iota(jnp.int32, (LANES, LANES), 1) // HEAD_DIM
    rw = jnp.concatenate([router_w.T, jnp.zeros((E_PAD - N_EXPERTS, D), F32)], axis=0)
    rb = jnp.concatenate([router_b, jnp.zeros((E_PAD - N_EXPERTS,), F32)])
    return dict(
        g0=norm0_mix_g, w0=_bf(w0),
        fox_b=jnp.concatenate([fox_b_f, jnp.zeros((LANES - B_HEADS,), F32)]).reshape(1, LANES),
        rwkv=dict(mu=_row(rwkv_mu), w0=_row(rwkv_w0), w2=_split2(rwkv_w2), a0=_row(rwkv_a0), a2=_split2(rwkv_a2),
                  g2=_split2(rwkv_g2), k_k=_row(rwkv_k_k), k_a=_row(rwkv_k_a), r_k=_row(rwkv_r_k),
                  ln_w=_row(rwkv_ln_w), ln_b=_row(rwkv_ln_b), bd=(hd == hd2).astype(BF16)),
        wo0=_bf(w_out0), g0f=norm0_ffn_g,
        ffn_g=_bf(ffn_w_gate), ffn_u=_bf(ffn_w_up), ffn_d=_bf(ffn_w_down),
        g1=norm1_mix_g, w1=_bf(w1), wo1=_bf(wo1), g1f=norm1_ffn_g,
        router=(*_split2(rw), rb.reshape(E_PAD, 1)),
        moe_g=_bf(moe_w_gate), moe_u=_bf(moe_w_up), moe_d=_bf(moe_w_down),
        gf=norm_final_g)


def _pair_states(S):
    B = S.shape[0]
    S = S.reshape(B, A_HEADS // 2, 2, HEAD_DIM, HEAD_DIM)
    z = jnp.zeros_like(S[:, :, 0])
    top = jnp.concatenate([S[:, :, 0], z], axis=-1)
    bot = jnp.concatenate([z, S[:, :, 1]], axis=-1)
    return jnp.concatenate([top, bot], axis=-2)


def _unpair_states(S):
    B = S.shape[0]
    a = S[:, :, :HEAD_DIM, :HEAD_DIM]
    b = S[:, :, HEAD_DIM:, HEAD_DIM:]
    return jnp.stack([a, b], axis=2).reshape(B, A_HEADS, HEAD_DIM, HEAD_DIM)


def _run(P, x, S0, shift0, caches, page_table):
    B, T, D = x.shape
    M = B * T
    xt = x.reshape(M, D)

    pab = norm_matmul(xt, P["g0"], P["w0"], A_PROJ, A_PROJ // 2)
    r, lw, k, v, kn, bb, gg = rwkv_prep(pab, shift0, P["rwkv"], T)
    s_in = jnp.zeros((B, A_HEADS // 2, LANES, LANES), F32) if S0 is None else _pair_states(S0)
    y, s_bd = rwkv_chunk(r, lw, k, v, kn, bb, s_in, B, T)
    ya = rwkv_post(y, r, k, v, gg, P["rwkv"])
    S_new = _unpair_states(s_bd)
    shift_new = lax.slice(pab, (0, T - 1, 0), (1, M, A_PROJ), (1, T, 1)).reshape(B, A_PROJ)

    cols = lambda c: lax.slice(pab, (1, 0, c * B_WIDTH), (2, M, (c + 1) * B_WIDTH))
    fk = cols(1).reshape(B, T, B_HEADS, HEAD_DIM)
    fv = cols(2).reshape(B, T, B_HEADS, HEAD_DIM)
    logf = fox_logf(pab, P["fox_b"], 3 * B_WIDTH // LANES)[:, :B_HEADS].reshape(B, T, B_HEADS)
    if caches is None:
        yb = fox_prompt(pab, cumsum_lanes(jnp.swapaxes(logf, 1, 2)), B, T)
    else:
        page = caches["fox_k"].shape[1]
        by_head = lambda c: jnp.transpose(c, (0, 2, 3, 1))
        lfn_t = jnp.pad(jnp.swapaxes(logf, 1, 2), ((0, 0), (0, 0), (0, page - T)))
        yb = fox_decode(pab, lfn_t, by_head(caches["fox_k"]), by_head(caches["fox_v"]),
                        jnp.swapaxes(caches["fox_logf"], 1, 2), page_table, T).reshape(M, B_WIDTH)
    yab = jnp.concatenate([ya, yb], axis=1)[None]
    x1 = matmul_res(yab, P["wo0"], xt, tk=A_WIDTH + B_WIDTH)

    h = swiglu_up(x1, P["g0f"], P["ffn_g"], P["ffn_u"])
    x2 = matmul_res(h[None], P["ffn_d"], x1, tk=_pick(h.shape[1], (1408, 1024, 512, 256, 128)))

    p1 = norm_matmul(x2, P["g1"], P["w1"], C_KV_WIDTH, C_KV_WIDTH)
    mk = p1[4].reshape(B, T, C_KV_HEADS, HEAD_DIM)
    mv = p1[5].reshape(B, T, C_KV_HEADS, HEAD_DIM)
    if caches is None:
        y1 = moba_prompt(p1, B, T)
    else:
        page = caches["moba_k"].shape[1]
        by_head = lambda c: jnp.transpose(c, (0, 2, 3, 1))
        y1 = moba_decode(p1, by_head(caches["moba_k"]), by_head(caches["moba_v"]), page_table, T,
                         page_table.shape[1] * page)
    x3 = matmul_res(y1, P["wo1"], x2, tk=C_KV_WIDTH)

    xn, rows, cols, meta = moe_route(x3, P["g1f"], P["router"])
    off = meta[:, :N_EXPERTS, 0].astype(jnp.int32)
    run = meta[:, :N_EXPERTS, 1].astype(jnp.int32)
    hs = moe_up_sparse(xn, rows, off, run, P["moe_g"], P["moe_u"])
    out = rmsnorm(moe_down_sparse(hs, cols, rows, off, run, P["moe_d"], x3), P["gf"])
    return out.reshape(B, T, D), S_new, shift_new, fk, fv, logf, mk, mv


def kernel(x_prompt, x_sample, state_rwkv_S, state_rwkv_shift, cache_fox_k, cache_fox_v, cache_fox_logf,
           cache_moba_k, cache_moba_v, page_table, norm0_mix_g, w_in0, fox_b_f, rwkv_mu, rwkv_w0, rwkv_w2,
           rwkv_a0, rwkv_a2, rwkv_g2, rwkv_k_k, rwkv_k_a, rwkv_r_k, rwkv_ln_w, rwkv_ln_b, w_out0, norm0_ffn_g,
           ffn_w_gate, ffn_w_up, ffn_w_down, norm1_mix_g, w_in1, w_out1, norm1_ffn_g, router_w, router_b,
           moe_w_gate, moe_w_up, moe_w_down, norm_final_g):
    P = _prep_params(norm0_mix_g, w_in0, fox_b_f, rwkv_mu, rwkv_w0, rwkv_w2, rwkv_a0, rwkv_a2, rwkv_g2,
                     rwkv_k_k, rwkv_k_a, rwkv_r_k, rwkv_ln_w, rwkv_ln_b, w_out0, norm0_ffn_g, ffn_w_gate,
                     ffn_w_up, ffn_w_down, norm1_mix_g, w_in1, w_out1, norm1_ffn_g, router_w, router_b,
                     moe_w_gate, moe_w_up, moe_w_down, norm_final_g)
    n_prompt = x_prompt.shape[0]
    prompt = _run(P, x_prompt, None, jnp.zeros((n_prompt, A_PROJ), x_prompt.dtype), None, None)
    caches = dict(fox_k=cache_fox_k, fox_v=cache_fox_v, fox_logf=cache_fox_logf,
                  moba_k=cache_moba_k, moba_v=cache_moba_v)
    sample = _run(P, x_sample, state_rwkv_S, state_rwkv_shift, caches, page_table)
    return (prompt[0], sample[0], *prompt[1:], *sample[1:])
```

```python
import functools

import jax
import jax.numpy as jnp
from jax import lax
from jax.experimental import pallas as pl
from jax.experimental.pallas import tpu as pltpu

F32 = jnp.float32
BF16 = jnp.bfloat16

HEAD_DIM = 64
A_HEADS = 16
A_WIDTH = A_HEADS * HEAD_DIM
A_LORA_W = 64
A_LORA_A = 64
A_LORA_G = 128
A_PROJ = 3 * A_WIDTH + A_LORA_W + A_LORA_A + A_LORA_G
RWKV_GN_EPS = 64e-5
B_HEADS = 16
B_WIDTH = B_HEADS * HEAD_DIM
C_HEADS = 32
C_KV_HEADS = 8
C_REP = C_HEADS // C_KV_HEADS
C_WIDTH = C_HEADS * HEAD_DIM
C_KV_WIDTH = C_KV_HEADS * HEAD_DIM
MOBA_BLOCK = 256
MOBA_TOPK = 3
N_EXPERTS = 8
NORM_EPS = 1e-6

LANES = 128
GROUP_LANES = 256
VMEM_LIMIT = 56 * 1024 * 1024
NEG_INF = float("-inf")
LOG2E = 1.4426950408889634
NB_PAD = 16
NT_DIMS = (((1,), (1,)), ((), ()))


def _cparams(*sem):
    return pltpu.CompilerParams(dimension_semantics=sem, vmem_limit_bytes=VMEM_LIMIT)


def _bf(x):
    return x.astype(BF16)


def _dot(a, b):
    return jnp.dot(_bf(a), _bf(b), preferred_element_type=F32)


def _dot_nt(a, b):
    return lax.dot_general(_bf(a), _bf(b), NT_DIMS, preferred_element_type=F32)


def _dot_tn(a, b):
    return lax.dot_general(_bf(a), _bf(b), (((0,), (0,)), ((), ())), preferred_element_type=F32)


def _split2(x):
    hi = _bf(x)
    lo = _bf(x - hi.astype(F32))
    return hi, lo


def _split3(x):
    hi = _bf(x)
    r1 = x - hi.astype(F32)
    mid = _bf(r1)
    lo = _bf(r1 - mid.astype(F32))
    return hi, mid, lo


def _dot3(a, b_hi, b_lo):
    a_hi, a_lo = _split2(a)
    return (jnp.dot(a_hi, b_hi, preferred_element_type=F32)
            + (jnp.dot(a_hi, b_lo, preferred_element_type=F32)
               + jnp.dot(a_lo, b_hi, preferred_element_type=F32)))


def _dot3_nt(a, b_hi, b_lo):
    a_hi, a_lo = _split2(a)
    return (lax.dot_general(a_hi, b_hi, NT_DIMS, preferred_element_type=F32)
            + (lax.dot_general(a_hi, b_lo, NT_DIMS, preferred_element_type=F32)
               + lax.dot_general(a_lo, b_hi, NT_DIMS, preferred_element_type=F32)))


def _head_sums(x, pair_ones):
    hi, lo = _split2(x)
    parts = []
    for c in range(x.shape[1] // LANES):
        sl = slice(c * LANES, (c + 1) * LANES)
        parts.append(jnp.dot(hi[:, sl], pair_ones, preferred_element_type=F32)
                     + jnp.dot(lo[:, sl], pair_ones, preferred_element_type=F32))
    return jnp.concatenate(parts, axis=1)


def _dot_exact_lhs3(a_exact, x):
    hi, mid, lo = _split3(x)
    return (jnp.dot(a_exact, hi, preferred_element_type=F32)
            + (jnp.dot(a_exact, mid, preferred_element_type=F32)
               + jnp.dot(a_exact, lo, preferred_element_type=F32)))


def _sigmoid(x):
    return 1.0 / (1.0 + jnp.exp(-x))


def _pick(n, prefs):
    for p in prefs:
        if n % p == 0:
            return p
    return n


def _norm_mm_kernel(x_ref, g_ref, w_ref, o_ref, xn_ref):
    @pl.when(pl.program_id(1) == 0)
    def _():
        x = x_ref[...]
        ms = jnp.mean(x * x, axis=-1, keepdims=True)
        xn_ref[...] = _bf(x * lax.rsqrt(ms + NORM_EPS) * g_ref[...])

    o_ref[0] = jnp.dot(xn_ref[...], w_ref[...], preferred_element_type=F32)


def norm_matmul(x, g, w, cw, tn):
    M, K = x.shape
    N = w.shape[1]
    nc = N // cw
    tm = _pick(M, (1024, 512, 256, 128, 8))
    per = cw // tn
    return pl.pallas_call(
        _norm_mm_kernel,
        grid=(M // tm, N // tn),
        in_specs=[pl.BlockSpec((tm, K), lambda i, j: (i, 0)),
                  pl.BlockSpec((1, K), lambda i, j: (0, 0)),
                  pl.BlockSpec((K, tn), lambda i, j: (0, j))],
        out_specs=pl.BlockSpec((1, tm, tn), lambda i, j: (j // per, i, j % per)),
        out_shape=jax.ShapeDtypeStruct((nc, M, cw), F32),
        scratch_shapes=[pltpu.VMEM((tm, K), BF16)],
        compiler_params=_cparams("parallel", "arbitrary"),
        name="norm_matmul",
    )(x, g.reshape(1, K), w)


def _mm_res_kernel(a_ref, w_ref, r_ref, o_ref, acc_ref, *, nk):
    k = pl.program_id(2)

    @pl.when(k == 0)
    def _():
        acc_ref[...] = r_ref[...]

    acc_ref[...] += jnp.dot(a_ref[0], w_ref[...], preferred_element_type=F32)

    @pl.when(k == nk - 1)
    def _():
        o_ref[...] = acc_ref[...]


def _mm_res_full_kernel(a_ref, w_ref, r_ref, o_ref):
    ka, _, kw = a_ref.shape
    acc = r_ref[...]
    for c in range(ka):
        acc = acc + jnp.dot(a_ref[c], w_ref[c * kw:(c + 1) * kw, :], preferred_element_type=F32)
    o_ref[...] = acc


def matmul_res(a, w, res, tk):
    ka, M, kw = a.shape
    N = w.shape[1]
    tm = _pick(M, (512, 256, 128, 8))
    tn = _pick(N, (1024, 512, 256, 128))
    per = kw // tk
    nk = ka * per
    if per == 1:
        return pl.pallas_call(
            _mm_res_full_kernel,
            grid=(M // tm, N // tn),
            in_specs=[pl.BlockSpec((ka, tm, kw), lambda i, j: (0, i, 0)),
                      pl.BlockSpec((ka * kw, tn), lambda i, j: (0, j)),
                      pl.BlockSpec((tm, tn), lambda i, j: (i, j))],
            out_specs=pl.BlockSpec((tm, tn), lambda i, j: (i, j)),
            out_shape=jax.ShapeDtypeStruct((M, N), F32),
            compiler_params=_cparams("parallel", "parallel"),
            name="matmul_res_full",
        )(a, w, res)
    return pl.pallas_call(
        functools.partial(_mm_res_kernel, nk=nk),
        grid=(M // tm, N // tn, nk),
        in_specs=[pl.BlockSpec((1, tm, tk), lambda i, j, k: (k // per, i, k % per)),
                  pl.BlockSpec((tk, tn), lambda i, j, k: (k, j)),
                  pl.BlockSpec((tm, tn), lambda i, j, k: (i, j))],
        out_specs=pl.BlockSpec((tm, tn), lambda i, j, k: (i, j)),
        out_shape=jax.ShapeDtypeStruct((M, N), F32),
        scratch_shapes=[pltpu.VMEM((tm, tn), F32)],
        compiler_params=_cparams("parallel", "parallel", "arbitrary"),
        name="matmul_res",
    )(a, w, res)


def _swiglu_up_kernel(x_ref, g_ref, wg_ref, wu_ref, h_ref, xn_ref):
    @pl.when(pl.program_id(1) == 0)
    def _():
        x = x_ref[...]
        ms = jnp.mean(x * x, axis=-1, keepdims=True)
        xn_ref[...] = _bf(x * lax.rsqrt(ms + NORM_EPS) * g_ref[...])

    xn = xn_ref[...]
    a = jnp.dot(xn, wg_ref[...], preferred_element_type=F32)
    b = jnp.dot(xn, wu_ref[...], preferred_element_type=F32)
    h_ref[...] = _bf(a * _sigmoid(a) * b)


def swiglu_up(x, g, wg, wu):
    M, K = x.shape
    F = wg.shape[1]
    tm = _pick(M, (1024, 512, 256, 128, 8))
    tn = _pick(F, (512, 256, 128))
    return pl.pallas_call(
        _swiglu_up_kernel,
        grid=(M // tm, F // tn),
        in_specs=[pl.BlockSpec((tm, K), lambda i, j: (i, 0)),
                  pl.BlockSpec((1, K), lambda i, j: (0, 0)),
                  pl.BlockSpec((K, tn), lambda i, j: (0, j)),
                  pl.BlockSpec((K, tn), lambda i, j: (0, j))],
        out_specs=pl.BlockSpec((tm, tn), lambda i, j: (i, j)),
        out_shape=jax.ShapeDtypeStruct((M, F), BF16),
        scratch_shapes=[pltpu.VMEM((tm, K), BF16)],
        compiler_params=_cparams("parallel", "arbitrary"),
        name="ffn_up",
    )(x, g.reshape(1, K), wg, wu)


MOE_RUN = 128
E_PAD = 16
TOP_K = 2


def _moe_cap(TM):
    return TOP_K * TM + N_EXPERTS * MOE_RUN


def _run_sizes(TM):
    top = -(-TM // MOE_RUN) * MOE_RUN
    sizes, s = [], MOE_RUN
    while s <= top:
        sizes.append(s)
        s *= 2
    return tuple(reversed(sizes))


def _moe_route_kernel(x_ref, g_ref, rwh_ref, rwl_ref, rb_ref, xn_ref, rows_ref, cols_ref, meta_ref):
    x = x_ref[...]
    TM = x.shape[0]
    ms = jnp.mean(x * x, axis=-1, keepdims=True)
    xn = x * lax.rsqrt(ms + NORM_EPS) * g_ref[...]
    xn_ref[...] = _bf(xn)
    hi, lo = _split2(xn)
    rwh = rwh_ref[...]
    logits = (lax.dot_general(rwh, hi, NT_DIMS, preferred_element_type=F32)
              + (lax.dot_general(rwh, lo, NT_DIMS, preferred_element_type=F32)
                 + lax.dot_general(rwl_ref[...], hi, NT_DIMS, preferred_element_type=F32))) + rb_ref[...]
    ef = lax.broadcasted_iota(jnp.int32, (E_PAD, TM), 0).astype(F32)
    z = jnp.where(ef < N_EXPERTS, logits, NEG_INF)
    m1 = jnp.max(z, axis=0, keepdims=True)
    i1 = jnp.min(jnp.where(z == m1, ef, float(E_PAD)), axis=0, keepdims=True)
    z2 = jnp.where(ef == i1, NEG_INF, z)
    m2 = jnp.max(z2, axis=0, keepdims=True)
    i2 = jnp.min(jnp.where(z2 == m2, ef, float(E_PAD)), axis=0, keepdims=True)
    e2 = jnp.exp(m2 - m1)
    den = 1.0 + e2
    member = jnp.where((ef == i1) | (ef == i2), 1.0, 0.0)
    before = (lax.broadcasted_iota(jnp.int32, (TM, TM), 0) < lax.broadcasted_iota(jnp.int32, (TM, TM), 1))
    rank = jnp.dot(_bf(member), before.astype(BF16), preferred_element_type=F32)
    cnt = jnp.sum(member, axis=1, keepdims=True)
    run = jnp.floor((cnt + (MOE_RUN - 1)) * (1.0 / MOE_RUN)) * MOE_RUN
    run_b = jnp.broadcast_to(run, (E_PAD, LANES))
    lower = (lax.broadcasted_iota(jnp.int32, (E_PAD, E_PAD), 0)
             > lax.broadcasted_iota(jnp.int32, (E_PAD, E_PAD), 1)).astype(BF16)
    off = jnp.dot(lower, _bf(run_b), preferred_element_type=F32)
    slot = off[:, 0:1] + rank
    slot1 = jnp.sum(jnp.where(ef == i1, slot, 0.0), axis=0, keepdims=True)
    slot2 = jnp.sum(jnp.where(ef == i2, slot, 0.0), axis=0, keepdims=True)
    info = jnp.concatenate([slot1, slot2, 1.0 / den, e2 / den, jnp.zeros((E_PAD - 4, TM), F32)], axis=0)
    rows_ref[0] = info
    ident = (lax.broadcasted_iota(jnp.int32, (E_PAD, LANES), 0)
             == lax.broadcasted_iota(jnp.int32, (E_PAD, LANES), 1)).astype(BF16)
    tn_dims = (((0,), (0,)), ((), ()))
    h3, m3, l3 = _split3(info)
    cols_ref[0] = (lax.dot_general(h3, ident, tn_dims, preferred_element_type=F32)
                   + (lax.dot_general(m3, ident, tn_dims, preferred_element_type=F32)
                      + lax.dot_general(l3, ident, tn_dims, preferred_element_type=F32)))
    lane = lax.broadcasted_iota(jnp.int32, (E_PAD, LANES), 1)
    meta_ref[0] = jnp.where(lane == 0, off, jnp.where(lane == 1, run_b, 0.0))


def moe_route(x, g, router):
    M, D = x.shape
    TM = _pick(M, (1024, 512, 256, 128, 64))
    NT = M // TM
    rwh, rwl, rb = router
    full = lambda a: pl.BlockSpec(a.shape, lambda i: (0,) * a.ndim)
    return pl.pallas_call(
        _moe_route_kernel,
        grid=(NT,),
        in_specs=[pl.BlockSpec((TM, D), lambda i: (i, 0)), pl.BlockSpec((1, D), lambda i: (0, 0)),
                  full(rwh), full(rwl), full(rb)],
        out_specs=[pl.BlockSpec((TM, D), lambda i: (i, 0)),
                   pl.BlockSpec((1, E_PAD, TM), lambda i: (i, 0, 0)),
                   pl.BlockSpec((1, TM, LANES), lambda i: (i, 0, 0)),
                   pl.BlockSpec((1, E_PAD, LANES), lambda i: (i, 0, 0))],
        out_shape=[jax.ShapeDtypeStruct((M, D), BF16),
                   jax.ShapeDtypeStruct((NT, E_PAD, TM), F32),
                   jax.ShapeDtypeStruct((NT, TM, LANES), F32),
                   jax.ShapeDtypeStruct((NT, E_PAD, LANES), F32)],
        compiler_params=_cparams("parallel"),
        name="moe_route",
    )(x, g.reshape(1, D), rwh, rwl, rb)


def _for_each_run_chunk(off_ref, run_ref, e, sizes, body):
    i = pl.program_id(0)
    base = off_ref[i, e]
    n = run_ref[i, e]
    for size in sizes:
        @pl.when((n & size) != 0)
        def _(size=size):
            body(pl.multiple_of(base + (n & (-2 * size)), MOE_RUN), size)


def _moe_up_kernel(off_ref, run_ref, xn_ref, rows_ref, wg_ref, wu_ref, hs_ref, xs_ref, *, sizes, GC):
    j = pl.program_id(1)
    e = pl.program_id(2)
    TM = xn_ref.shape[0]
    CAP = xs_ref.shape[0]

    @pl.when((j == 0) & (e == 0))
    def _():
        s1 = rows_ref[0, 0:1, :]
        s2 = rows_ref[0, 1:2, :]
        xnb = xn_ref[...]
        for c in range(CAP // GC):
            srow = (c * GC + lax.broadcasted_iota(jnp.int32, (GC, TM), 0)).astype(F32)
            onehot = jnp.where((srow == s1) | (srow == s2), 1.0, 0.0)
            xs_ref[c * GC:(c + 1) * GC, :] = _bf(jnp.dot(_bf(onehot), xnb, preferred_element_type=F32))

    @pl.when(e == 0)
    def _():
        hs_ref[...] = jnp.zeros_like(hs_ref)

    def body(start, size):
        xr = xs_ref[pl.ds(start, size), :]
        a = jnp.dot(xr, wg_ref[0], preferred_element_type=F32)
        b = jnp.dot(xr, wu_ref[0], preferred_element_type=F32)
        hs_ref[0, pl.ds(start, size), :] = _bf(a * _sigmoid(a) * b)

    _for_each_run_chunk(off_ref, run_ref, e, sizes, body)


def moe_up_sparse(xn, rows, off, run, wg, wu):
    M, D = xn.shape
    NT, _, TM = rows.shape
    E, _, F = wg.shape
    CAP = _moe_cap(TM)
    tn = _pick(F, (256, 128))
    gs = pltpu.PrefetchScalarGridSpec(
        num_scalar_prefetch=2, grid=(NT, F // tn, E),
        in_specs=[pl.BlockSpec((TM, D), lambda i, j, e, o, r: (i, 0)),
                  pl.BlockSpec((1, E_PAD, TM), lambda i, j, e, o, r: (i, 0, 0)),
                  pl.BlockSpec((1, D, tn), lambda i, j, e, o, r: (e, 0, j)),
                  pl.BlockSpec((1, D, tn), lambda i, j, e, o, r: (e, 0, j))],
        out_specs=pl.BlockSpec((1, CAP, tn), lambda i, j, e, o, r: (i, 0, j)),
        scratch_shapes=[pltpu.VMEM((CAP, D), BF16)])
    return pl.pallas_call(
        functools.partial(_moe_up_kernel, sizes=_run_sizes(TM), GC=_pick(CAP, (512, 256, 128))),
        grid_spec=gs,
        out_shape=jax.ShapeDtypeStruct((NT, CAP, F), BF16),
        compiler_params=_cparams("arbitrary", "arbitrary", "arbitrary"),
        name="moe_up",
    )(off, run, xn, rows, wg, wu)


def _moe_down_kernel(off_ref, run_ref, hs_ref, wd_ref, x_ref, cols_ref, rows_ref, o_ref, ys_ref, *,
                     sizes, GC, n_e, n_kf):
    kf = pl.program_id(2)
    e = pl.program_id(3)
    TM = x_ref.shape[0]
    CAP = ys_ref.shape[0]

    @pl.when((e == 0) & (kf == 0))
    def _():
        ys_ref[...] = jnp.zeros_like(ys_ref)

    def body(start, size):
        ys_ref[pl.ds(start, size), :] += jnp.dot(hs_ref[0, pl.ds(start, size), :], wd_ref[0],
                                                 preferred_element_type=F32)

    _for_each_run_chunk(off_ref, run_ref, e, sizes, body)

    @pl.when((e == n_e - 1) & (kf == n_kf - 1))
    def _():
        info = cols_ref[0]
        s1, s2 = info[:, 0:1], info[:, 1:2]
        r1, r2, p1, p2 = (rows_ref[0, k:k + 1, :] for k in range(4))
        acc = x_ref[...]
        for c in range(CAP // GC):
            srow = (c * GC + lax.broadcasted_iota(jnp.int32, (GC, TM), 0)).astype(F32)
            gate = jnp.sum(jnp.where(srow == r1, p1, 0.0) + jnp.where(srow == r2, p2, 0.0), axis=1, keepdims=True)
            scol = (c * GC + lax.broadcasted_iota(jnp.int32, (TM, GC), 1)).astype(F32)
            onehot = jnp.where((scol == s1) | (scol == s2), 1.0, 0.0)
            acc = acc + jnp.dot(_bf(onehot), _bf(ys_ref[c * GC:(c + 1) * GC, :] * gate), preferred_element_type=F32)
        o_ref[...] = acc


def moe_down_sparse(hs, cols, rows, off, run, wd, x):
    NT, CAP, F = hs.shape
    TM = cols.shape[1]
    E, _, D = wd.shape
    tn = _pick(D, (512, 256, 128))
    tk = _pick(F, (1408, 1024, 512, 256, 128))
    n_kf = F // tk
    gs = pltpu.PrefetchScalarGridSpec(
        num_scalar_prefetch=2, grid=(NT, D // tn, n_kf, E),
        in_specs=[pl.BlockSpec((1, CAP, tk), lambda i, n, k, e, o, r: (i, 0, k)),
                  pl.BlockSpec((1, tk, tn), lambda i, n, k, e, o, r: (e, k, n)),
                  pl.BlockSpec((TM, tn), lambda i, n, k, e, o, r: (i, n)),
                  pl.BlockSpec((1, TM, LANES), lambda i, n, k, e, o, r: (i, 0, 0)),
                  pl.BlockSpec((1, E_PAD, TM), lambda i, n, k, e, o, r: (i, 0, 0))],
        out_specs=pl.BlockSpec((TM, tn), lambda i, n, k, e, o, r: (i, n)),
        scratch_shapes=[pltpu.VMEM((CAP, tn), F32)])
    return pl.pallas_call(
        functools.partial(_moe_down_kernel, sizes=_run_sizes(TM), GC=_pick(CAP, (512, 256, 128)), n_e=E, n_kf=n_kf),
        grid_spec=gs,
        out_shape=jax.ShapeDtypeStruct(x.shape, F32),
        compiler_params=_cparams("arbitrary", "arbitrary", "arbitrary", "arbitrary"),
        name="moe_down",
    )(off, run, hs, wd, x, cols, rows)


def _rmsnorm_kernel(x_ref, g_ref, o_ref):
    x = x_ref[...]
    ms = jnp.mean(x * x, axis=-1, keepdims=True)
    o_ref[...] = x * lax.rsqrt(ms + NORM_EPS) * g_ref[...]


def rmsnorm(x, g):
    M, D = x.shape
    tm = _pick(M, (512, 256, 128, 8))
    return pl.pallas_call(
        _rmsnorm_kernel,
        grid=(M // tm,),
        in_specs=[pl.BlockSpec((tm, D), lambda i: (i, 0)), pl.BlockSpec((1, D), lambda i: (0, 0))],
        out_specs=pl.BlockSpec((tm, D), lambda i: (i, 0)),
        out_shape=jax.ShapeDtypeStruct((M, D), F32),
        compiler_params=_cparams("parallel"),
        name="final_norm",
    )(x, g.reshape(1, D))


def _rwkv_prep_kernel(p_ref, prev_ref, s0_ref, mu_ref, w0_ref, w2h_ref, w2l_ref, a0_ref, a2h_ref, a2l_ref,
                      g2h_ref, g2l_ref, kk_ref, ka_ref, bd_ref,
                      r_ref, lw_ref, k_ref, v_ref, kn_ref, b_ref, g_ref, *, T):
    p = p_ref[0]
    tm = p.shape[0]
    row = lax.broadcasted_iota(jnp.int32, (tm, 1), 0)
    shifted = jnp.where(row == 0, prev_ref[0, 7:8, :], pltpu.roll(p, 1, 0))
    nseq = s0_ref.shape[0]
    s0_rows = jnp.broadcast_to(s0_ref[...], (nseq, tm // nseq, p.shape[1])).reshape(tm, p.shape[1])
    first = ((pl.program_id(0) * tm + row) % T) == 0
    ps = p + (jnp.where(first, s0_rows, shifted) - p) * mu_ref[...]
    W = A_WIDTH
    r = ps[:, 0:W]
    k = ps[:, W:2 * W]
    v = ps[:, 2 * W:3 * W]
    o = 3 * W
    wd = ps[:, o:o + A_LORA_W]
    ad = ps[:, o + A_LORA_W:o + A_LORA_W + A_LORA_A]
    gd = ps[:, o + A_LORA_W + A_LORA_A:]
    z = -(w0_ref[...] + _dot3(jnp.tanh(wd), w2h_ref[...], w2l_ref[...]))
    softplus = jnp.maximum(z, 0.0) + jnp.log(1.0 + jnp.exp(-jnp.abs(z)))
    lw_ref[...] = -jnp.exp(-softplus - 0.5)
    a = _sigmoid(a0_ref[...] + _dot3(ad, a2h_ref[...], a2l_ref[...]))
    g_ref[...] = _dot3(_sigmoid(gd), g2h_ref[...], g2l_ref[...])
    kk = k * kk_ref[...]
    ss = _head_sums(kk * kk, bd_ref[...])
    kn = kk / jnp.maximum(jnp.sqrt(ss), 1e-12)
    r_ref[...] = r
    v_ref[...] = v
    kn_ref[...] = kn
    b_ref[...] = kn * a
    k_ref[...] = k * (1.0 + (a - 1.0) * ka_ref[...])


def rwkv_prep(pab, shift0, prm, T):
    M = pab.shape[1]
    tm = _pick(M, (256, 128, 8))
    assert T % tm == 0 or tm % T == 0
    nseq = max(tm // T, 1)
    s0_map = (lambda i: (i, 0, 0)) if tm >= T else (lambda i: ((i * tm) // T, 0, 0))
    W = A_WIDTH
    row = lambda n: pl.BlockSpec((1, n), lambda i: (0, 0))
    mat = lambda a, b: pl.BlockSpec((a, b), lambda i: (0, 0))
    out = pl.BlockSpec((tm, W), lambda i: (i, 0))
    return pl.pallas_call(
        functools.partial(_rwkv_prep_kernel, T=T),
        grid=(M // tm,),
        in_specs=[pl.BlockSpec((1, tm, A_PROJ), lambda i: (0, i, 0)),
                  pl.BlockSpec((1, 8, A_PROJ), lambda i: (0, jnp.maximum(i * (tm // 8) - 1, 0), 0)),
                  pl.BlockSpec((nseq, 1, A_PROJ), s0_map),
                  row(A_PROJ), row(W), mat(A_LORA_W, W), mat(A_LORA_W, W),
                  row(W), mat(A_LORA_A, W), mat(A_LORA_A, W),
                  mat(A_LORA_G, W), mat(A_LORA_G, W), row(W), row(W), mat(LANES, LANES)],
        out_specs=[out] * 7,
        out_shape=[jax.ShapeDtypeStruct((M, W), F32)] * 7,
        compiler_params=_cparams("parallel"),
        name="rwkv_prep",
    )(pab, pab, shift0[:, None, :], prm["mu"], prm["w0"], *prm["w2"], prm["a0"], *prm["a2"], *prm["g2"],
      prm["k_k"], prm["k_a"], prm["bd"])


def _rwkv_chunk_kernel(r_ref, lw_ref, k_ref, v_ref, kn_ref, b_ref, s0_ref, y_ref, s_ref, *, C, nc, Bb):
    C2 = 2 * C
    lane = lax.broadcasted_iota(jnp.int32, (C2, LANES), 1)
    rowi = lax.broadcasted_iota(jnp.int32, (C2, LANES), 0)
    mask2 = ((rowi >= C) == (lane >= HEAD_DIM)).astype(F32)
    ri = lax.broadcasted_iota(jnp.int32, (C2, C2), 0)
    ci = lax.broadcasted_iota(jnp.int32, (C2, C2), 1)
    same = (ri >= C) == (ci >= C)
    strict = same & (ri > ci)
    incl = same & (ri >= ci)
    eye = (ri == ci).astype(F32)
    ti = lax.broadcasted_iota(jnp.int32, (C, C), 0)
    tj = lax.broadcasted_iota(jnp.int32, (C, C), 1)
    tri = (ti >= tj).astype(BF16)
    n_levels = max(C.bit_length() - 2, 0)
    merged = C2 % LANES == 0
    streams = range(Bb)

    def stack(x):
        return jnp.concatenate([x, x], axis=0) * mask2

    def each(f, *lists):
        return [f(*[l[i] for l in lists]) for i in streams]

    @pl.when(pl.program_id(2) == 0)
    def _():
        s_ref[...] = s0_ref[...]

    def chunk(c, states):
        sl = pl.ds(pl.multiple_of(c * C, C), C)
        S = list(states)
        load = lambda ref: [ref[i, sl, :] for i in streams]
        r, lw, k, v, kn, b = (load(ref) for ref in (r_ref, lw_ref, k_ref, v_ref, kn_ref, b_ref))
        cs = each(lambda x: _dot_exact_lhs3(tri, x), lw)
        gt = each(lambda x: x[C - 1:C, :], cs)
        e_neg = each(lambda x: jnp.exp(-x), cs)
        e_rem = each(lambda g, x: jnp.exp(g - x), gt, cs)
        KT = each(lambda a, x, l: stack(a * jnp.exp(x - l)), kn, cs, lw)
        BI = each(lambda a, e: stack(a * e), b, e_neg)
        KI = each(lambda a, e: stack(a * e), k, e_neg)
        RT = each(lambda a, x: stack(a * jnp.exp(x)), r, cs)
        V2 = each(stack, v)
        KG = each(lambda a, e: stack(a * e), k, e_rem)
        BG = each(lambda a, e: stack(a * e), b, e_rem)
        if merged:
            quad = each(lambda kt, rt, bi, ki: _dot_nt(jnp.concatenate([kt, rt], axis=0),
                                                        jnp.concatenate([bi, ki], axis=0)), KT, RT, BI, KI)
            a_kb = each(lambda q: q[:C2, :C2], quad)
            a_kv = each(lambda q: q[:C2, C2:], quad)
            a_rb = each(lambda q: q[C2:, :C2], quad)
            a_rk = each(lambda q: q[C2:, C2:], quad)
        else:
            a_kb = each(_dot_nt, KT, BI)
            a_kv = each(_dot_nt, KT, KI)
            a_rb = each(_dot_nt, RT, BI)
            a_rk = each(_dot_nt, RT, KI)
        a_rb = each(lambda x: jnp.where(incl, x, 0.0), a_rb)
        a_rk = each(lambda x: jnp.where(incl, x, 0.0), a_rk)
        pw = each(lambda x: -jnp.where(strict, x, 0.0), a_kb)
        tinv = each(lambda x: eye + x, pw)
        if n_levels:
            pw = each(_dot, pw, pw)
            for _ in range(n_levels - 1):
                both = each(lambda p, t: _dot(jnp.concatenate([p, t], axis=0), p), pw, tinv)
                pw = each(lambda x: x[:C2], both)
                tinv = each(lambda t, x: t + x[C2:], tinv, both)
            tinv = each(lambda t, p: t + _dot(t, p), tinv, pw)
        av = each(lambda x, vv: _dot(jnp.where(strict, x, 0.0), vv), a_kv, V2)
        kpw = each(lambda t, kt, a: _dot(t, jnp.concatenate([kt, a], axis=1)), tinv, KT, av)
        corr = each(_dot, a_rb, kpw)
        rp = each(lambda x, c_: x - c_[:, :LANES], RT, corr)
        y1 = each(lambda a, vv, c_: _dot(a, vv) - c_[:, LANES:], a_rk, V2, corr)
        low = each(_dot_tn, kpw, BG)
        mlow = each(lambda x: x[:LANES], low)
        nt = each(lambda vv, kg, x: _dot_tn(vv, kg) - x[LANES:], V2, KG, low)
        y2 = each(lambda p, s, y: _dot_nt(p, s) + y, rp, S, y1)
        for i in streams:
            y_ref[i, sl, :] = y2[i][:C] + y2[i][C:]
        new = each(lambda s, g, m, n: s * jnp.exp(g) - _dot3(s, *_split2(m)) + n, S, gt, mlow, nt)
        return tuple(new)

    final = lax.fori_loop(0, nc, chunk, tuple(s_ref[i, 0] for i in streams))
    for i in streams:
        s_ref[i, 0] = final[i]


def rwkv_chunk(r, lw, k, v, kn, b, s_bd, B, T):
    C = min(T, 64)
    Tt = _pick(T, (256, 128, 64)) if T > C else T
    nc = Tt // C
    Bb = _pick(B, (8, 4, 2, 1))
    HP = A_WIDTH // LANES
    seq = pl.BlockSpec((Bb, Tt, LANES), lambda i, h, t: (i, t, h))
    st = pl.BlockSpec((Bb, 1, LANES, LANES), lambda i, h, t: (i, h, 0, 0))
    r3 = lambda x: x.reshape(B, T, A_WIDTH)
    y, s = pl.pallas_call(
        functools.partial(_rwkv_chunk_kernel, C=C, nc=nc, Bb=Bb),
        grid=(B // Bb, HP, T // Tt),
        in_specs=[seq] * 6 + [st],
        out_specs=[seq, st],
        out_shape=[jax.ShapeDtypeStruct((B, T, A_WIDTH), F32),
                   jax.ShapeDtypeStruct((B, HP, LANES, LANES), F32)],
        compiler_params=_cparams("parallel", "parallel", "arbitrary"),
        name="rwkv_chunk",
    )(r3(r), r3(lw), r3(k), r3(v), r3(kn), r3(b), s_bd)
    return y.reshape(B * T, A_WIDTH), s


def _rwkv_post_kernel(y_ref, r_ref, k_ref, v_ref, g_ref, rk_ref, lnw_ref, lnb_ref, bd_ref, o_ref):
    y = y_ref[...]
    bd = bd_ref[...]
    inv_n = 1.0 / HEAD_DIM
    mean = _head_sums(y, bd) * inv_n
    d = y - mean
    var = _head_sums(d * d, bd) * inv_n
    yn = d * lax.rsqrt(var + RWKV_GN_EPS) * lnw_ref[...] + lnb_ref[...]
    bonus = _head_sums(r_ref[...] * k_ref[...] * rk_ref[...], bd)
    o_ref[...] = _bf((yn + bonus * v_ref[...]) * g_ref[...])


def rwkv_post(y, r, k, v, g, prm):
    M, W = y.shape
    tm = _pick(M, (256, 128, 8))
    blk = pl.BlockSpec((tm, W), lambda i: (i, 0))
    row = pl.BlockSpec((1, W), lambda i: (0, 0))
    return pl.pallas_call(
        _rwkv_post_kernel,
        grid=(M // tm,),
        in_specs=[blk] * 5 + [row] * 3 + [pl.BlockSpec((LANES, LANES), lambda i: (0, 0))],
        out_specs=blk,
        out_shape=jax.ShapeDtypeStruct((M, W), BF16),
        compiler_params=_cparams("parallel"),
        name="rwkv_post",
    )(y, r, k, v, g, prm["r_k"], prm["ln_w"], prm["ln_b"], prm["bd"])


def _logf_kernel(f_ref, b_ref, o_ref):
    z = f_ref[0] + b_ref[...]
    o_ref[...] = jnp.minimum(z, 0.0) - jnp.log(1.0 + jnp.exp(-jnp.abs(z)))


def fox_logf(pab, bias_row, col_block):
    M = pab.shape[1]
    tm = _pick(M, (1024, 512, 256, 128, 8))
    return pl.pallas_call(
        _logf_kernel,
        grid=(M // tm,),
        in_specs=[pl.BlockSpec((1, tm, LANES), lambda i: (1, i, col_block)),
                  pl.BlockSpec((1, LANES), lambda i: (0, 0))],
        out_specs=pl.BlockSpec((tm, LANES), lambda i: (i, 0)),
        out_shape=jax.ShapeDtypeStruct((M, LANES), F32),
        compiler_params=_cparams("parallel"),
        name="fox_logf",
    )(pab, bias_row)


def _cumsum_kernel(x_ref, o_ref, carry_ref):
    @pl.when(pl.program_id(1) == 0)
    def _():
        carry_ref[...] = jnp.zeros_like(carry_ref)

    ti = lax.broadcasted_iota(jnp.int32, (LANES, LANES), 0)
    tj = lax.broadcasted_iota(jnp.int32, (LANES, LANES), 1)
    triu = (ti <= tj).astype(BF16)
    hi, mid, lo = _split3(x_ref[0])
    cs = (jnp.dot(hi, triu, preferred_element_type=F32)
          + (jnp.dot(mid, triu, preferred_element_type=F32)
             + jnp.dot(lo, triu, preferred_element_type=F32))) + carry_ref[...]
    o_ref[0] = cs
    carry_ref[...] = jnp.broadcast_to(cs[:, LANES - 1:LANES], cs.shape)


def cumsum_lanes(xT):
    B, H, L = xT.shape
    return pl.pallas_call(
        _cumsum_kernel,
        grid=(B, L // LANES),
        in_specs=[pl.BlockSpec((1, H, LANES), lambda b, p: (b, 0, p))],
        out_specs=pl.BlockSpec((1, H, LANES), lambda b, p: (b, 0, p)),
        out_shape=jax.ShapeDtypeStruct((B, H, L), F32),
        scratch_shapes=[pltpu.VMEM((H, LANES), F32)],
        compiler_params=_cparams("parallel", "arbitrary"),
        name="fox_cumsum",
    )(xT)


def _head_lane_mask(n_rows, width, head):
    lane = lax.broadcasted_iota(jnp.int32, (n_rows, width), 1)
    return (lane // HEAD_DIM) == head


def _two_pass_attend(qs, n_past, add_past, diag_start, add_diag, kb_ref, vb_ref, s_ref, acc_ref, l_ref, m_ref):
    blk = MOBA_BLOCK
    rep = blk // LANES

    def scores(kb):
        return lax.dot_general(qs, kb, NT_DIMS, preferred_element_type=F32)

    s_d = add_diag(scores(kb_ref[pl.ds(diag_start, blk), :]))
    s_ref[s_ref.shape[0] - 1] = s_d
    l_ref[...] = s_d

    def pass1(n, carry):
        s = add_past(n, scores(kb_ref[pl.ds(pl.multiple_of(n * blk, blk), blk), :]))
        s_ref[n] = s
        l_ref[...] = jnp.maximum(l_ref[...], s)
        return carry

    lax.fori_loop(0, n_past, pass1, 0)
    m_ref[...] = jnp.broadcast_to(jnp.max(l_ref[...], axis=1, keepdims=True), m_ref.shape)

    def probs(n):
        return jnp.exp2(s_ref[n] - jnp.concatenate([m_ref[...]] * rep, axis=1))

    p_d = probs(s_ref.shape[0] - 1)
    l_ref[...] = p_d
    acc_ref[...] = jnp.dot(_bf(p_d), vb_ref[pl.ds(diag_start, blk), :], preferred_element_type=F32)

    def pass2(n, carry):
        p = probs(n)
        l_ref[...] += p
        acc_ref[...] += jnp.dot(_bf(p), vb_ref[pl.ds(pl.multiple_of(n * blk, blk), blk), :],
                                preferred_element_type=F32)
        return carry

    lax.fori_loop(0, n_past, pass2, 0)
    return acc_ref[...] / jnp.sum(l_ref[...], axis=1, keepdims=True)


def _attend_scratch(n_blocks, R):
    return [pltpu.VMEM((n_blocks + 1, R, MOBA_BLOCK), F32),
            pltpu.VMEM((R, GROUP_LANES), F32),
            pltpu.VMEM((R, MOBA_BLOCK), F32),
            pltpu.VMEM((R, LANES), F32)]


def _fox_prompt_kernel(q_ref, k_ref, v_ref, c_ref, o_ref, kb_ref, vb_ref, s_ref, acc_ref, l_ref, m_ref, *, tq, hpg):
    g = pl.program_id(1)
    qi = pl.program_id(2)

    @pl.when(qi == 0)
    def _():
        kb_ref[...] = _bf(k_ref[0])
        vb_ref[...] = _bf(v_ref[0])

    q = q_ref[0] * (HEAD_DIM ** -0.5 * LOG2E)
    qs = _bf(jnp.concatenate([jnp.where(_head_lane_mask(tq, GROUP_LANES, h), q, 0.0) for h in range(hpg)], axis=0))
    causal = lax.broadcasted_iota(jnp.int32, (tq, tq), 0) >= lax.broadcasted_iota(jnp.int32, (tq, tq), 1)

    def add_bias(n, s, mask):
        parts = []
        for h in range(hpg):
            c_row = c_ref[0, pl.ds(g * hpg + h, 1), pl.ds(pl.multiple_of(n * tq, tq), tq)] * LOG2E
            sh = s[h * tq:(h + 1) * tq, :] - c_row
            parts.append(jnp.where(causal, sh, NEG_INF) if mask else sh)
        return jnp.concatenate(parts, axis=0)

    o = _two_pass_attend(qs, qi, lambda n, s: add_bias(n, s, False), pl.multiple_of(qi * tq, tq),
                         lambda s: add_bias(qi, s, True), kb_ref, vb_ref, s_ref, acc_ref, l_ref, m_ref)
    out = jnp.zeros((tq, GROUP_LANES), F32)
    for h in range(hpg):
        out = out + jnp.where(_head_lane_mask(tq, GROUP_LANES, h), o[h * tq:(h + 1) * tq, :], 0.0)
    o_ref[...] = _bf(out)


def fox_prompt(pab, cT, B, T):
    tq = MOBA_BLOCK
    nq = T // tq
    hpg = GROUP_LANES // HEAD_DIM
    G = B_WIDTH // GROUP_LANES
    return pl.pallas_call(
        functools.partial(_fox_prompt_kernel, tq=tq, hpg=hpg),
        grid=(B, G, nq),
        in_specs=[pl.BlockSpec((1, tq, GROUP_LANES), lambda b, g, qi: (1, b * nq + qi, g)),
                  pl.BlockSpec((1, T, GROUP_LANES), lambda b, g, qi: (1, b, G + g)),
                  pl.BlockSpec((1, T, GROUP_LANES), lambda b, g, qi: (1, b, 2 * G + g)),
                  pl.BlockSpec((1, B_HEADS, T), lambda b, g, qi: (b, 0, 0))],
        out_specs=pl.BlockSpec((tq, GROUP_LANES), lambda b, g, qi: (b * nq + qi, g)),
        out_shape=jax.ShapeDtypeStruct((B * T, B_WIDTH), BF16),
        scratch_shapes=[pltpu.VMEM((T, GROUP_LANES), BF16),
                        pltpu.VMEM((T, GROUP_LANES), BF16)] + _attend_scratch(nq - 1, hpg * tq),
        compiler_params=_cparams("parallel", "parallel", "arbitrary"),
        name="fox_prompt",
    )(pab, pab, pab, cT)


def _moba_slopes(n_rows, rows_per_head, i, kv_base):
    c = lax.broadcasted_iota(jnp.int32, (n_rows, 1), 0) // rows_per_head
    head = C_REP * (kv_base + c) + i
    return jnp.exp2(-8.0 * (head + 1).astype(F32) / C_HEADS)


def _top_blocks(z, live, idxf, axis=1):
    sel = jnp.zeros(z.shape, F32)
    for _ in range(MOBA_TOPK):
        m = jnp.max(z, axis=axis, keepdims=True)
        idx = jnp.min(jnp.where((z == m) & live, idxf, float(LANES)), axis=axis, keepdims=True)
        pick = idxf == idx
        sel = jnp.where(pick, 1.0, sel)
        z = jnp.where(pick, NEG_INF, z)
    return sel


def _moba_prompt_kernel(q0_ref, q1_ref, q2_ref, q3_ref, k_ref, v_ref, o_ref,
                        kb_ref, vb_ref, km_ref, s_ref, acc_ref, l_ref, m_ref, *, tq, nb, cpg):
    g = pl.program_id(1)
    qi = pl.program_id(2)
    R = cpg * tq
    blk = MOBA_BLOCK

    @pl.when(qi == 0)
    def _():
        km_ref[...] = jnp.zeros_like(km_ref)
        for n in range(nb):
            kblk = k_ref[0, n * blk:(n + 1) * blk, :]
            kb_ref[n * blk:(n + 1) * blk, :] = _bf(kblk)
            vb_ref[n * blk:(n + 1) * blk, :] = _bf(v_ref[0, n * blk:(n + 1) * blk, :])
            km_ref[n:n + 1, :] = jnp.sum(kblk, axis=0, keepdims=True) * (1.0 / blk)

    own = (qi * tq) // blk
    row_tok = lax.broadcasted_iota(jnp.int32, (R, 1), 0) % tq
    qpos = (qi * tq + row_tok).astype(F32)
    koff = lax.broadcasted_iota(jnp.int32, (R, blk), 1)
    causal = (qi * tq - own * blk + row_tok) >= koff
    blockf = lax.broadcasted_iota(jnp.int32, (NB_PAD, R), 0).astype(F32)
    past = blockf < own
    km = km_ref[0:NB_PAD, :]
    spread_row = lax.broadcasted_iota(jnp.int32, (NB_PAD, LANES), 0)
    for i, q_ref in enumerate((q0_ref, q1_ref, q2_ref, q3_ref)):
        q = q_ref[0] * (HEAD_DIM ** -0.5 * LOG2E)
        qs = jnp.concatenate([jnp.where(_head_lane_mask(tq, GROUP_LANES, c), q, 0.0) for c in range(cpg)], axis=0)
        gate = _dot3_nt(km, *_split2(qs))
        sel_bf = _bf(_top_blocks(jnp.where(past, gate, NEG_INF), past, blockf, axis=0))
        slope = _moba_slopes(R, tq, i, g * cpg) * LOG2E
        b0 = slope * koff.astype(F32)
        slope_rep = jnp.broadcast_to(slope, (R, LANES))
        sq_rep = slope_rep * qpos
        own_term = b0 + slope * ((own * blk).astype(F32) - qpos)

        def add_past(n, s, b0=b0, sel_bf=sel_bf, slope_rep=slope_rep, sq_rep=sq_rep):
            chosen = _dot_tn(sel_bf, (spread_row == n).astype(BF16)) > 0.5
            rt = jnp.where(chosen, slope_rep * (n * blk).astype(F32) - sq_rep, NEG_INF)
            return s + b0 + jnp.concatenate([rt] * (blk // LANES), axis=1)

        def add_own(s, own_term=own_term):
            return jnp.where(causal, s + own_term, NEG_INF)

        o = _two_pass_attend(_bf(qs), own, add_past, pl.multiple_of(own * blk, blk), add_own,
                             kb_ref, vb_ref, s_ref, acc_ref, l_ref, m_ref)
        out = jnp.zeros((tq, GROUP_LANES), F32)
        for c in range(cpg):
            out = out + jnp.where(_head_lane_mask(tq, GROUP_LANES, c), o[c * tq:(c + 1) * tq, :], 0.0)
        o_ref[i] = _bf(out)


def moba_prompt(p1, B, T):
    tq = MOBA_BLOCK
    nq = T // tq
    nb = T // MOBA_BLOCK
    assert nb <= NB_PAD
    cpg = GROUP_LANES // HEAD_DIM
    G = C_KV_WIDTH // GROUP_LANES
    R = cpg * tq
    qspec = lambda i: pl.BlockSpec((1, tq, GROUP_LANES), lambda b, g, qi, i=i: (i, b * nq + qi, g))
    return pl.pallas_call(
        functools.partial(_moba_prompt_kernel, tq=tq, nb=nb, cpg=cpg),
        grid=(B, G, nq),
        in_specs=[qspec(0), qspec(1), qspec(2), qspec(3),
                  pl.BlockSpec((1, T, GROUP_LANES), lambda b, g, qi: (4, b, g)),
                  pl.BlockSpec((1, T, GROUP_LANES), lambda b, g, qi: (5, b, g))],
        out_specs=pl.BlockSpec((C_REP, tq, GROUP_LANES), lambda b, g, qi: (0, b * nq + qi, g)),
        out_shape=jax.ShapeDtypeStruct((C_REP, B * T, C_KV_WIDTH), BF16),
        scratch_shapes=[pltpu.VMEM((T, GROUP_LANES), BF16),
                        pltpu.VMEM((T, GROUP_LANES), BF16),
                        pltpu.VMEM((LANES, GROUP_LANES), F32)] + _attend_scratch(nb - 1, R),
        compiler_params=_cparams("parallel", "parallel", "arbitrary"),
        name="moba_prompt",
    )(p1, p1, p1, p1, p1, p1)


def _block_diag_rows(x, n_heads):
    lane = lax.broadcasted_iota(jnp.int32, x.shape, 1) // HEAD_DIM
    return jnp.concatenate([jnp.where(lane == h, x, 0.0) for h in range(n_heads)], axis=0)


def _gather_heads(out, n_heads, T):
    lane = lax.broadcasted_iota(jnp.int32, (T, out.shape[1]), 1) // HEAD_DIM
    y = jnp.zeros((T, out.shape[1]), F32)
    for h in range(n_heads):
        y = y + jnp.where(lane == h, out[h * T:(h + 1) * T, :], 0.0)
    return y


def _pad_rows(x, n):
    return jnp.concatenate([x, jnp.zeros((n - x.shape[0], x.shape[1]), x.dtype)], axis=0)


def _softmax_pv(s_list, s_new, v_refs, v_new, width):
    m = jnp.max(s_new, axis=1, keepdims=True)
    for s in s_list:
        m = jnp.maximum(m, jnp.max(s, axis=1, keepdims=True))
    p = jnp.exp(s_new - m)
    l = jnp.sum(p, axis=1, keepdims=True)
    acc = jnp.dot(_bf(p), v_new, preferred_element_type=F32)
    for s, v_ref in zip(s_list, v_refs):
        p = jnp.exp(s - m)
        l = l + jnp.sum(p, axis=1, keepdims=True)
        acc = acc + _dot_nt(p, v_ref[0].reshape(width, v_ref.shape[-1]))
    return acc / l


def _fox_decode_kernel(pt_ref, q_ref, kn_ref, vn_ref, lfn_ref, *rest, T, NP):
    k_refs = rest[0:NP]
    v_refs = rest[NP:2 * NP]
    lf_refs = rest[2 * NP:3 * NP]
    o_ref = rest[3 * NP]
    H, W = B_HEADS, B_WIDTH
    R = H * T
    page = k_refs[0].shape[-1]
    qbd = _bf(_block_diag_rows(q_ref[0] * (HEAD_DIM ** -0.5), H))
    ti = lax.broadcasted_iota(jnp.int32, (page, page), 0)
    tj = lax.broadcasted_iota(jnp.int32, (page, page), 1)
    triu = (ti <= tj).astype(BF16)

    def cumsum(x, carry):
        hi, mid, lo = _split3(x)
        cs = (jnp.dot(hi, triu, preferred_element_type=F32)
              + (jnp.dot(mid, triu, preferred_element_type=F32)
                 + jnp.dot(lo, triu, preferred_element_type=F32))) + carry
        return cs, jnp.broadcast_to(cs[:, page - 1:page], cs.shape)

    def per_row(c):
        return jnp.broadcast_to(c[:, None, :], (H, T, page)).reshape(R, page)

    carry = jnp.zeros((H, page), F32)
    s_list = []
    for j in range(NP):
        cs, carry = cumsum(lf_refs[j][0], carry)
        s = jnp.dot(qbd, _bf(k_refs[j][0].reshape(W, page)), preferred_element_type=F32)
        s_list.append(s - per_row(cs))
    cs_new, _ = cumsum(lfn_ref[0], carry)
    row_tok = lax.broadcasted_iota(jnp.int32, (R, page), 0) % T
    key = lax.broadcasted_iota(jnp.int32, (R, page), 1)
    s_new = _dot_nt(qbd, _pad_rows(kn_ref[0], page)) - per_row(cs_new)
    s_new = jnp.where(key <= row_tok, s_new, NEG_INF)
    out = _softmax_pv(s_list, s_new, v_refs, _bf(_pad_rows(vn_ref[0], page)), W)
    o_ref[0] = _bf(_gather_heads(out, H, T))


def fox_decode(pab, lfn_t, cache_k, cache_v, cache_lf, page_table, T):
    B, NP = page_table.shape
    W = B_WIDTH
    page4 = lambda j: (lambda b, pt: (pt[b, j], 0, 0, 0))
    page3 = lambda j: (lambda b, pt: (pt[b, j], 0, 0))
    in_specs = [pl.BlockSpec((1, T, W), lambda b, pt: (1, b, 0)),
                pl.BlockSpec((1, T, W), lambda b, pt: (1, b, 1)),
                pl.BlockSpec((1, T, W), lambda b, pt: (1, b, 2)),
                pl.BlockSpec((1, *lfn_t.shape[1:]), lambda b, pt: (b, 0, 0))]
    in_specs += [pl.BlockSpec((1, *cache_k.shape[1:]), page4(j)) for j in range(NP)] * 2
    in_specs += [pl.BlockSpec((1, *cache_lf.shape[1:]), page3(j)) for j in range(NP)]
    gs = pltpu.PrefetchScalarGridSpec(
        num_scalar_prefetch=1, grid=(B,), in_specs=in_specs,
        out_specs=pl.BlockSpec((1, T, W), lambda b, pt: (b, 0, 0)))
    return pl.pallas_call(
        functools.partial(_fox_decode_kernel, T=T, NP=NP),
        grid_spec=gs,
        out_shape=jax.ShapeDtypeStruct((B, T, W), BF16),
        compiler_params=_cparams("arbitrary"),
        name="fox_decode",
    )(page_table, pab, pab, pab, lfn_t, *([cache_k] * NP), *([cache_v] * NP), *([cache_lf] * NP))


def _moba_decode_kernel(pt_ref, q_ref, kn_ref, vn_ref, *rest, T, NP, q_start):
    k_refs = rest[0:NP]
    v_refs = rest[NP:2 * NP]
    o_ref = rest[2 * NP]
    Hkv, W = C_KV_HEADS, C_KV_WIDTH
    page = k_refs[0].shape[-1]
    ppb = MOBA_BLOCK // page
    nbp = NP // ppb
    own = q_start // MOBA_BLOCK
    RG = Hkv * T
    R = C_REP * RG

    qbd = jnp.concatenate([_block_diag_rows(q_ref[i] * (HEAD_DIM ** -0.5), Hkv) for i in range(C_REP)], axis=0)
    qbd_bf = _bf(qbd)
    rowi = lax.broadcasted_iota(jnp.int32, (R, 1), 0)
    row_tok = rowi % T
    head = C_REP * ((rowi // T) % Hkv) + rowi // RG
    slope = jnp.exp2(-8.0 * (head + 1).astype(F32) / C_HEADS)
    qpos = (q_start + row_tok).astype(F32)
    lane = lax.broadcasted_iota(jnp.int32, (R, page), 1)
    lanef = lane.astype(F32)

    kps = [k_refs[j][0].reshape(W, page) for j in range(NP)]
    col = lax.broadcasted_iota(jnp.int32, (W, page), 1)
    km = jnp.zeros((W, page), F32)
    for n in range(nbp):
        tot = jnp.sum(kps[n * ppb], axis=1, keepdims=True)
        for j in range(1, ppb):
            tot = tot + jnp.sum(kps[n * ppb + j], axis=1, keepdims=True)
        km = jnp.where(col == n, tot * (1.0 / MOBA_BLOCK), km)
    live = lane < min(own, nbp)
    sel = _top_blocks(jnp.where(live, _dot3(qbd, *_split2(km)), NEG_INF), live, lanef)

    b0 = slope * lanef
    s_list = []
    for n in range(nbp):
        chosen = jnp.max(jnp.where(lane == n, sel, 0.0), axis=1, keepdims=True) > 0.5
        for j in range(n * ppb, (n + 1) * ppb):
            row_term = jnp.where(chosen, slope * (j * page - qpos), NEG_INF)
            s_list.append(jnp.dot(qbd_bf, _bf(kps[j]), preferred_element_type=F32) + b0 + row_term)
    s_new = _dot_nt(qbd_bf, _pad_rows(kn_ref[0], page)) - slope * (row_tok - lane).astype(F32)
    s_new = jnp.where(lane <= row_tok, s_new, NEG_INF)
    out = _softmax_pv(s_list, s_new, v_refs, _bf(_pad_rows(vn_ref[0], page)), W)
    for i in range(C_REP):
        o_ref[i] = _bf(_gather_heads(out[i * RG:(i + 1) * RG, :], Hkv, T))


def moba_decode(p1, cache_k, cache_v, page_table, T, q_start):
    B, NP = page_table.shape
    W = C_KV_WIDTH
    page4 = lambda j: (lambda b, pt: (pt[b, j], 0, 0, 0))
    in_specs = [pl.BlockSpec((C_REP, T, W), lambda b, pt: (0, b, 0)),
                pl.BlockSpec((1, T, W), lambda b, pt: (C_REP, b, 0)),
                pl.BlockSpec((1, T, W), lambda b, pt: (C_REP + 1, b, 0))]
    in_specs += [pl.BlockSpec((1, *cache_k.shape[1:]), page4(j)) for j in range(NP)] * 2
    gs = pltpu.PrefetchScalarGridSpec(
        num_scalar_prefetch=1, grid=(B,), in_specs=in_specs,
        out_specs=pl.BlockSpec((C_REP, T, W), lambda b, pt: (0, b, 0)))
    return pl.pallas_call(
        functools.partial(_moba_decode_kernel, T=T, NP=NP, q_start=q_start),
        grid_spec=gs,
        out_shape=jax.ShapeDtypeStruct((C_REP, B * T, W), BF16),
        compiler_params=_cparams("arbitrary"),
        name="moba_decode",
    )(page_table, p1, p1, p1, *([cache_k] * NP), *([cache_v] * NP))


def _row(v):
    return v.reshape(1, -1).astype(F32)


def _prep_params(norm0_mix_g, w_in0, fox_b_f, rwkv_mu, rwkv_w0, rwkv_w2, rwkv_a0, rwkv_a2, rwkv_g2, rwkv_k_k,
                 rwkv_k_a, rwkv_r_k, rwkv_ln_w, rwkv_ln_b, w_out0, norm0_ffn_g, ffn_w_gate, ffn_w_up,
                 ffn_w_down, norm1_mix_g, w_in1, w_out1, norm1_ffn_g, router_w, router_b, moe_w_gate,
                 moe_w_up, moe_w_down, norm_final_g):
    D = w_in0.shape[0]
    pad_b = A_PROJ - (3 * B_WIDTH + B_HEADS)
    w0 = jnp.concatenate([w_in0, jnp.zeros((D, pad_b), F32)], axis=1)
    wq = w_in1[:, :C_WIDTH].reshape(D, C_KV_HEADS, C_REP, HEAD_DIM).transpose(0, 2, 1, 3).reshape(D, C_WIDTH)
    w1 = jnp.concatenate([wq, w_in1[:, C_WIDTH:]], axis=1)
    wo1 = w_out1.reshape(C_KV_HEADS, C_REP, HEAD_DIM, D).transpose(1, 0, 2, 3).reshape(C_WIDTH, D)
    hd = lax.broadcasted_iota(jnp.int32, (LANES, LANES), 0) // HEAD_DIM
    hd2 = lax.broadcasted_iota(jnp.int32, (LANES, LANES), 1) // HEAD_DIM
    rw = jnp.concatenate([router_w.T, jnp.zeros((E_PAD - N_EXPERTS, D), F32)], axis=0)
    rb = jnp.concatenate([router_b, jnp.zeros((E_PAD - N_EXPERTS,), F32)])
    return dict(
        g0=norm0_mix_g, w0=_bf(w0),
        fox_b=jnp.concatenate([fox_b_f, jnp.zeros((LANES - B_HEADS,), F32)]).reshape(1, LANES),
        rwkv=dict(mu=_row(rwkv_mu), w0=_row(rwkv_w0), w2=_split2(rwkv_w2), a0=_row(rwkv_a0), a2=_split2(rwkv_a2),
                  g2=_split2(rwkv_g2), k_k=_row(rwkv_k_k), k_a=_row(rwkv_k_a), r_k=_row(rwkv_r_k),
                  ln_w=_row(rwkv_ln_w), ln_b=_row(rwkv_ln_b), bd=(hd == hd2).astype(BF16)),
        wo0=_bf(w_out0), g0f=norm0_ffn_g,
        ffn_g=_bf(ffn_w_gate), ffn_u=_bf(ffn_w_up), ffn_d=_bf(ffn_w_down),
        g1=norm1_mix_g, w1=_bf(w1), wo1=_bf(wo1), g1f=norm1_ffn_g,
        router=(*_split2(rw), rb.reshape(E_PAD, 1)),
        moe_g=_bf(moe_w_gate), moe_u=_bf(moe_w_up), moe_d=_bf(moe_w_down),
        gf=norm_final_g)


def _pair_states(S):
    B = S.shape[0]
    S = S.reshape(B, A_HEADS // 2, 2, HEAD_DIM, HEAD_DIM)
    z = jnp.zeros_like(S[:, :, 0])
    top = jnp.concatenate([S[:, :, 0], z], axis=-1)
    bot = jnp.concatenate([z, S[:, :, 1]], axis=-1)
    return jnp.concatenate([top, bot], axis=-2)


def _unpair_states(S):
    B = S.shape[0]
    a = S[:, :, :HEAD_DIM, :HEAD_DIM]
    b = S[:, :, HEAD_DIM:, HEAD_DIM:]
    return jnp.stack([a, b], axis=2).reshape(B, A_HEADS, HEAD_DIM, HEAD_DIM)


def _run(P, x, S0, shift0, caches, page_table):
    B, T, D = x.shape
    M = B * T
    xt = x.reshape(M, D)

    pab = norm_matmul(xt, P["g0"], P["w0"], A_PROJ, A_PROJ // 2)
    r, lw, k, v, kn, bb, gg = rwkv_prep(pab, shift0, P["rwkv"], T)
    s_in = jnp.zeros((B, A_HEADS // 2, LANES, LANES), F32) if S0 is None else _pair_states(S0)
    y, s_bd = rwkv_chunk(r, lw, k, v, kn, bb, s_in, B, T)
    ya = rwkv_post(y, r, k, v, gg, P["rwkv"])
    S_new = _unpair_states(s_bd)
    shift_new = lax.slice(pab, (0, T - 1, 0), (1, M, A_PROJ), (1, T, 1)).reshape(B, A_PROJ)

    cols = lambda c: lax.slice(pab, (1, 0, c * B_WIDTH), (2, M, (c + 1) * B_WIDTH))
    fk = cols(1).reshape(B, T, B_HEADS, HEAD_DIM)
    fv = cols(2).reshape(B, T, B_HEADS, HEAD_DIM)
    logf = fox_logf(pab, P["fox_b"], 3 * B_WIDTH // LANES)[:, :B_HEADS].reshape(B, T, B_HEADS)
    if caches is None:
        yb = fox_prompt(pab, cumsum_lanes(jnp.swapaxes(logf, 1, 2)), B, T)
    else:
        page = caches["fox_k"].shape[1]
        by_head = lambda c: jnp.transpose(c, (0, 2, 3, 1))
        lfn_t = jnp.pad(jnp.swapaxes(logf, 1, 2), ((0, 0), (0, 0), (0, page - T)))
        yb = fox_decode(pab, lfn_t, by_head(caches["fox_k"]), by_head(caches["fox_v"]),
                        jnp.swapaxes(caches["fox_logf"], 1, 2), page_table, T).reshape(M, B_WIDTH)
    yab = jnp.concatenate([ya, yb], axis=1)[None]
    x1 = matmul_res(yab, P["wo0"], xt, tk=A_WIDTH + B_WIDTH)

    h = swiglu_up(x1, P["g0f"], P["ffn_g"], P["ffn_u"])
    x2 = matmul_res(h[None], P["ffn_d"], x1, tk=_pick(h.shape[1], (2816, 1024, 512, 256, 128)))

    p1 = norm_matmul(x2, P["g1"], P["w1"], C_KV_WIDTH, C_KV_WIDTH)
    mk = p1[4].reshape(B, T, C_KV_HEADS, HEAD_DIM)
    mv = p1[5].reshape(B, T, C_KV_HEADS, HEAD_DIM)
    if caches is None:
        y1 = moba_prompt(p1, B, T)
    else:
        page = caches["moba_k"].shape[1]
        by_head = lambda c: jnp.transpose(c, (0, 2, 3, 1))
        y1 = moba_decode(p1, by_head(caches["moba_k"]), by_head(caches["moba_v"]), page_table, T,
                         page_table.shape[1] * page)
    x3 = matmul_res(y1, P["wo1"], x2, tk=C_KV_WIDTH)

    xn, rows, cols, meta = moe_route(x3, P["g1f"], P["router"])
    off = meta[:, :N_EXPERTS, 0].astype(jnp.int32)
    run = meta[:, :N_EXPERTS, 1].astype(jnp.int32)
    hs = moe_up_sparse(xn, rows, off, run, P["moe_g"], P["moe_u"])
    out = rmsnorm(moe_down_sparse(hs, cols, rows, off, run, P["moe_d"], x3), P["gf"])
    return out.reshape(B, T, D), S_new, shift_new, fk, fv, logf, mk, mv


def kernel(x_prompt, x_sample, state_rwkv_S, state_rwkv_shift, cache_fox_k, cache_fox_v, cache_fox_logf,
           cache_moba_k, cache_moba_v, page_table, norm0_mix_g, w_in0, fox_b_f, rwkv_mu, rwkv_w0, rwkv_w2,
           rwkv_a0, rwkv_a2, rwkv_g2, rwkv_k_k, rwkv_k_a, rwkv_r_k, rwkv_ln_w, rwkv_ln_b, w_out0, norm0_ffn_g,
           ffn_w_gate, ffn_w_up, ffn_w_down, norm1_mix_g, w_in1, w_out1, norm1_ffn_g, router_w, router_b,
           moe_w_gate, moe_w_up, moe_w_down, norm_final_g):
    P = _prep_params(norm0_mix_g, w_in0, fox_b_f, rwkv_mu, rwkv_w0, rwkv_w2, rwkv_a0, rwkv_a2, rwkv_g2,
                     rwkv_k_k, rwkv_k_a, rwkv_r_k, rwkv_ln_w, rwkv_ln_b, w_out0, norm0_ffn_g, ffn_w_gate,
                     ffn_w_up, ffn_w_down, norm1_mix_g, w_in1, w_out1, norm1_ffn_g, router_w, router_b,
                     moe_w_gate, moe_w_up, moe_w_down, norm_final_g)
    n_prompt = x_prompt.shape[0]
    prompt = _run(P, x_prompt, None, jnp.zeros((n_prompt, A_PROJ), x_prompt.dtype), None, None)
    caches = dict(fox_k=cache_fox_k, fox_v=cache_fox_v, fox_logf=cache_fox_logf,
                  moba_k=cache_moba_k, moba_v=cache_moba_v)
    sample = _run(P, x_sample, state_rwkv_S, state_rwkv_shift, caches, page_table)
    return (prompt[0], sample[0], *prompt[1:], *sample[1:])
```

```python
import functools

import jax
import jax.numpy as jnp
from jax import lax
from jax.experimental import pallas as pl
from jax.experimental.pallas import tpu as pltpu

F32 = jnp.float32
BF16 = jnp.bfloat16

HEAD_DIM = 64
A_HEADS = 16
A_WIDTH = A_HEADS * HEAD_DIM
A_LORA_W = 64
A_LORA_A = 64
A_LORA_G = 128
A_PROJ = 3 * A_WIDTH + A_LORA_W + A_LORA_A + A_LORA_G
RWKV_GN_EPS = 64e-5
B_HEADS = 16
B_WIDTH = B_HEADS * HEAD_DIM
C_HEADS = 32
C_KV_HEADS = 8
C_REP = C_HEADS // C_KV_HEADS
C_WIDTH = C_HEADS * HEAD_DIM
C_KV_WIDTH = C_KV_HEADS * HEAD_DIM
MOBA_BLOCK = 256
MOBA_TOPK = 3
N_EXPERTS = 8
NORM_EPS = 1e-6

LANES = 128
GROUP_LANES = 256
VMEM_LIMIT = 56 * 1024 * 1024
NEG_INF = float("-inf")
LOG2E = 1.4426950408889634
NB_PAD = 16
NT_DIMS = (((1,), (1,)), ((), ()))


def _cparams(*sem):
    return pltpu.CompilerParams(dimension_semantics=sem, vmem_limit_bytes=VMEM_LIMIT)


def _bf(x):
    return x.astype(BF16)


def _dot(a, b):
    return jnp.dot(_bf(a), _bf(b), preferred_element_type=F32)


def _dot_nt(a, b):
    return lax.dot_general(_bf(a), _bf(b), NT_DIMS, preferred_element_type=F32)


def _dot_tn(a, b):
    return lax.dot_general(_bf(a), _bf(b), (((0,), (0,)), ((), ())), preferred_element_type=F32)


def _split2(x):
    hi = _bf(x)
    lo = _bf(x - hi.astype(F32))
    return hi, lo


def _split3(x):
    hi = _bf(x)
    r1 = x - hi.astype(F32)
    mid = _bf(r1)
    lo = _bf(r1 - mid.astype(F32))
    return hi, mid, lo


def _dot3(a, b_hi, b_lo):
    a_hi, a_lo = _split2(a)
    return (jnp.dot(a_hi, b_hi, preferred_element_type=F32)
            + (jnp.dot(a_hi, b_lo, preferred_element_type=F32)
               + jnp.dot(a_lo, b_hi, preferred_element_type=F32)))


def _dot3_nt(a, b_hi, b_lo):
    a_hi, a_lo = _split2(a)
    return (lax.dot_general(a_hi, b_hi, NT_DIMS, preferred_element_type=F32)
            + (lax.dot_general(a_hi, b_lo, NT_DIMS, preferred_element_type=F32)
               + lax.dot_general(a_lo, b_hi, NT_DIMS, preferred_element_type=F32)))


def _head_sums(x, pair_ones):
    hi, lo = _split2(x)
    parts = []
    for c in range(x.shape[1] // LANES):
        sl = slice(c * LANES, (c + 1) * LANES)
        parts.append(jnp.dot(hi[:, sl], pair_ones, preferred_element_type=F32)
                     + jnp.dot(lo[:, sl], pair_ones, preferred_element_type=F32))
    return jnp.concatenate(parts, axis=1)


def _dot_exact_lhs3(a_exact, x):
    hi, mid, lo = _split3(x)
    return (jnp.dot(a_exact, hi, preferred_element_type=F32)
            + (jnp.dot(a_exact, mid, preferred_element_type=F32)
               + jnp.dot(a_exact, lo, preferred_element_type=F32)))


def _sigmoid(x):
    return 1.0 / (1.0 + jnp.exp(-x))


def _pick(n, prefs):
    for p in prefs:
        if n % p == 0:
            return p
    return n


def _norm_mm_kernel(x_ref, g_ref, w_ref, o_ref, xn_ref):
    @pl.when(pl.program_id(1) == 0)
    def _():
        x = x_ref[...]
        ms = jnp.mean(x * x, axis=-1, keepdims=True)
        xn_ref[...] = _bf(x * lax.rsqrt(ms + NORM_EPS) * g_ref[...])

    o_ref[0] = jnp.dot(xn_ref[...], w_ref[...], preferred_element_type=F32)


def norm_matmul(x, g, w, cw, tn):
    M, K = x.shape
    N = w.shape[1]
    nc = N // cw
    tm = _pick(M, (1024, 512, 256, 128, 8))
    per = cw // tn
    return pl.pallas_call(
        _norm_mm_kernel,
        grid=(M // tm, N // tn),
        in_specs=[pl.BlockSpec((tm, K), lambda i, j: (i, 0)),
                  pl.BlockSpec((1, K), lambda i, j: (0, 0)),
                  pl.BlockSpec((K, tn), lambda i, j: (0, j))],
        out_specs=pl.BlockSpec((1, tm, tn), lambda i, j: (j // per, i, j % per)),
        out_shape=jax.ShapeDtypeStruct((nc, M, cw), F32),
        scratch_shapes=[pltpu.VMEM((tm, K), BF16)],
        compiler_params=_cparams("parallel", "arbitrary"),
        name="norm_matmul",
    )(x, g.reshape(1, K), w)


def _mm_res_kernel(a_ref, w_ref, r_ref, o_ref, acc_ref, *, nk):
    k = pl.program_id(2)

    @pl.when(k == 0)
    def _():
        acc_ref[...] = r_ref[...]

    acc_ref[...] += jnp.dot(a_ref[0], w_ref[...], preferred_element_type=F32)

    @pl.when(k == nk - 1)
    def _():
        o_ref[...] = acc_ref[...]


def _mm_res_full_kernel(a_ref, w_ref, r_ref, o_ref):
    ka, _, kw = a_ref.shape
    acc = r_ref[...]
    for c in range(ka):
        acc = acc + jnp.dot(a_ref[c], w_ref[c * kw:(c + 1) * kw, :], preferred_element_type=F32)
    o_ref[...] = acc


def matmul_res(a, w, res, tk):
    ka, M, kw = a.shape
    N = w.shape[1]
    tm = _pick(M, (512, 256, 128, 8))
    tn = _pick(N, (1024, 512, 256, 128))
    per = kw // tk
    nk = ka * per
    if per == 1:
        return pl.pallas_call(
            _mm_res_full_kernel,
            grid=(M // tm, N // tn),
            in_specs=[pl.BlockSpec((ka, tm, kw), lambda i, j: (0, i, 0)),
                      pl.BlockSpec((ka * kw, tn), lambda i, j: (0, j)),
                      pl.BlockSpec((tm, tn), lambda i, j: (i, j))],
            out_specs=pl.BlockSpec((tm, tn), lambda i, j: (i, j)),
            out_shape=jax.ShapeDtypeStruct((M, N), F32),
            compiler_params=_cparams("parallel", "parallel"),
            name="matmul_res_full",
        )(a, w, res)
    return pl.pallas_call(
        functools.partial(_mm_res_kernel, nk=nk),
        grid=(M // tm, N // tn, nk),
        in_specs=[pl.BlockSpec((1, tm, tk), lambda i, j, k: (k // per, i, k % per)),
                  pl.BlockSpec((tk, tn), lambda i, j, k: (k, j)),
                  pl.BlockSpec((tm, tn), lambda i, j, k: (i, j))],
        out_specs=pl.BlockSpec((tm, tn), lambda i, j, k: (i, j)),
        out_shape=jax.ShapeDtypeStruct((M, N), F32),
        scratch_shapes=[pltpu.VMEM((tm, tn), F32)],
        compiler_params=_cparams("parallel", "parallel", "arbitrary"),
        name="matmul_res",
    )(a, w, res)


def _swiglu_up_kernel(x_ref, g_ref, wg_ref, wu_ref, h_ref, xn_ref):
    @pl.when(pl.program_id(1) == 0)
    def _():
        x = x_ref[...]
        ms = jnp.mean(x * x, axis=-1, keepdims=True)
        xn_ref[...] = _bf(x * lax.rsqrt(ms + NORM_EPS) * g_ref[...])

    xn = xn_ref[...]
    a = jnp.dot(xn, wg_ref[...], preferred_element_type=F32)
    b = jnp.dot(xn, wu_ref[...], preferred_element_type=F32)
    h_ref[...] = _bf(a * _sigmoid(a) * b)


def swiglu_up(x, g, wg, wu):
    M, K = x.shape
    F = wg.shape[1]
    tm = _pick(M, (1024, 512, 256, 128, 8))
    tn = _pick(F, (512, 256, 128))
    return pl.pallas_call(
        _swiglu_up_kernel,
        grid=(M // tm, F // tn),
        in_specs=[pl.BlockSpec((tm, K), lambda i, j: (i, 0)),
                  pl.BlockSpec((1, K), lambda i, j: (0, 0)),
                  pl.BlockSpec((K, tn), lambda i, j: (0, j)),
                  pl.BlockSpec((K, tn), lambda i, j: (0, j))],
        out_specs=pl.BlockSpec((tm, tn), lambda i, j: (i, j)),
        out_shape=jax.ShapeDtypeStruct((M, F), BF16),
        scratch_shapes=[pltpu.VMEM((tm, K), BF16)],
        compiler_params=_cparams("parallel", "arbitrary"),
        name="ffn_up",
    )(x, g.reshape(1, K), wg, wu)


MOE_RUN = 128
E_PAD = 16
TOP_K = 2


def _moe_cap(TM):
    return TOP_K * TM + N_EXPERTS * MOE_RUN


def _run_sizes(TM):
    top = -(-TM // MOE_RUN) * MOE_RUN
    sizes, s = [], MOE_RUN
    while s <= top:
        sizes.append(s)
        s *= 2
    return tuple(reversed(sizes))


def _moe_route_kernel(x_ref, g_ref, rwh_ref, rwl_ref, rb_ref, xn_ref, rows_ref, cols_ref, meta_ref):
    x = x_ref[...]
    TM = x.shape[0]
    ms = jnp.mean(x * x, axis=-1, keepdims=True)
    xn = x * lax.rsqrt(ms + NORM_EPS) * g_ref[...]
    xn_ref[...] = _bf(xn)
    hi, lo = _split2(xn)
    rwh = rwh_ref[...]
    logits = (lax.dot_general(rwh, hi, NT_DIMS, preferred_element_type=F32)
              + (lax.dot_general(rwh, lo, NT_DIMS, preferred_element_type=F32)
                 + lax.dot_general(rwl_ref[...], hi, NT_DIMS, preferred_element_type=F32))) + rb_ref[...]
    ef = lax.broadcasted_iota(jnp.int32, (E_PAD, TM), 0).astype(F32)
    z = jnp.where(ef < N_EXPERTS, logits, NEG_INF)
    m1 = jnp.max(z, axis=0, keepdims=True)
    i1 = jnp.min(jnp.where(z == m1, ef, float(E_PAD)), axis=0, keepdims=True)
    z2 = jnp.where(ef == i1, NEG_INF, z)
    m2 = jnp.max(z2, axis=0, keepdims=True)
    i2 = jnp.min(jnp.where(z2 == m2, ef, float(E_PAD)), axis=0, keepdims=True)
    e2 = jnp.exp(m2 - m1)
    den = 1.0 + e2
    member = jnp.where((ef == i1) | (ef == i2), 1.0, 0.0)
    before = (lax.broadcasted_iota(jnp.int32, (TM, TM), 0) < lax.broadcasted_iota(jnp.int32, (TM, TM), 1))
    rank = jnp.dot(_bf(member), before.astype(BF16), preferred_element_type=F32)
    cnt = jnp.sum(member, axis=1, keepdims=True)
    run = jnp.floor((cnt + (MOE_RUN - 1)) * (1.0 / MOE_RUN)) * MOE_RUN
    run_b = jnp.broadcast_to(run, (E_PAD, LANES))
    lower = (lax.broadcasted_iota(jnp.int32, (E_PAD, E_PAD), 0)
             > lax.broadcasted_iota(jnp.int32, (E_PAD, E_PAD), 1)).astype(BF16)
    off = jnp.dot(lower, _bf(run_b), preferred_element_type=F32)
    slot = off[:, 0:1] + rank
    slot1 = jnp.sum(jnp.where(ef == i1, slot, 0.0), axis=0, keepdims=True)
    slot2 = jnp.sum(jnp.where(ef == i2, slot, 0.0), axis=0, keepdims=True)
    info = jnp.concatenate([slot1, slot2, 1.0 / den, e2 / den, jnp.zeros((E_PAD - 4, TM), F32)], axis=0)
    rows_ref[0] = info
    ident = (lax.broadcasted_iota(jnp.int32, (E_PAD, LANES), 0)
             == lax.broadcasted_iota(jnp.int32, (E_PAD, LANES), 1)).astype(BF16)
    tn_dims = (((0,), (0,)), ((), ()))
    h3, m3, l3 = _split3(info)
    cols_ref[0] = (lax.dot_general(h3, ident, tn_dims, preferred_element_type=F32)
                   + (lax.dot_general(m3, ident, tn_dims, preferred_element_type=F32)
                      + lax.dot_general(l3, ident, tn_dims, preferred_element_type=F32)))
    lane = lax.broadcasted_iota(jnp.int32, (E_PAD, LANES), 1)
    meta_ref[0] = jnp.where(lane == 0, off, jnp.where(lane == 1, run_b, 0.0))


def moe_route(x, g, router):
    M, D = x.shape
    TM = _pick(M, (1024, 512, 256, 128, 64))
    NT = M // TM
    rwh, rwl, rb = router
    full = lambda a: pl.BlockSpec(a.shape, lambda i: (0,) * a.ndim)
    return pl.pallas_call(
        _moe_route_kernel,
        grid=(NT,),
        in_specs=[pl.BlockSpec((TM, D), lambda i: (i, 0)), pl.BlockSpec((1, D), lambda i: (0, 0)),
                  full(rwh), full(rwl), full(rb)],
        out_specs=[pl.BlockSpec((TM, D), lambda i: (i, 0)),
                   pl.BlockSpec((1, E_PAD, TM), lambda i: (i, 0, 0)),
                   pl.BlockSpec((1, TM, LANES), lambda i: (i, 0, 0)),
                   pl.BlockSpec((1, E_PAD, LANES), lambda i: (i, 0, 0))],
        out_shape=[jax.ShapeDtypeStruct((M, D), BF16),
                   jax.ShapeDtypeStruct((NT, E_PAD, TM), F32),
                   jax.ShapeDtypeStruct((NT, TM, LANES), F32),
                   jax.ShapeDtypeStruct((NT, E_PAD, LANES), F32)],
        compiler_params=_cparams("parallel"),
        name="moe_route",
    )(x, g.reshape(1, D), rwh, rwl, rb)


def _for_each_run_chunk(off_ref, run_ref, e, sizes, body):
    i = pl.program_id(0)
    base = off_ref[i, e]
    n = run_ref[i, e]
    for size in sizes:
        @pl.when((n & size) != 0)
        def _(size=size):
            body(pl.multiple_of(base + (n & (-2 * size)), MOE_RUN), size)


def _moe_up_kernel(off_ref, run_ref, xn_ref, rows_ref, wg_ref, wu_ref, hs_ref, xs_ref, *, sizes, GC):
    j = pl.program_id(1)
    e = pl.program_id(2)
    TM = xn_ref.shape[0]
    CAP = xs_ref.shape[0]

    @pl.when((j == 0) & (e == 0))
    def _():
        s1 = rows_ref[0, 0:1, :]
        s2 = rows_ref[0, 1:2, :]
        xnb = xn_ref[...]
        for c in range(CAP // GC):
            srow = (c * GC + lax.broadcasted_iota(jnp.int32, (GC, TM), 0)).astype(F32)
            onehot = jnp.where((srow == s1) | (srow == s2), 1.0, 0.0)
            xs_ref[c * GC:(c + 1) * GC, :] = _bf(jnp.dot(_bf(onehot), xnb, preferred_element_type=F32))

    @pl.when(e == 0)
    def _():
        hs_ref[...] = jnp.zeros_like(hs_ref)

    def body(start, size):
        xr = xs_ref[pl.ds(start, size), :]
        a = jnp.dot(xr, wg_ref[0], preferred_element_type=F32)
        b = jnp.dot(xr, wu_ref[0], preferred_element_type=F32)
        hs_ref[0, pl.ds(start, size), :] = _bf(a * _sigmoid(a) * b)

    _for_each_run_chunk(off_ref, run_ref, e, sizes, body)


def moe_up_sparse(xn, rows, off, run, wg, wu):
    M, D = xn.shape
    NT, _, TM = rows.shape
    E, _, F = wg.shape
    CAP = _moe_cap(TM)
    tn = _pick(F, (256, 128))
    gs = pltpu.PrefetchScalarGridSpec(
        num_scalar_prefetch=2, grid=(NT, F // tn, E),
        in_specs=[pl.BlockSpec((TM, D), lambda i, j, e, o, r: (i, 0)),
                  pl.BlockSpec((1, E_PAD, TM), lambda i, j, e, o, r: (i, 0, 0)),
                  pl.BlockSpec((1, D, tn), lambda i, j, e, o, r: (e, 0, j)),
                  pl.BlockSpec((1, D, tn), lambda i, j, e, o, r: (e, 0, j))],
        out_specs=pl.BlockSpec((1, CAP, tn), lambda i, j, e, o, r: (i, 0, j)),
        scratch_shapes=[pltpu.VMEM((CAP, D), BF16)])
    return pl.pallas_call(
        functools.partial(_moe_up_kernel, sizes=_run_sizes(TM), GC=_pick(CAP, (512, 256, 128))),
        grid_spec=gs,
        out_shape=jax.ShapeDtypeStruct((NT, CAP, F), BF16),
        compiler_params=_cparams("arbitrary", "arbitrary", "arbitrary"),
        name="moe_up",
    )(off, run, xn, rows, wg, wu)


def _moe_down_kernel(off_ref, run_ref, hs_ref, wd_ref, x_ref, cols_ref, rows_ref, o_ref, ys_ref, *,
                     sizes, GC, n_e, n_kf):
    kf = pl.program_id(2)
    e = pl.program_id(3)
    TM = x_ref.shape[0]
    CAP = ys_ref.shape[0]

    @pl.when((e == 0) & (kf == 0))
    def _():
        ys_ref[...] = jnp.zeros_like(ys_ref)

    def body(start, size):
        ys_ref[pl.ds(start, size), :] += jnp.dot(hs_ref[0, pl.ds(start, size), :], wd_ref[0],
                                                 preferred_element_type=F32)

    _for_each_run_chunk(off_ref, run_ref, e, sizes, body)

    @pl.when((e == n_e - 1) & (kf == n_kf - 1))
    def _():
        info = cols_ref[0]
        s1, s2 = info[:, 0:1], info[:, 1:2]
        r1, r2, p1, p2 = (rows_ref[0, k:k + 1, :] for k in range(4))
        acc = x_ref[...]
        for c in range(CAP // GC):
            srow = (c * GC + lax.broadcasted_iota(jnp.int32, (GC, TM), 0)).astype(F32)
            gate = jnp.sum(jnp.where(srow == r1, p1, 0.0) + jnp.where(srow == r2, p2, 0.0), axis=1, keepdims=True)
            scol = (c * GC + lax.broadcasted_iota(jnp.int32, (TM, GC), 1)).astype(F32)
            onehot = jnp.where((scol == s1) | (scol == s2), 1.0, 0.0)
            acc = acc + jnp.dot(_bf(onehot), _bf(ys_ref[c * GC:(c + 1) * GC, :] * gate), preferred_element_type=F32)
        o_ref[...] = acc


def moe_down_sparse(hs, cols, rows, off, run, wd, x):
    NT, CAP, F = hs.shape
    TM = cols.shape[1]
    E, _, D = wd.shape
    tn = _pick(D, (1024, 512, 256, 128))
    tk = _pick(F, (1408, 1024, 512, 256, 128))
    n_kf = F // tk
    gs = pltpu.PrefetchScalarGridSpec(
        num_scalar_prefetch=2, grid=(NT, D // tn, n_kf, E),
        in_specs=[pl.BlockSpec((1, CAP, tk), lambda i, n, k, e, o, r: (i, 0, k), pipeline_mode=pl.Buffered(1)),
                  pl.BlockSpec((1, tk, tn), lambda i, n, k, e, o, r: (e, k, n)),
                  pl.BlockSpec((TM, tn), lambda i, n, k, e, o, r: (i, n), pipeline_mode=pl.Buffered(1)),
                  pl.BlockSpec((1, TM, LANES), lambda i, n, k, e, o, r: (i, 0, 0)),
                  pl.BlockSpec((1, E_PAD, TM), lambda i, n, k, e, o, r: (i, 0, 0))],
        out_specs=pl.BlockSpec((TM, tn), lambda i, n, k, e, o, r: (i, n)),
        scratch_shapes=[pltpu.VMEM((CAP, tn), F32)])
    return pl.pallas_call(
        functools.partial(_moe_down_kernel, sizes=_run_sizes(TM), GC=_pick(CAP, (512, 256, 128)), n_e=E, n_kf=n_kf),
        grid_spec=gs,
        out_shape=jax.ShapeDtypeStruct(x.shape, F32),
        compiler_params=_cparams("arbitrary", "arbitrary", "arbitrary", "arbitrary"),
        name="moe_down",
    )(off, run, hs, wd, x, cols, rows)


def _rmsnorm_kernel(x_ref, g_ref, o_ref):
    x = x_ref[...]
    ms = jnp.mean(x * x, axis=-1, keepdims=True)
    o_ref[...] = x * lax.rsqrt(ms + NORM_EPS) * g_ref[...]


def rmsnorm(x, g):
    M, D = x.shape
    tm = _pick(M, (512, 256, 128, 8))
    return pl.pallas_call(
        _rmsnorm_kernel,
        grid=(M // tm,),
        in_specs=[pl.BlockSpec((tm, D), lambda i: (i, 0)), pl.BlockSpec((1, D), lambda i: (0, 0))],
        out_specs=pl.BlockSpec((tm, D), lambda i: (i, 0)),
        out_shape=jax.ShapeDtypeStruct((M, D), F32),
        compiler_params=_cparams("parallel"),
        name="final_norm",
    )(x, g.reshape(1, D))


def _rwkv_prep_kernel(p_ref, prev_ref, s0_ref, mu_ref, w0_ref, w2h_ref, w2l_ref, a0_ref, a2h_ref, a2l_ref,
                      g2h_ref, g2l_ref, kk_ref, ka_ref, bd_ref,
                      r_ref, lw_ref, k_ref, v_ref, kn_ref, b_ref, g_ref, *, T):
    p = p_ref[0]
    tm = p.shape[0]
    row = lax.broadcasted_iota(jnp.int32, (tm, 1), 0)
    shifted = jnp.where(row == 0, prev_ref[0, 7:8, :], pltpu.roll(p, 1, 0))
    nseq = s0_ref.shape[0]
    s0_rows = jnp.broadcast_to(s0_ref[...], (nseq, tm // nseq, p.shape[1])).reshape(tm, p.shape[1])
    first = ((pl.program_id(0) * tm + row) % T) == 0
    ps = p + (jnp.where(first, s0_rows, shifted) - p) * mu_ref[...]
    W = A_WIDTH
    r = ps[:, 0:W]
    k = ps[:, W:2 * W]
    v = ps[:, 2 * W:3 * W]
    o = 3 * W
    wd = ps[:, o:o + A_LORA_W]
    ad = ps[:, o + A_LORA_W:o + A_LORA_W + A_LORA_A]
    gd = ps[:, o + A_LORA_W + A_LORA_A:]
    z = -(w0_ref[...] + _dot3(jnp.tanh(wd), w2h_ref[...], w2l_ref[...]))
    softplus = jnp.maximum(z, 0.0) + jnp.log(1.0 + jnp.exp(-jnp.abs(z)))
    lw_ref[...] = -jnp.exp(-softplus - 0.5)
    a = _sigmoid(a0_ref[...] + _dot3(ad, a2h_ref[...], a2l_ref[...]))
    g_ref[...] = _dot3(_sigmoid(gd), g2h_ref[...], g2l_ref[...])
    kk = k * kk_ref[...]
    ss = _head_sums(kk * kk, bd_ref[...])
    kn = kk / jnp.maximum(jnp.sqrt(ss), 1e-12)
    r_ref[...] = r
    v_ref[...] = v
    kn_ref[...] = kn
    b_ref[...] = kn * a
    k_ref[...] = k * (1.0 + (a - 1.0) * ka_ref[...])


def rwkv_prep(pab, shift0, prm, T):
    M = pab.shape[1]
    tm = _pick(M, (256, 128, 8))
    assert T % tm == 0 or tm % T == 0
    nseq = max(tm // T, 1)
    s0_map = (lambda i: (i, 0, 0)) if tm >= T else (lambda i: ((i * tm) // T, 0, 0))
    W = A_WIDTH
    row = lambda n: pl.BlockSpec((1, n), lambda i: (0, 0))
    mat = lambda a, b: pl.BlockSpec((a, b), lambda i: (0, 0))
    out = pl.BlockSpec((tm, W), lambda i: (i, 0))
    return pl.pallas_call(
        functools.partial(_rwkv_prep_kernel, T=T),
        grid=(M // tm,),
        in_specs=[pl.BlockSpec((1, tm, A_PROJ), lambda i: (0, i, 0)),
                  pl.BlockSpec((1, 8, A_PROJ), lambda i: (0, jnp.maximum(i * (tm // 8) - 1, 0), 0)),
                  pl.BlockSpec((nseq, 1, A_PROJ), s0_map),
                  row(A_PROJ), row(W), mat(A_LORA_W, W), mat(A_LORA_W, W),
                  row(W), mat(A_LORA_A, W), mat(A_LORA_A, W),
                  mat(A_LORA_G, W), mat(A_LORA_G, W), row(W), row(W), mat(LANES, LANES)],
        out_specs=[out] * 7,
        out_shape=[jax.ShapeDtypeStruct((M, W), F32)] * 7,
        compiler_params=_cparams("parallel"),
        name="rwkv_prep",
    )(pab, pab, shift0[:, None, :], prm["mu"], prm["w0"], *prm["w2"], prm["a0"], *prm["a2"], *prm["g2"],
      prm["k_k"], prm["k_a"], prm["bd"])


def _rwkv_chunk_kernel(r_ref, lw_ref, k_ref, v_ref, kn_ref, b_ref, s0_ref, y_ref, s_ref, *, C, nc, Bb):
    C2 = 2 * C
    lane = lax.broadcasted_iota(jnp.int32, (C2, LANES), 1)
    rowi = lax.broadcasted_iota(jnp.int32, (C2, LANES), 0)
    mask2 = ((rowi >= C) == (lane >= HEAD_DIM)).astype(F32)
    ri = lax.broadcasted_iota(jnp.int32, (C2, C2), 0)
    ci = lax.broadcasted_iota(jnp.int32, (C2, C2), 1)
    same = (ri >= C) == (ci >= C)
    strict = same & (ri > ci)
    incl = same & (ri >= ci)
    eye = (ri == ci).astype(F32)
    ti = lax.broadcasted_iota(jnp.int32, (C, C), 0)
    tj = lax.broadcasted_iota(jnp.int32, (C, C), 1)
    tri = (ti >= tj).astype(BF16)
    n_levels = max(C.bit_length() - 2, 0)
    merged = C2 % LANES == 0
    streams = range(Bb)

    def stack(x):
        return jnp.concatenate([x, x], axis=0) * mask2

    def each(f, *lists):
        return [f(*[l[i] for l in lists]) for i in streams]

    @pl.when(pl.program_id(2) == 0)
    def _():
        s_ref[...] = s0_ref[...]

    def chunk(c, states):
        sl = pl.ds(pl.multiple_of(c * C, C), C)
        S = list(states)
        load = lambda ref: [ref[i, sl, :] for i in streams]
        r, lw, k, v, kn, b = (load(ref) for ref in (r_ref, lw_ref, k_ref, v_ref, kn_ref, b_ref))
        cs = each(lambda x: _dot_exact_lhs3(tri, x), lw)
        gt = each(lambda x: x[C - 1:C, :], cs)
        e_neg = each(lambda x: jnp.exp(-x), cs)
        e_rem = each(lambda g, x: jnp.exp(g - x), gt, cs)
        KT = each(lambda a, x, l: stack(a * jnp.exp(x - l)), kn, cs, lw)
        BI = each(lambda a, e: stack(a * e), b, e_neg)
        KI = each(lambda a, e: stack(a * e), k, e_neg)
        RT = each(lambda a, x: stack(a * jnp.exp(x)), r, cs)
        V2 = each(stack, v)
        KG = each(lambda a, e: stack(a * e), k, e_rem)
        BG = each(lambda a, e: stack(a * e), b, e_rem)
        if merged:
            quad = each(lambda kt, rt, bi, ki: _dot_nt(jnp.concatenate([kt, rt], axis=0),
                                                        jnp.concatenate([bi, ki], axis=0)), KT, RT, BI, KI)
            a_kb = each(lambda q: q[:C2, :C2], quad)
            a_kv = each(lambda q: q[:C2, C2:], quad)
            a_rb = each(lambda q: q[C2:, :C2], quad)
            a_rk = each(lambda q: q[C2:, C2:], quad)
        else:
            a_kb = each(_dot_nt, KT, BI)
            a_kv = each(_dot_nt, KT, KI)
            a_rb = each(_dot_nt, RT, BI)
            a_rk = each(_dot_nt, RT, KI)
        a_rb = each(lambda x: jnp.where(incl, x, 0.0), a_rb)
        a_rk = each(lambda x: jnp.where(incl, x, 0.0), a_rk)
        pw = each(lambda x: -jnp.where(strict, x, 0.0), a_kb)
        tinv = each(lambda x: eye + x, pw)
        if n_levels:
            pw = each(_dot, pw, pw)
            for _ in range(n_levels - 1):
                both = each(lambda p, t: _dot(jnp.concatenate([p, t], axis=0), p), pw, tinv)
                pw = each(lambda x: x[:C2], both)
                tinv = each(lambda t, x: t + x[C2:], tinv, both)
            tinv = each(lambda t, p: t + _dot(t, p), tinv, pw)
        av = each(lambda x, vv: _dot(jnp.where(strict, x, 0.0), vv), a_kv, V2)
        kpw = each(lambda t, kt, a: _dot(t, jnp.concatenate([kt, a], axis=1)), tinv, KT, av)
        corr = each(_dot, a_rb, kpw)
        rp = each(lambda x, c_: x - c_[:, :LANES], RT, corr)
        y1 = each(lambda a, vv, c_: _dot(a, vv) - c_[:, LANES:], a_rk, V2, corr)
        low = each(_dot_tn, kpw, BG)
        mlow = each(lambda x: x[:LANES], low)
        nt = each(lambda vv, kg, x: _dot_tn(vv, kg) - x[LANES:], V2, KG, low)
        y2 = each(lambda p, s, y: _dot_nt(p, s) + y, rp, S, y1)
        for i in streams:
            y_ref[i, sl, :] = y2[i][:C] + y2[i][C:]
        new = each(lambda s, g, m, n: s * jnp.exp(g) - _dot3(s, *_split2(m)) + n, S, gt, mlow, nt)
        return tuple(new)

    final = lax.fori_loop(0, nc, chunk, tuple(s_ref[i, 0] for i in streams))
    for i in streams:
        s_ref[i, 0] = final[i]


def rwkv_chunk(r, lw, k, v, kn, b, s_bd, B, T):
    C = min(T, 64)
    Tt = _pick(T, (256, 128, 64)) if T > C else T
    nc = Tt // C
    Bb = _pick(B, (8, 4, 2, 1))
    HP = A_WIDTH // LANES
    seq = pl.BlockSpec((Bb, Tt, LANES), lambda i, h, t: (i, t, h))
    st = pl.BlockSpec((Bb, 1, LANES, LANES), lambda i, h, t: (i, h, 0, 0))
    r3 = lambda x: x.reshape(B, T, A_WIDTH)
    y, s = pl.pallas_call(
        functools.partial(_rwkv_chunk_kernel, C=C, nc=nc, Bb=Bb),
        grid=(B // Bb, HP, T // Tt),
        in_specs=[seq] * 6 + [st],
        out_specs=[seq, st],
        out_shape=[jax.ShapeDtypeStruct((B, T, A_WIDTH), F32),
                   jax.ShapeDtypeStruct((B, HP, LANES, LANES), F32)],
        compiler_params=_cparams("parallel", "parallel", "arbitrary"),
        name="rwkv_chunk",
    )(r3(r), r3(lw), r3(k), r3(v), r3(kn), r3(b), s_bd)
    return y.reshape(B * T, A_WIDTH), s


def _rwkv_post_kernel(y_ref, r_ref, k_ref, v_ref, g_ref, rk_ref, lnw_ref, lnb_ref, bd_ref, o_ref):
    y = y_ref[...]
    bd = bd_ref[...]
    inv_n = 1.0 / HEAD_DIM
    mean = _head_sums(y, bd) * inv_n
    d = y - mean
    var = _head_sums(d * d, bd) * inv_n
    yn = d * lax.rsqrt(var + RWKV_GN_EPS) * lnw_ref[...] + lnb_ref[...]
    bonus = _head_sums(r_ref[...] * k_ref[...] * rk_ref[...], bd)
    o_ref[...] = _bf((yn + bonus * v_ref[...]) * g_ref[...])


def rwkv_post(y, r, k, v, g, prm):
    M, W = y.shape
    tm = _pick(M, (256, 128, 8))
    blk = pl.BlockSpec((tm, W), lambda i: (i, 0))
    row = pl.BlockSpec((1, W), lambda i: (0, 0))
    return pl.pallas_call(
        _rwkv_post_kernel,
        grid=(M // tm,),
        in_specs=[blk] * 5 + [row] * 3 + [pl.BlockSpec((LANES, LANES), lambda i: (0, 0))],
        out_specs=blk,
        out_shape=jax.ShapeDtypeStruct((M, W), BF16),
        compiler_params=_cparams("parallel"),
        name="rwkv_post",
    )(y, r, k, v, g, prm["r_k"], prm["ln_w"], prm["ln_b"], prm["bd"])


def _logf_kernel(f_ref, b_ref, o_ref):
    z = f_ref[0] + b_ref[...]
    o_ref[...] = jnp.minimum(z, 0.0) - jnp.log(1.0 + jnp.exp(-jnp.abs(z)))


def fox_logf(pab, bias_row, col_block):
    M = pab.shape[1]
    tm = _pick(M, (1024, 512, 256, 128, 8))
    return pl.pallas_call(
        _logf_kernel,
        grid=(M // tm,),
        in_specs=[pl.BlockSpec((1, tm, LANES), lambda i: (1, i, col_block)),
                  pl.BlockSpec((1, LANES), lambda i: (0, 0))],
        out_specs=pl.BlockSpec((tm, LANES), lambda i: (i, 0)),
        out_shape=jax.ShapeDtypeStruct((M, LANES), F32),
        compiler_params=_cparams("parallel"),
        name="fox_logf",
    )(pab, bias_row)


def _cumsum_kernel(x_ref, o_ref, carry_ref):
    @pl.when(pl.program_id(1) == 0)
    def _():
        carry_ref[...] = jnp.zeros_like(carry_ref)

    ti = lax.broadcasted_iota(jnp.int32, (LANES, LANES), 0)
    tj = lax.broadcasted_iota(jnp.int32, (LANES, LANES), 1)
    triu = (ti <= tj).astype(BF16)
    hi, mid, lo = _split3(x_ref[0])
    cs = (jnp.dot(hi, triu, preferred_element_type=F32)
          + (jnp.dot(mid, triu, preferred_element_type=F32)
             + jnp.dot(lo, triu, preferred_element_type=F32))) + carry_ref[...]
    o_ref[0] = cs
    carry_ref[...] = jnp.broadcast_to(cs[:, LANES - 1:LANES], cs.shape)


def cumsum_lanes(xT):
    B, H, L = xT.shape
    return pl.pallas_call(
        _cumsum_kernel,
        grid=(B, L // LANES),
        in_specs=[pl.BlockSpec((1, H, LANES), lambda b, p: (b, 0, p))],
        out_specs=pl.BlockSpec((1, H, LANES), lambda b, p: (b, 0, p)),
        out_shape=jax.ShapeDtypeStruct((B, H, L), F32),
        scratch_shapes=[pltpu.VMEM((H, LANES), F32)],
        compiler_params=_cparams("parallel", "arbitrary"),
        name="fox_cumsum",
    )(xT)


def _head_lane_mask(n_rows, width, head):
    lane = lax.broadcasted_iota(jnp.int32, (n_rows, width), 1)
    return (lane // HEAD_DIM) == head


def _two_pass_attend(qs, n_past, add_past, diag_start, add_diag, kb_ref, vb_ref, s_ref, acc_ref, l_ref, m_ref):
    blk = MOBA_BLOCK
    rep = blk // LANES

    def scores(kb):
        return lax.dot_general(qs, kb, NT_DIMS, preferred_element_type=F32)

    s_d = add_diag(scores(kb_ref[pl.ds(diag_start, blk), :]))
    s_ref[s_ref.shape[0] - 1] = s_d
    l_ref[...] = s_d

    def pass1(n, carry):
        s = add_past(n, scores(kb_ref[pl.ds(pl.multiple_of(n * blk, blk), blk), :]))
        s_ref[n] = s
        l_ref[...] = jnp.maximum(l_ref[...], s)
        return carry

    lax.fori_loop(0, n_past, pass1, 0)
    m_ref[...] = jnp.broadcast_to(jnp.max(l_ref[...], axis=1, keepdims=True), m_ref.shape)

    def probs(n):
        return jnp.exp2(s_ref[n] - jnp.concatenate([m_ref[...]] * rep, axis=1))

    p_d = probs(s_ref.shape[0] - 1)
    l_ref[...] = p_d
    acc_ref[...] = jnp.dot(_bf(p_d), vb_ref[pl.ds(diag_start, blk), :], preferred_element_type=F32)

    def pass2(n, carry):
        p = probs(n)
        l_ref[...] += p
        acc_ref[...] += jnp.dot(_bf(p), vb_ref[pl.ds(pl.multiple_of(n * blk, blk), blk), :],
                                preferred_element_type=F32)
        return carry

    lax.fori_loop(0, n_past, pass2, 0)
    return acc_ref[...] / jnp.sum(l_ref[...], axis=1, keepdims=True)


def _attend_scratch(n_blocks, R):
    return [pltpu.VMEM((n_blocks + 1, R, MOBA_BLOCK), F32),
            pltpu.VMEM((R, GROUP_LANES), F32),
            pltpu.VMEM((R, MOBA_BLOCK), F32),
            pltpu.VMEM((R, LANES), F32)]


def _fox_prompt_kernel(q_ref, k_ref, v_ref, c_ref, o_ref, kb_ref, vb_ref, s_ref, acc_ref, l_ref, m_ref, *, tq, hpg):
    g = pl.program_id(1)
    qi = pl.program_id(2)

    @pl.when(qi == 0)
    def _():
        kb_ref[...] = _bf(k_ref[0])
        vb_ref[...] = _bf(v_ref[0])

    q = q_ref[0] * (HEAD_DIM ** -0.5 * LOG2E)
    qs = _bf(jnp.concatenate([jnp.where(_head_lane_mask(tq, GROUP_LANES, h), q, 0.0) for h in range(hpg)], axis=0))
    causal = lax.broadcasted_iota(jnp.int32, (tq, tq), 0) >= lax.broadcasted_iota(jnp.int32, (tq, tq), 1)

    def add_bias(n, s, mask):
        parts = []
        for h in range(hpg):
            c_row = c_ref[0, pl.ds(g * hpg + h, 1), pl.ds(pl.multiple_of(n * tq, tq), tq)] * LOG2E
            sh = s[h * tq:(h + 1) * tq, :] - c_row
            parts.append(jnp.where(causal, sh, NEG_INF) if mask else sh)
        return jnp.concatenate(parts, axis=0)

    o = _two_pass_attend(qs, qi, lambda n, s: add_bias(n, s, False), pl.multiple_of(qi * tq, tq),
                         lambda s: add_bias(qi, s, True), kb_ref, vb_ref, s_ref, acc_ref, l_ref, m_ref)
    out = jnp.zeros((tq, GROUP_LANES), F32)
    for h in range(hpg):
        out = out + jnp.where(_head_lane_mask(tq, GROUP_LANES, h), o[h * tq:(h + 1) * tq, :], 0.0)
    o_ref[...] = _bf(out)


def fox_prompt(pab, cT, B, T):
    tq = MOBA_BLOCK
    nq = T // tq
    hpg = GROUP_LANES // HEAD_DIM
    G = B_WIDTH // GROUP_LANES
    return pl.pallas_call(
        functools.partial(_fox_prompt_kernel, tq=tq, hpg=hpg),
        grid=(B, G, nq),
        in_specs=[pl.BlockSpec((1, tq, GROUP_LANES), lambda b, g, qi: (1, b * nq + qi, g)),
                  pl.BlockSpec((1, T, GROUP_LANES), lambda b, g, qi: (1, b, G + g)),
                  pl.BlockSpec((1, T, GROUP_LANES), lambda b, g, qi: (1, b, 2 * G + g)),
                  pl.BlockSpec((1, B_HEADS, T), lambda b, g, qi: (b, 0, 0))],
        out_specs=pl.BlockSpec((tq, GROUP_LANES), lambda b, g, qi: (b * nq + qi, g)),
        out_shape=jax.ShapeDtypeStruct((B * T, B_WIDTH), BF16),
        scratch_shapes=[pltpu.VMEM((T, GROUP_LANES), BF16),
                        pltpu.VMEM((T, GROUP_LANES), BF16)] + _attend_scratch(nq - 1, hpg * tq),
        compiler_params=_cparams("parallel", "parallel", "arbitrary"),
        name="fox_prompt",
    )(pab, pab, pab, cT)


def _moba_slopes(n_rows, rows_per_head, i, kv_base):
    c = lax.broadcasted_iota(jnp.int32, (n_rows, 1), 0) // rows_per_head
    head = C_REP * (kv_base + c) + i
    return jnp.exp2(-8.0 * (head + 1).astype(F32) / C_HEADS)


def _top_blocks(z, live, idxf, axis=1):
    sel = jnp.zeros(z.shape, F32)
    for _ in range(MOBA_TOPK):
        m = jnp.max(z, axis=axis, keepdims=True)
        idx = jnp.min(jnp.where((z == m) & live, idxf, float(LANES)), axis=axis, keepdims=True)
        pick = idxf == idx
        sel = jnp.where(pick, 1.0, sel)
        z = jnp.where(pick, NEG_INF, z)
    return sel


def _moba_prompt_kernel(q0_ref, q1_ref, q2_ref, q3_ref, k_ref, v_ref, o_ref,
                        kb_ref, vb_ref, km_ref, s_ref, acc_ref, l_ref, m_ref, *, tq, nb, cpg):
    g = pl.program_id(1)
    qi = pl.program_id(2)
    R = cpg * tq
    blk = MOBA_BLOCK

    @pl.when(qi == 0)
    def _():
        km_ref[...] = jnp.zeros_like(km_ref)
        for n in range(nb):
            kblk = k_ref[0, n * blk:(n + 1) * blk, :]
            kb_ref[n * blk:(n + 1) * blk, :] = _bf(kblk)
            vb_ref[n * blk:(n + 1) * blk, :] = _bf(v_ref[0, n * blk:(n + 1) * blk, :])
            km_ref[n:n + 1, :] = jnp.sum(kblk, axis=0, keepdims=True) * (1.0 / blk)

    own = (qi * tq) // blk
    row_tok = lax.broadcasted_iota(jnp.int32, (R, 1), 0) % tq
    qpos = (qi * tq + row_tok).astype(F32)
    koff = lax.broadcasted_iota(jnp.int32, (R, blk), 1)
    causal = (qi * tq - own * blk + row_tok) >= koff
    blockf = lax.broadcasted_iota(jnp.int32, (NB_PAD, R), 0).astype(F32)
    past = blockf < own
    km = km_ref[0:NB_PAD, :]
    spread_row = lax.broadcasted_iota(jnp.int32, (NB_PAD, LANES), 0)
    for i, q_ref in enumerate((q0_ref, q1_ref, q2_ref, q3_ref)):
        q = q_ref[0] * (HEAD_DIM ** -0.5 * LOG2E)
        qs = jnp.concatenate([jnp.where(_head_lane_mask(tq, GROUP_LANES, c), q, 0.0) for c in range(cpg)], axis=0)
        gate = _dot3_nt(km, *_split2(qs))
        sel_bf = _bf(_top_blocks(jnp.where(past, gate, NEG_INF), past, blockf, axis=0))
        slope = _moba_slopes(R, tq, i, g * cpg) * LOG2E
        b0 = slope * koff.astype(F32)
        slope_rep = jnp.broadcast_to(slope, (R, LANES))
        sq_rep = slope_rep * qpos
        own_term = b0 + slope * ((own * blk).astype(F32) - qpos)

        def add_past(n, s, b0=b0, sel_bf=sel_bf, slope_rep=slope_rep, sq_rep=sq_rep):
            chosen = _dot_tn(sel_bf, (spread_row == n).astype(BF16)) > 0.5
            rt = jnp.where(chosen, slope_rep * (n * blk).astype(F32) - sq_rep, NEG_INF)
            return s + b0 + jnp.concatenate([rt] * (blk // LANES), axis=1)

        def add_own(s, own_term=own_term):
            return jnp.where(causal, s + own_term, NEG_INF)

        o = _two_pass_attend(_bf(qs), own, add_past, pl.multiple_of(own * blk, blk), add_own,
                             kb_ref, vb_ref, s_ref, acc_ref, l_ref, m_ref)
        out = jnp.zeros((tq, GROUP_LANES), F32)
        for c in range(cpg):
            out = out + jnp.where(_head_lane_mask(tq, GROUP_LANES, c), o[c * tq:(c + 1) * tq, :], 0.0)
        o_ref[i] = _bf(out)


def moba_prompt(p1, B, T):
    tq = MOBA_BLOCK
    nq = T // tq
    nb = T // MOBA_BLOCK
    assert nb <= NB_PAD
    cpg = GROUP_LANES // HEAD_DIM
    G = C_KV_WIDTH // GROUP_LANES
    R = cpg * tq
    qspec = lambda i: pl.BlockSpec((1, tq, GROUP_LANES), lambda b, g, qi, i=i: (i, b * nq + qi, g))
    return pl.pallas_call(
        functools.partial(_moba_prompt_kernel, tq=tq, nb=nb, cpg=cpg),
        grid=(B, G, nq),
        in_specs=[qspec(0), qspec(1), qspec(2), qspec(3),
                  pl.BlockSpec((1, T, GROUP_LANES), lambda b, g, qi: (4, b, g)),
                  pl.BlockSpec((1, T, GROUP_LANES), lambda b, g, qi: (5, b, g))],
        out_specs=pl.BlockSpec((C_REP, tq, GROUP_LANES), lambda b, g, qi: (0, b * nq + qi, g)),
        out_shape=jax.ShapeDtypeStruct((C_REP, B * T, C_KV_WIDTH), BF16),
        scratch_shapes=[pltpu.VMEM((T, GROUP_LANES), BF16),
                        pltpu.VMEM((T, GROUP_LANES), BF16),
                        pltpu.VMEM((LANES, GROUP_LANES), F32)] + _attend_scratch(nb - 1, R),
        compiler_params=_cparams("parallel", "parallel", "arbitrary"),
        name="moba_prompt",
    )(p1, p1, p1, p1, p1, p1)


def _block_diag_rows(x, n_heads):
    lane = lax.broadcasted_iota(jnp.int32, x.shape, 1) // HEAD_DIM
    return jnp.concatenate([jnp.where(lane == h, x, 0.0) for h in range(n_heads)], axis=0)


def _gather_heads(out, n_heads, T):
    lane = lax.broadcasted_iota(jnp.int32, (T, out.shape[1]), 1) // HEAD_DIM
    y = jnp.zeros((T, out.shape[1]), F32)
    for h in range(n_heads):
        y = y + jnp.where(lane == h, out[h * T:(h + 1) * T, :], 0.0)
    return y


def _pad_rows(x, n):
    return jnp.concatenate([x, jnp.zeros((n - x.shape[0], x.shape[1]), x.dtype)], axis=0)


def _softmax_pv(s_list, s_new, v_refs, v_new, width):
    m = jnp.max(s_new, axis=1, keepdims=True)
    for s in s_list:
        m = jnp.maximum(m, jnp.max(s, axis=1, keepdims=True))
    p = jnp.exp(s_new - m)
    l = jnp.sum(p, axis=1, keepdims=True)
    acc = jnp.dot(_bf(p), v_new, preferred_element_type=F32)
    for s, v_ref in zip(s_list, v_refs):
        p = jnp.exp(s - m)
        l = l + jnp.sum(p, axis=1, keepdims=True)
        acc = acc + _dot_nt(p, v_ref[0].reshape(width, v_ref.shape[-1]))
    return acc / l


def _fox_decode_kernel(pt_ref, q_ref, kn_ref, vn_ref, lfn_ref, *rest, T, NP):
    k_refs = rest[0:NP]
    v_refs = rest[NP:2 * NP]
    lf_refs = rest[2 * NP:3 * NP]
    o_ref = rest[3 * NP]
    H, W = B_HEADS, B_WIDTH
    R = H * T
    page = k_refs[0].shape[-1]
    qbd = _bf(_block_diag_rows(q_ref[0] * (HEAD_DIM ** -0.5), H))
    ti = lax.broadcasted_iota(jnp.int32, (page, page), 0)
    tj = lax.broadcasted_iota(jnp.int32, (page, page), 1)
    triu = (ti <= tj).astype(BF16)

    def cumsum(x, carry):
        hi, mid, lo = _split3(x)
        cs = (jnp.dot(hi, triu, preferred_element_type=F32)
              + (jnp.dot(mid, triu, preferred_element_type=F32)
                 + jnp.dot(lo, triu, preferred_element_type=F32))) + carry
        return cs, jnp.broadcast_to(cs[:, page - 1:page], cs.shape)

    def per_row(c):
        return jnp.broadcast_to(c[:, None, :], (H, T, page)).reshape(R, page)

    carry = jnp.zeros((H, page), F32)
    s_list = []
    for j in range(NP):
        cs, carry = cumsum(lf_refs[j][0], carry)
        s = jnp.dot(qbd, _bf(k_refs[j][0].reshape(W, page)), preferred_element_type=F32)
        s_list.append(s - per_row(cs))
    cs_new, _ = cumsum(lfn_ref[0], carry)
    row_tok = lax.broadcasted_iota(jnp.int32, (R, page), 0) % T
    key = lax.broadcasted_iota(jnp.int32, (R, page), 1)
    s_new = _dot_nt(qbd, _pad_rows(kn_ref[0], page)) - per_row(cs_new)
    s_new = jnp.where(key <= row_tok, s_new, NEG_INF)
    out = _softmax_pv(s_list, s_new, v_refs, _bf(_pad_rows(vn_ref[0], page)), W)
    o_ref[0] = _bf(_gather_heads(out, H, T))


def fox_decode(pab, lfn_t, cache_k, cache_v, cache_lf, page_table, T):
    B, NP = page_table.shape
    W = B_WIDTH
    page4 = lambda j: (lambda b, pt: (pt[b, j], 0, 0, 0))
    page3 = lambda j: (lambda b, pt: (pt[b, j], 0, 0))
    in_specs = [pl.BlockSpec((1, T, W), lambda b, pt: (1, b, 0)),
                pl.BlockSpec((1, T, W), lambda b, pt: (1, b, 1)),
                pl.BlockSpec((1, T, W), lambda b, pt: (1, b, 2)),
                pl.BlockSpec((1, *lfn_t.shape[1:]), lambda b, pt: (b, 0, 0))]
    in_specs += [pl.BlockSpec((1, *cache_k.shape[1:]), page4(j)) for j in range(NP)] * 2
    in_specs += [pl.BlockSpec((1, *cache_lf.shape[1:]), page3(j)) for j in range(NP)]
    gs = pltpu.PrefetchScalarGridSpec(
        num_scalar_prefetch=1, grid=(B,), in_specs=in_specs,
        out_specs=pl.BlockSpec((1, T, W), lambda b, pt: (b, 0, 0)))
    return pl.pallas_call(
        functools.partial(_fox_decode_kernel, T=T, NP=NP),
        grid_spec=gs,
        out_shape=jax.ShapeDtypeStruct((B, T, W), BF16),
        compiler_params=_cparams("arbitrary"),
        name="fox_decode",
    )(page_table, pab, pab, pab, lfn_t, *([cache_k] * NP), *([cache_v] * NP), *([cache_lf] * NP))


def _moba_decode_kernel(pt_ref, q_ref, kn_ref, vn_ref, *rest, T, NP, q_start):
    k_refs = rest[0:NP]
    v_refs = rest[NP:2 * NP]
    o_ref = rest[2 * NP]
    Hkv, W = C_KV_HEADS, C_KV_WIDTH
    page = k_refs[0].shape[-1]
    ppb = MOBA_BLOCK // page
    nbp = NP // ppb
    own = q_start // MOBA_BLOCK
    RG = Hkv * T
    R = C_REP * RG

    qbd = jnp.concatenate([_block_diag_rows(q_ref[i] * (HEAD_DIM ** -0.5), Hkv) for i in range(C_REP)], axis=0)
    qbd_bf = _bf(qbd)
    rowi = lax.broadcasted_iota(jnp.int32, (R, 1), 0)
    row_tok = rowi % T
    head = C_REP * ((rowi // T) % Hkv) + rowi // RG
    slope = jnp.exp2(-8.0 * (head + 1).astype(F32) / C_HEADS)
    qpos = (q_start + row_tok).astype(F32)
    lane = lax.broadcasted_iota(jnp.int32, (R, page), 1)
    lanef = lane.astype(F32)

    kps = [k_refs[j][0].reshape(W, page) for j in range(NP)]
    col = lax.broadcasted_iota(jnp.int32, (W, page), 1)
    km = jnp.zeros((W, page), F32)
    for n in range(nbp):
        tot = jnp.sum(kps[n * ppb], axis=1, keepdims=True)
        for j in range(1, ppb):
            tot = tot + jnp.sum(kps[n * ppb + j], axis=1, keepdims=True)
        km = jnp.where(col == n, tot * (1.0 / MOBA_BLOCK), km)
    live = lane < min(own, nbp)
    sel = _top_blocks(jnp.where(live, _dot3(qbd, *_split2(km)), NEG_INF), live, lanef)

    b0 = slope * lanef
    s_list = []
    for n in range(nbp):
        chosen = jnp.max(jnp.where(lane == n, sel, 0.0), axis=1, keepdims=True) > 0.5
        for j in range(n * ppb, (n + 1) * ppb):
            row_term = jnp.where(chosen, slope * (j * page - qpos), NEG_INF)
            s_list.append(jnp.dot(qbd_bf, _bf(kps[j]), preferred_element_type=F32) + b0 + row_term)
    s_new = _dot_nt(qbd_bf, _pad_rows(kn_ref[0], page)) - slope * (row_tok - lane).astype(F32)
    s_new = jnp.where(lane <= row_tok, s_new, NEG_INF)
    out = _softmax_pv(s_list, s_new, v_refs, _bf(_pad_rows(vn_ref[0], page)), W)
    for i in range(C_REP):
        o_ref[i] = _bf(_gather_heads(out[i * RG:(i + 1) * RG, :], Hkv, T))


def moba_decode(p1, cache_k, cache_v, page_table, T, q_start):
    B, NP = page_table.shape
    W = C_KV_WIDTH
    page4 = lambda j: (lambda b, pt: (pt[b, j], 0, 0, 0))
    in_specs = [pl.BlockSpec((C_REP, T, W), lambda b, pt: (0, b, 0)),
                pl.BlockSpec((1, T, W), lambda b, pt: (C_REP, b, 0)),
                pl.BlockSpec((1, T, W), lambda b, pt: (C_REP + 1, b, 0))]
    in_specs += [pl.BlockSpec((1, *cache_k.shape[1:]), page4(j)) for j in range(NP)] * 2
    gs = pltpu.PrefetchScalarGridSpec(
        num_scalar_prefetch=1, grid=(B,), in_specs=in_specs,
        out_specs=pl.BlockSpec((C_REP, T, W), lambda b, pt: (0, b, 0)))
    return pl.pallas_call(
        functools.partial(_moba_decode_kernel, T=T, NP=NP, q_start=q_start),
        grid_spec=gs,
        out_shape=jax.ShapeDtypeStruct((C_REP, B * T, W), BF16),
        compiler_params=_cparams("arbitrary"),
        name="moba_decode",
    )(page_table, p1, p1, p1, *([cache_k] * NP), *([cache_v] * NP))


def _row(v):
    return v.reshape(1, -1).astype(F32)


def _prep_params(norm0_mix_g, w_in0, fox_b_f, rwkv_mu, rwkv_w0, rwkv_w2, rwkv_a0, rwkv_a2, rwkv_g2, rwkv_k_k,
                 rwkv_k_a, rwkv_r_k, rwkv_ln_w, rwkv_ln_b, w_out0, norm0_ffn_g, ffn_w_gate, ffn_w_up,
                 ffn_w_down, norm1_mix_g, w_in1, w_out1, norm1_ffn_g, router_w, router_b, moe_w_gate,
                 moe_w_up, moe_w_down, norm_final_g):
    D = w_in0.shape[0]
    pad_b = A_PROJ - (3 * B_WIDTH + B_HEADS)
    w0 = jnp.concatenate([w_in0, jnp.zeros((D, pad_b), F32)], axis=1)
    wq = w_in1[:, :C_WIDTH].reshape(D, C_KV_HEADS, C_REP, HEAD_DIM).transpose(0, 2, 1, 3).reshape(D, C_WIDTH)
    w1 = jnp.concatenate([wq, w_in1[:, C_WIDTH:]], axis=1)
    wo1 = w_out1.reshape(C_KV_HEADS, C_REP, HEAD_DIM, D).transpose(1, 0, 2, 3).reshape(C_WIDTH, D)
    hd = lax.broadcasted_iota(jnp.int32, (LANES, LANES), 0) // HEAD_DIM
    hd2 = lax.broadcasted_iota(jnp.int32, (LANES, LANES), 1) // HEAD_DIM
    rw = jnp.concatenate([router_w.T, jnp.zeros((E_PAD - N_EXPERTS, D), F32)], axis=0)
    rb = jnp.concatenate([router_b, jnp.zeros((E_PAD - N_EXPERTS,), F32)])
    return dict(
        g0=norm0_mix_g, w0=_bf(w0),
        fox_b=jnp.concatenate([fox_b_f, jnp.zeros((LANES - B_HEADS,), F32)]).reshape(1, LANES),
        rwkv=dict(mu=_row(rwkv_mu), w0=_row(rwkv_w0), w2=_split2(rwkv_w2), a0=_row(rwkv_a0), a2=_split2(rwkv_a2),
                  g2=_split2(rwkv_g2), k_k=_row(rwkv_k_k), k_a=_row(rwkv_k_a), r_k=_row(rwkv_r_k),
                  ln_w=_row(rwkv_ln_w), ln_b=_row(rwkv_ln_b), bd=(hd == hd2).astype(BF16)),
        wo0=_bf(w_out0), g0f=norm0_ffn_g,
        ffn_g=_bf(ffn_w_gate), ffn_u=_bf(ffn_w_up), ffn_d=_bf(ffn_w_down),
        g1=norm1_mix_g, w1=_bf(w1), wo1=_bf(wo1), g1f=norm1_ffn_g,
        router=(*_split2(rw), rb.reshape(E_PAD, 1)),
        moe_g=_bf(moe_w_gate), moe_u=_bf(moe_w_up), moe_d=_bf(moe_w_down),
        gf=norm_final_g)


def _pair_states(S):
    B = S.shape[0]
    S = S.reshape(B, A_HEADS // 2, 2, HEAD_DIM, HEAD_DIM)
    z = jnp.zeros_like(S[:, :, 0])
    top = jnp.concatenate([S[:, :, 0], z], axis=-1)
    bot = jnp.concatenate([z, S[:, :, 1]], axis=-1)
    return jnp.concatenate([top, bot], axis=-2)


def _unpair_states(S):
    B = S.shape[0]
    a = S[:, :, :HEAD_DIM, :HEAD_DIM]
    b = S[:, :, HEAD_DIM:, HEAD_DIM:]
    return jnp.stack([a, b], axis=2).reshape(B, A_HEADS, HEAD_DIM, HEAD_DIM)


def _run(P, x, S0, shift0, caches, page_table):
    B, T, D = x.shape
    M = B * T
    xt = x.reshape(M, D)

    pab = norm_matmul(xt, P["g0"], P["w0"], A_PROJ, A_PROJ // 2)
    r, lw, k, v, kn, bb, gg = rwkv_prep(pab, shift0, P["rwkv"], T)
    s_in = jnp.zeros((B, A_HEADS // 2, LANES, LANES), F32) if S0 is None else _pair_states(S0)
    y, s_bd = rwkv_chunk(r, lw, k, v, kn, bb, s_in, B, T)
    ya = rwkv_post(y, r, k, v, gg, P["rwkv"])
    S_new = _unpair_states(s_bd)
    shift_new = lax.slice(pab, (0, T - 1, 0), (1, M, A_PROJ), (1, T, 1)).reshape(B, A_PROJ)

    cols = lambda c: lax.slice(pab, (1, 0, c * B_WIDTH), (2, M, (c + 1) * B_WIDTH))
    fk = cols(1).reshape(B, T, B_HEADS, HEAD_DIM)
    fv = cols(2).reshape(B, T, B_HEADS, HEAD_DIM)
    logf = fox_logf(pab, P["fox_b"], 3 * B_WIDTH // LANES)[:, :B_HEADS].reshape(B, T, B_HEADS)
    if caches is None:
        yb = fox_prompt(pab, cumsum_lanes(jnp.swapaxes(logf, 1, 2)), B, T)
    else:
        page = caches["fox_k"].shape[1]
        by_head = lambda c: jnp.transpose(c, (0, 2, 3, 1))
        lfn_t = jnp.pad(jnp.swapaxes(logf, 1, 2), ((0, 0), (0, 0), (0, page - T)))
        yb = fox_decode(pab, lfn_t, by_head(caches["fox_k"]), by_head(caches["fox_v"]),
                        jnp.swapaxes(caches["fox_logf"], 1, 2), page_table, T).reshape(M, B_WIDTH)
    yab = jnp.concatenate([ya, yb], axis=1)[None]
    x1 = matmul_res(yab, P["wo0"], xt, tk=A_WIDTH + B_WIDTH)

    h = swiglu_up(x1, P["g0f"], P["ffn_g"], P["ffn_u"])
    x2 = matmul_res(h[None], P["ffn_d"], x1, tk=_pick(h.shape[1], (2816, 1024, 512, 256, 128)))

    p1 = norm_matmul(x2, P["g1"], P["w1"], C_KV_WIDTH, C_KV_WIDTH)
    mk = p1[4].reshape(B, T, C_KV_HEADS, HEAD_DIM)
    mv = p1[5].reshape(B, T, C_KV_HEADS, HEAD_DIM)
    if caches is None:
        y1 = moba_prompt(p1, B, T)
    else:
        page = caches["moba_k"].shape[1]
        by_head = lambda c: jnp.transpose(c, (0, 2, 3, 1))
        y1 = moba_decode(p1, by_head(caches["moba_k"]), by_head(caches["moba_v"]), page_table, T,
                         page_table.shape[1] * page)
    x3 = matmul_res(y1, P["wo1"], x2, tk=C_KV_WIDTH)

    xn, rows, cols, meta = moe_route(x3, P["g1f"], P["router"])
    off = meta[:, :N_EXPERTS, 0].astype(jnp.int32)
    run = meta[:, :N_EXPERTS, 1].astype(jnp.int32)
    hs = moe_up_sparse(xn, rows, off, run, P["moe_g"], P["moe_u"])
    out = rmsnorm(moe_down_sparse(hs, cols, rows, off, run, P["moe_d"], x3), P["gf"])
    return out.reshape(B, T, D), S_new, shift_new, fk, fv, logf, mk, mv


def kernel(x_prompt, x_sample, state_rwkv_S, state_rwkv_shift, cache_fox_k, cache_fox_v, cache_fox_logf,
           cache_moba_k, cache_moba_v, page_table, norm0_mix_g, w_in0, fox_b_f, rwkv_mu, rwkv_w0, rwkv_w2,
           rwkv_a0, rwkv_a2, rwkv_g2, rwkv_k_k, rwkv_k_a, rwkv_r_k, rwkv_ln_w, rwkv_ln_b, w_out0, norm0_ffn_g,
           ffn_w_gate, ffn_w_up, ffn_w_down, norm1_mix_g, w_in1, w_out1, norm1_ffn_g, router_w, router_b,
           moe_w_gate, moe_w_up, moe_w_down, norm_final_g):
    P = _prep_params(norm0_mix_g, w_in0, fox_b_f, rwkv_mu, rwkv_w0, rwkv_w2, rwkv_a0, rwkv_a2, rwkv_g2,
                     rwkv_k_k, rwkv_k_a, rwkv_r_k, rwkv_ln_w, rwkv_ln_b, w_out0, norm0_ffn_g, ffn_w_gate,
                     ffn_w_up, ffn_w_down, norm1_mix_g, w_in1, w_out1, norm1_ffn_g, router_w, router_b,
                     moe_w_gate, moe_w_up, moe_w_down, norm_final_g)
    n_prompt = x_prompt.shape[0]
    prompt = _run(P, x_prompt, None, jnp.zeros((n_prompt, A_PROJ), x_prompt.dtype), None, None)
    caches = dict(fox_k=cache_fox_k, fox_v=cache_fox_v, fox_logf=cache_fox_logf,
                  moba_k=cache_moba_k, moba_v=cache_moba_v)
    sample = _run(P, x_sample, state_rwkv_S, state_rwkv_shift, caches, page_table)
    return (prompt[0], sample[0], *prompt[1:], *sample[1:])
```
